```python
import math
import jax, jax.numpy as jnp
from jax import lax
import numpy as np

D_MODEL = 4096
BATCH = 8
SEQ = 4096
DEPTH = 4

N_META = 16
D_MIX = D_MODEL
MLA_HEADS = D_MODEL // 256
MLA_NOPE = 128
MLA_ROPE = 64
MLA_V = 128
Q_LORA = D_MODEL // 4
KV_LORA = D_MODEL // 8
SB_HEADS = D_MODEL // 256
SB_HEAD_DIM = 128
W_MLA = MLA_HEADS * MLA_V
W_SB = SB_HEADS * SB_HEAD_DIM
BLOCK_Q = 128
ROPE_THETA = 10000.0
EPS = 1e-6
MLA_SCALE = 1.0 / math.sqrt(MLA_NOPE + MLA_ROPE)
SB_SCALE = 1.0 / math.sqrt(SB_HEAD_DIM)
IN_SIZES = (Q_LORA, KV_LORA, MLA_ROPE, W_MLA, W_SB, W_SB, W_SB, W_SB)
D_IN = Q_LORA + KV_LORA + MLA_ROPE + W_MLA + 4 * W_SB

kernel_name = "hymba_mla_stickbreaking_hybrid"


def rms_norm(x, g):
    xf = x.astype(jnp.float32)
    y = xf * lax.rsqrt(jnp.mean(xf * xf, axis=-1, keepdims=True) + EPS)
    return (y * g.astype(jnp.float32)).astype(x.dtype)


def rope_tables(pos, dtype):
    inv_freq = ROPE_THETA ** (-jnp.arange(0, MLA_ROPE, 2, dtype=jnp.float32) / MLA_ROPE)
    ang = pos.astype(jnp.float32)[:, None] * inv_freq[None, :]
    return jnp.cos(ang).astype(dtype), jnp.sin(ang).astype(dtype)


def apply_rope(x, cos, sin):
    half = x.shape[-1] // 2
    x1, x2 = x[..., :half], x[..., half:]
    return jnp.concatenate([x1 * cos - x2 * sin, x2 * cos + x1 * sin], axis=-1)


def mla_block(q_nope, q_rope, k_nope, k_rope, v, q_pos, k_pos):
    s = (jnp.einsum('bqhd,bkhd->bhqk', q_nope, k_nope, preferred_element_type=jnp.float32)
         + jnp.einsum('bqhr,bkr->bhqk', q_rope, k_rope, preferred_element_type=jnp.float32))
    s = s * MLA_SCALE
    causal = k_pos[None, :] <= q_pos[:, None]
    s = jnp.where(causal, s, -jnp.inf)
    p = jax.nn.softmax(s, axis=-1)
    return jnp.einsum('bhqk,bkhd->bqhd', p.astype(v.dtype), v)


def sb_block(q, k, v, q_pos, k_pos):
    z = jnp.einsum('bqhd,bkhd->bhqk', q, k, preferred_element_type=jnp.float32) * SB_SCALE
    strict = k_pos[None, :] < q_pos[:, None]
    log_beta = jax.nn.log_sigmoid(z)
    log_one_minus = jnp.where(strict, jax.nn.log_sigmoid(-z), 0.0)
    log_remaining = lax.cumsum(log_one_minus, axis=3, reverse=True) - log_one_minus
    a = jnp.where(strict, jnp.exp(log_beta + log_remaining), 0.0)
    return jnp.einsum('bhqk,bkhd->bqhd', a.astype(v.dtype), v)


def blocked_attention(block_fn, qs, kvs, pos):
    B, L = qs[0].shape[0], qs[0].shape[1]
    meta_out = block_fn(*[a[:, :N_META] for a in qs], *[a[:, :N_META] for a in kvs],
                        pos[:N_META], pos[:N_META])
    n_blk = (L - N_META) // BLOCK_Q

    def to_blocks(a):
        r = a[:, N_META:].reshape((B, n_blk, BLOCK_Q) + a.shape[2:])
        return jnp.moveaxis(r, 1, 0)

    q_blocks = tuple(to_blocks(a) for a in qs)
    pos_blocks = pos[N_META:].reshape(n_blk, BLOCK_Q)

    def body(args):
        *qb, pb = args
        return block_fn(*qb, *kvs, pb, pos)

    out = lax.map(body, (*q_blocks, pos_blocks))
    out = jnp.moveaxis(out, 0, 1).reshape((B, L - N_META) + out.shape[3:])
    return jnp.concatenate([meta_out, out], axis=1)


def hybrid_layer(h, g_norm, w_in, g_q, g_kv, w_uq, w_ukv, g_out_mla, g_out_sb, w_o,
                 cos, sin, pos):
    B, L, _ = h.shape
    u = rms_norm(h, g_norm)
    proj = u @ w_in
    split_points = tuple(int(v) for v in np.cumsum(IN_SIZES)[:-1])
    c_q, c_kv, k_r, z_mla, q_sb, k_sb, v_sb, z_sb = jnp.split(proj, split_points, axis=-1)

    q = (rms_norm(c_q, g_q) @ w_uq).reshape(B, L, MLA_HEADS, MLA_NOPE + MLA_ROPE)
    q_nope = q[..., :MLA_NOPE]
    q_rope = apply_rope(q[..., MLA_NOPE:], cos[:, None, :], sin[:, None, :])
    kv = (rms_norm(c_kv, g_kv) @ w_ukv).reshape(B, L, MLA_HEADS, MLA_NOPE + MLA_V)
    k_nope, v_mla = kv[..., :MLA_NOPE], kv[..., MLA_NOPE:]
    k_rope = apply_rope(k_r, cos, sin)
    y_mla = blocked_attention(mla_block, (q_nope, q_rope), (k_nope, k_rope, v_mla), pos)
    y_mla = y_mla.reshape(B, L, W_MLA)

    q_s = q_sb.reshape(B, L, SB_HEADS, SB_HEAD_DIM)
    k_s = k_sb.reshape(B, L, SB_HEADS, SB_HEAD_DIM)
    v_s = v_sb.reshape(B, L, SB_HEADS, SB_HEAD_DIM)
    y_sb = blocked_attention(sb_block, (q_s,), (k_s, v_s), pos).reshape(B, L, W_SB)

    y = jnp.concatenate([rms_norm(y_mla, g_out_mla) * jax.nn.silu(z_mla),
                         rms_norm(y_sb, g_out_sb) * jax.nn.silu(z_sb)], axis=-1)
    return h + y @ w_o


def _fwd_setup_inputs(seed: int = 0) -> dict:
    key = jax.random.key(seed)
    ks = jax.random.split(key, 13)
    f32 = jnp.float32
    x = jax.random.normal(ks[0], (BATCH, SEQ, D_MODEL), f32)
    meta_tokens = jax.random.normal(ks[1], (N_META, D_MODEL), f32)
    g_norm = 1.0 + 0.02 * jax.random.normal(ks[2], (DEPTH, D_MODEL), f32)
    w_in = jax.random.normal(ks[3], (DEPTH, D_MODEL, D_IN), f32) * D_MODEL ** -0.5
    g_q = 1.0 + 0.02 * jax.random.normal(ks[4], (DEPTH, Q_LORA), f32)
    g_kv = 1.0 + 0.02 * jax.random.normal(ks[5], (DEPTH, KV_LORA), f32)
    w_uq = jax.random.normal(ks[6], (DEPTH, Q_LORA, MLA_HEADS * (MLA_NOPE + MLA_ROPE)), f32) * Q_LORA ** -0.5
    w_ukv = jax.random.normal(ks[7], (DEPTH, KV_LORA, MLA_HEADS * (MLA_NOPE + MLA_V)), f32) * KV_LORA ** -0.5
    g_out_mla = 1.0 + 0.02 * jax.random.normal(ks[8], (DEPTH, W_MLA), f32)
    g_out_sb = 1.0 + 0.02 * jax.random.normal(ks[9], (DEPTH, W_SB), f32)
    w_o = jax.random.normal(ks[10], (DEPTH, D_MIX, D_MODEL), f32) * D_MIX ** -0.5
    g_final = 1.0 + 0.02 * jax.random.normal(ks[11], (D_MODEL,), f32)
    return {"x": x, "meta_tokens": meta_tokens, "g_norm": g_norm, "w_in": w_in,
            "g_q": g_q, "g_kv": g_kv, "w_uq": w_uq, "w_ukv": w_ukv,
            "g_out_mla": g_out_mla, "g_out_sb": g_out_sb, "w_o": w_o,
            "g_final": g_final}


def _fwd_reference(x, meta_tokens, g_norm, w_in, g_q, g_kv, w_uq, w_ukv, g_out_mla, g_out_sb,
              w_o, g_final):
    B = x.shape[0]
    meta = jnp.broadcast_to(meta_tokens[None].astype(x.dtype), (B, N_META, D_MODEL))
    h = jnp.concatenate([meta, x], axis=1)
    L = h.shape[1]
    pos = jnp.arange(L, dtype=jnp.int32)
    cos, sin = rope_tables(pos, x.dtype)
    for i in range(DEPTH):
        h = hybrid_layer(h, g_norm[i], w_in[i], g_q[i], g_kv[i], w_uq[i], w_ukv[i],
                         g_out_mla[i], g_out_sb[i], w_o[i], cos, sin, pos)
    return rms_norm(h[:, N_META:], g_final)


import jax as _jax
import jax.numpy as _jnp

TWIN_FORMAT = 'train_step'
FWD_PARAMS = ['x', 'meta_tokens', 'g_norm', 'w_in', 'g_q', 'g_kv', 'w_uq', 'w_ukv', 'g_out_mla', 'g_out_sb', 'w_o', 'g_final']
TWIN_WEIGHTS = ['meta_tokens', 'g_norm', 'w_in', 'g_q', 'g_kv', 'w_uq', 'w_ukv', 'g_out_mla', 'g_out_sb', 'w_o', 'g_final']
TWIN_DIFF_INPUT = 'x'
TWIN_INPUTS = ['x', 'meta_tokens', 'g_norm', 'w_in', 'g_q', 'g_kv', 'w_uq', 'w_ukv', 'g_out_mla', 'g_out_sb', 'w_o', 'g_final', 'loss_target', 'm_meta_tokens', 'm_g_norm', 'm_w_in', 'm_g_q', 'm_g_kv', 'm_w_uq', 'm_w_ukv', 'm_g_out_mla', 'm_g_out_sb', 'm_w_o', 'm_g_final', 'v_meta_tokens', 'v_g_norm', 'v_w_in', 'v_g_q', 'v_g_kv', 'v_w_uq', 'v_w_ukv', 'v_g_out_mla', 'v_g_out_sb', 'v_w_o', 'v_g_final']
TWIN_OUTPUTS = ['loss', 'grad_x', 'grad_meta_tokens', 'grad_g_norm', 'grad_w_in', 'grad_g_q', 'grad_g_kv', 'grad_w_uq', 'grad_w_ukv', 'grad_g_out_mla', 'grad_g_out_sb', 'grad_w_o', 'grad_g_final', 'delta_meta_tokens', 'delta_g_norm', 'delta_w_in', 'delta_g_q', 'delta_g_kv', 'delta_w_uq', 'delta_w_ukv', 'delta_g_out_mla', 'delta_g_out_sb', 'delta_w_o', 'delta_g_final', 'new_m_meta_tokens', 'new_m_g_norm', 'new_m_w_in', 'new_m_g_q', 'new_m_g_kv', 'new_m_w_uq', 'new_m_w_ukv', 'new_m_g_out_mla', 'new_m_g_out_sb', 'new_m_w_o', 'new_m_g_final', 'new_v_meta_tokens', 'new_v_g_norm', 'new_v_w_in', 'new_v_g_q', 'new_v_g_kv', 'new_v_w_uq', 'new_v_w_ukv', 'new_v_g_out_mla', 'new_v_g_out_sb', 'new_v_w_o', 'new_v_g_final']
TWIN_LEAF_KINDS = {'loss': 'loss', 'grad_x': 'grad_x', 'grad_meta_tokens': 'grad_w', 'grad_g_norm': 'grad_w', 'grad_w_in': 'grad_w', 'grad_g_q': 'grad_w', 'grad_g_kv': 'grad_w', 'grad_w_uq': 'grad_w', 'grad_w_ukv': 'grad_w', 'grad_g_out_mla': 'grad_w', 'grad_g_out_sb': 'grad_w', 'grad_w_o': 'grad_w', 'grad_g_final': 'grad_w', 'delta_meta_tokens': 'delta_w', 'delta_g_norm': 'delta_w', 'delta_w_in': 'delta_w', 'delta_g_q': 'delta_w', 'delta_g_kv': 'delta_w', 'delta_w_uq': 'delta_w', 'delta_w_ukv': 'delta_w', 'delta_g_out_mla': 'delta_w', 'delta_g_out_sb': 'delta_w', 'delta_w_o': 'delta_w', 'delta_g_final': 'delta_w', 'new_m_meta_tokens': 'new_m', 'new_m_g_norm': 'new_m', 'new_m_w_in': 'new_m', 'new_m_g_q': 'new_m', 'new_m_g_kv': 'new_m', 'new_m_w_uq': 'new_m', 'new_m_w_ukv': 'new_m', 'new_m_g_out_mla': 'new_m', 'new_m_g_out_sb': 'new_m', 'new_m_w_o': 'new_m', 'new_m_g_final': 'new_m', 'new_v_meta_tokens': 'new_v', 'new_v_g_norm': 'new_v', 'new_v_w_in': 'new_v', 'new_v_g_q': 'new_v', 'new_v_g_kv': 'new_v', 'new_v_w_uq': 'new_v', 'new_v_w_ukv': 'new_v', 'new_v_g_out_mla': 'new_v', 'new_v_g_out_sb': 'new_v', 'new_v_w_o': 'new_v', 'new_v_g_final': 'new_v'}


def _forward(args):
    return _fwd_reference(*[args[k] for k in FWD_PARAMS])


def _output_shape():
    out = _jax.eval_shape(lambda: _forward(_fwd_setup_inputs(0)))
    return out.shape, out.dtype

N_MICROBATCH = 1
ADAM_LR = 0.001
ADAM_B1 = 0.9
ADAM_B2 = 0.999
ADAM_EPS = 1e-08
ADAM_WD = 0.01
ADAM_STEP = 10
PER_EXAMPLE_BATCH_AXIS = {'x': 0, 'loss_target': 0}
SHARED_INPUTS = []
_WEIGHT_DTYPES = {'meta_tokens': _jnp.float32, 'g_norm': _jnp.float32, 'w_in': _jnp.float32, 'g_q': _jnp.float32, 'g_kv': _jnp.float32, 'w_uq': _jnp.float32, 'w_ukv': _jnp.float32, 'g_out_mla': _jnp.float32, 'g_out_sb': _jnp.float32, 'w_o': _jnp.float32, 'g_final': _jnp.float32}
MOMENT_SCALE = {'meta_tokens': 6.271278e-03, 'g_norm': 4.415056e-02, 'w_in': 2.596483e-02, 'g_q': 3.150631e-02, 'g_kv': 7.452739e-02, 'w_uq': 1.800433e-02, 'w_ukv': 2.389915e-02, 'g_out_mla': 2.909860e-02, 'g_out_sb': 2.496409e-02, 'w_o': 2.627750e-02, 'g_final': 7.985521e+00}


def _to_microbatches(a, axis):
    t = _jnp.moveaxis(a, axis, 0)
    t = t.reshape((N_MICROBATCH, t.shape[0] // N_MICROBATCH) + t.shape[1:])
    return _jnp.moveaxis(t, 1, axis + 1)


def setup_inputs(seed: int = 0) -> dict:
    inp = _fwd_setup_inputs(seed)
    key = _jax.random.fold_in(_jax.random.key(seed), 7919)
    shape, _ = _output_shape()
    out = dict(inp)
    out["loss_target"] = _jax.random.normal(_jax.random.fold_in(key, 0), shape, _jnp.float32)
    for i, name in enumerate(TWIN_WEIGHTS):
        w = inp[name].astype(_jnp.float32)
        if MOMENT_SCALE is None:
            s = _jnp.sqrt(_jnp.mean(_jnp.square(w)) + 1e-30)
        else:
            s = MOMENT_SCALE[name]
        km, kv = _jax.random.split(_jax.random.fold_in(key, i + 1))
        out[name] = w
        out["m_" + name] = s * _jax.random.normal(km, w.shape, _jnp.float32)
        out["v_" + name] = (s * s) * _jax.random.uniform(kv, w.shape, _jnp.float32, 0.5, 1.5)
    if N_MICROBATCH > 1:
        for name, axis in PER_EXAMPLE_BATCH_AXIS.items():
            out[name] = _to_microbatches(out[name], axis)
    return {'x': out['x'], 'meta_tokens': out['meta_tokens'], 'g_norm': out['g_norm'], 'w_in': out['w_in'], 'g_q': out['g_q'], 'g_kv': out['g_kv'], 'w_uq': out['w_uq'], 'w_ukv': out['w_ukv'], 'g_out_mla': out['g_out_mla'], 'g_out_sb': out['g_out_sb'], 'w_o': out['w_o'], 'g_final': out['g_final'], 'loss_target': out['loss_target'], 'm_meta_tokens': out['m_meta_tokens'], 'm_g_norm': out['m_g_norm'], 'm_w_in': out['m_w_in'], 'm_g_q': out['m_g_q'], 'm_g_kv': out['m_g_kv'], 'm_w_uq': out['m_w_uq'], 'm_w_ukv': out['m_w_ukv'], 'm_g_out_mla': out['m_g_out_mla'], 'm_g_out_sb': out['m_g_out_sb'], 'm_w_o': out['m_w_o'], 'm_g_final': out['m_g_final'], 'v_meta_tokens': out['v_meta_tokens'], 'v_g_norm': out['v_g_norm'], 'v_w_in': out['v_w_in'], 'v_g_q': out['v_g_q'], 'v_g_kv': out['v_g_kv'], 'v_w_uq': out['v_w_uq'], 'v_w_ukv': out['v_w_ukv'], 'v_g_out_mla': out['v_g_out_mla'], 'v_g_out_sb': out['v_g_out_sb'], 'v_w_o': out['v_w_o'], 'v_g_final': out['v_g_final']}


def _loss(weights, diff, rest, loss_target):
    with _jax.named_scope("forward"):
        args = {**rest, TWIN_DIFF_INPUT: diff, **{k: w.astype(_WEIGHT_DTYPES[k]) for k, w in weights.items()}}
        y = _forward(args)
    with _jax.named_scope("loss_head"):
        err = _jnp.square(y.astype(_jnp.float32) - loss_target)
        return 0.5 * _jnp.sum(_jnp.mean(err, axis=-1)) if err.ndim else 0.5 * err


def _adamw(w, g, m, v):
    m = ADAM_B1 * m + (1.0 - ADAM_B1) * g
    v = ADAM_B2 * v + (1.0 - ADAM_B2) * _jnp.square(g)
    m_hat = m / (1.0 - ADAM_B1 ** ADAM_STEP)
    v_hat = v / (1.0 - ADAM_B2 ** ADAM_STEP)
    delta = -ADAM_LR * (m_hat / (_jnp.sqrt(v_hat) + ADAM_EPS) + ADAM_WD * w)
    return delta, m, v


def reference(x, meta_tokens, g_norm, w_in, g_q, g_kv, w_uq, w_ukv, g_out_mla, g_out_sb, w_o, g_final, loss_target, m_meta_tokens, m_g_norm, m_w_in, m_g_q, m_g_kv, m_w_uq, m_w_ukv, m_g_out_mla, m_g_out_sb, m_w_o, m_g_final, v_meta_tokens, v_g_norm, v_w_in, v_g_q, v_g_kv, v_w_uq, v_w_ukv, v_g_out_mla, v_g_out_sb, v_w_o, v_g_final):
    given = dict(x=x, meta_tokens=meta_tokens, g_norm=g_norm, w_in=w_in, g_q=g_q, g_kv=g_kv, w_uq=w_uq, w_ukv=w_ukv, g_out_mla=g_out_mla, g_out_sb=g_out_sb, w_o=w_o, g_final=g_final, loss_target=loss_target, m_meta_tokens=m_meta_tokens, m_g_norm=m_g_norm, m_w_in=m_w_in, m_g_q=m_g_q, m_g_kv=m_g_kv, m_w_uq=m_w_uq, m_w_ukv=m_w_ukv, m_g_out_mla=m_g_out_mla, m_g_out_sb=m_g_out_sb, m_w_o=m_w_o, m_g_final=m_g_final, v_meta_tokens=v_meta_tokens, v_g_norm=v_g_norm, v_w_in=v_w_in, v_g_q=v_g_q, v_g_kv=v_g_kv, v_w_uq=v_w_uq, v_w_ukv=v_w_ukv, v_g_out_mla=v_g_out_mla, v_g_out_sb=v_g_out_sb, v_w_o=v_w_o, v_g_final=v_g_final)
    weights = {n: given[n] for n in TWIN_WEIGHTS}
    shared = {n: given[n] for n in SHARED_INPUTS}
    per_example = {n: given[n] for n in ['x']}
    grad_fn = _jax.value_and_grad(_loss, argnums=(0, 1))

    def one_microbatch(ex, loss_target):
        ex = dict(ex)
        diff = ex.pop(TWIN_DIFF_INPUT)
        return grad_fn(weights, diff, {**shared, **ex}, loss_target)

    if N_MICROBATCH == 1:
        loss, (grad_w, grad_x) = one_microbatch(per_example, given["loss_target"])
    else:
        def body(carry, xs):
            loss_sum, grad_sum = carry
            l_k, (gw_k, gx_k) = one_microbatch(xs[0], xs[1])
            with _jax.named_scope("update"):
                return (loss_sum + l_k, _jax.tree.map(_jnp.add, grad_sum, gw_k)), gx_k

        init = (_jnp.zeros((), _jnp.float32), _jax.tree.map(_jnp.zeros_like, weights))
        (loss, grad_w), grad_x = _jax.lax.scan(body, init, (per_example, given["loss_target"]))
    with _jax.named_scope("update"):
        delta_w, new_m, new_v = {}, {}, {}
        for n in TWIN_WEIGHTS:
            delta_w[n], new_m[n], new_v[n] = _adamw(weights[n], grad_w[n], given["m_" + n], given["v_" + n])
    return (loss, grad_x, *[grad_w[n] for n in TWIN_WEIGHTS], *[delta_w[n] for n in TWIN_WEIGHTS],
            *[new_m[n] for n in TWIN_WEIGHTS], *[new_v[n] for n in TWIN_WEIGHTS])
```

```python
import functools
import math

import jax
import jax.numpy as jnp
from jax import lax
from jax.experimental import pallas as pl
from jax.experimental.pallas import tpu as pltpu

F32 = jnp.float32
BF16 = jnp.bfloat16
MESH = pl.DeviceIdType.MESH

V7X_LANES = 128
VMEM_LIMIT = 56 * 1024 * 1024
VMEM_TILE_BUDGET = 40 * 1024 * 1024

ROPE_DIM = 64
ROPE_THETA = 10000.0
EPS = 1e-6
ROW_ALIGN = 384
ATT_BLK = 128

ADAM_LR = 0.001
ADAM_B1 = 0.9
ADAM_B2 = 0.999
ADAM_EPS = 1e-08
ADAM_WD = 0.01
ADAM_STEP = 10

NEG = -1e30
NT_DIMS = (((1,), (1,)), ((), ()))
TN_DIMS = (((0,), (0,)), ((), ()))
NN_DIMS = (((1,), (0,)), ((), ()))


def _cparams(*sem):
    return pltpu.CompilerParams(dimension_semantics=sem, vmem_limit_bytes=VMEM_LIMIT)


def _divisor_tile(n, cap, align):
    best = None
    t = align
    while t <= min(n, cap):
        if n % t == 0:
            best = t
        t += align
    return best if best is not None else n


def _mm_tiles(M, N, K, a_bytes, b_bytes, o_bytes, has_res):
    best = None
    for tm in sorted({_divisor_tile(M, c, 128) for c in (1408, 1024, 704, 512, 384, 256, 128)}, reverse=True):
        for tn in sorted({_divisor_tile(N, c, 128) for c in (1024, 512, 256, 128)}, reverse=True):
            for tk in sorted({_divisor_tile(K, c, 128) for c in (4096, 2048, 1408, 1024, 704, 512, 384, 256, 128)},
                             reverse=True):
                need = 2 * (tm * tk * a_bytes + tk * tn * b_bytes + tm * tn * o_bytes)
                need += 2 * tm * tn * 4 if has_res else 0
                need += tm * tn * 4 if tk != K else 0
                need += tm * tk * 2 if a_bytes != 2 else 0
                need += tk * tn * 2 if b_bytes != 2 else 0
                need += tm * tn * 4
                if need > VMEM_TILE_BUDGET:
                    continue
                score = (tm * tn / (tm + tn), tk)
                if best is None or score > best[0]:
                    best = (score, (tm, tn, tk))
    assert best is not None, (M, N, K)
    return best[1]


def _matmul(a, b, *, mode, out_dtype, name, residual=None):
    if mode == "nn":
        (M, K), N = a.shape, b.shape[1]
    elif mode == "nt":
        (M, K), N = a.shape, b.shape[0]
    else:
        (K, M), N = a.shape, b.shape[1]
    tm, tn, tk = _mm_tiles(M, N, K, a.dtype.itemsize, b.dtype.itemsize, jnp.dtype(out_dtype).itemsize,
                           residual is not None)
    nk = K // tk
    dims = {"nn": NN_DIMS, "nt": NT_DIMS, "tn": TN_DIMS}[mode]

    def body(*refs):
        if residual is not None:
            a_ref, b_ref, r_ref, o_ref = refs[:4]
        else:
            a_ref, b_ref, o_ref = refs[:3]
            r_ref = None
        part = lax.dot_general(a_ref[...].astype(BF16), b_ref[...].astype(BF16), dims,
                               preferred_element_type=F32)
        if nk == 1:
            if r_ref is not None:
                part = part + r_ref[...]
            o_ref[...] = part.astype(o_ref.dtype)
            return
        acc_ref = refs[-1]
        k = pl.program_id(2)

        @pl.when(k == 0)
        def _():
            acc_ref[...] = part

        @pl.when(k > 0)
        def _():
            acc_ref[...] += part

        @pl.when(k == nk - 1)
        def _():
            r = acc_ref[...]
            if r_ref is not None:
                r = r + r_ref[...]
            o_ref[...] = r.astype(o_ref.dtype)

    if mode == "tn":
        a_spec = pl.BlockSpec((tk, tm), lambda i, j, k: (k, i))
    else:
        a_spec = pl.BlockSpec((tm, tk), lambda i, j, k: (i, k))
    if mode == "nt":
        b_spec = pl.BlockSpec((tn, tk), lambda i, j, k: (j, k))
    else:
        b_spec = pl.BlockSpec((tk, tn), lambda i, j, k: (k, j))
    o_spec = pl.BlockSpec((tm, tn), lambda i, j, k: (i, j))
    in_specs = [a_spec, b_spec]
    args = [a, b]
    if residual is not None:
        in_specs.append(o_spec)
        args.append(residual)
    return pl.pallas_call(
        body, name=name,
        out_shape=jax.ShapeDtypeStruct((M, N), out_dtype),
        grid=(M // tm, N // tn, nk),
        in_specs=in_specs, out_specs=o_spec,
        scratch_shapes=[pltpu.VMEM((tm, tn), F32)] if nk > 1 else [],
        compiler_params=_cparams("parallel", "parallel", "arbitrary"),
    )(*args)


def _row_tile(rows, width, n_arrays):
    cap = max(16, VMEM_TILE_BUDGET // (2 * n_arrays * width * 4))
    return _divisor_tile(rows, min(cap, 384), 16)


def _rms_fwd(x, g, *, col0, out_dtype, name):
    T = x.shape[0]
    W = g.shape[-1]
    assert col0 % W == 0
    cb = col0 // W
    tm = _row_tile(T, W, 3)

    def body(x_ref, g_ref, o_ref):
        xv = x_ref[...].astype(F32)
        r = lax.rsqrt(jnp.mean(xv * xv, axis=-1, keepdims=True) + EPS)
        o_ref[...] = ((xv * r) * g_ref[...]).astype(o_ref.dtype)

    return pl.pallas_call(
        body, name=name,
        out_shape=jax.ShapeDtypeStruct((T, W), out_dtype),
        grid=(T // tm,),
        in_specs=[pl.BlockSpec((tm, W), lambda i: (i, cb)), pl.BlockSpec((1, W), lambda i: (0, 0))],
        out_specs=pl.BlockSpec((tm, W), lambda i: (i, 0)),
        compiler_params=_cparams("parallel"),
    )(x, g.reshape(1, W))


def _rms_bwd(x, g, dy, *, col0, out_dtype, name, residual=None):
    T = x.shape[0]
    W = g.shape[-1]
    assert col0 % W == 0
    cb = col0 // W
    tm = _row_tile(T, W, 6)
    nt = T // tm

    def body(*refs):
        if residual is not None:
            x_ref, g_ref, dy_ref, r_ref, dx_ref, dg_ref, acc_ref = refs
        else:
            x_ref, g_ref, dy_ref, dx_ref, dg_ref, acc_ref = refs
            r_ref = None
        i = pl.program_id(0)
        xv = x_ref[...].astype(F32)
        r = lax.rsqrt(jnp.mean(xv * xv, axis=-1, keepdims=True) + EPS)
        xh = xv * r
        dyv = dy_ref[...].astype(F32)
        dxh = dyv * g_ref[...]
        dx = r * (dxh - xh * jnp.mean(dxh * xh, axis=-1, keepdims=True))
        if r_ref is not None:
            dx = dx + r_ref[...]
        dx_ref[...] = dx.astype(dx_ref.dtype)
        part = jnp.sum((dyv * xh).reshape(tm // 8, 8, W), axis=0)

        @pl.when(i == 0)
        def _():
            acc_ref[...] = part

        @pl.when(i > 0)
        def _():
            acc_ref[...] += part

        @pl.when(i == nt - 1)
        def _():
            dg_ref[...] = jnp.sum(acc_ref[...], axis=0, keepdims=True)

    row = pl.BlockSpec((tm, W), lambda i: (i, 0))
    in_specs = [pl.BlockSpec((tm, W), lambda i: (i, cb)), pl.BlockSpec((1, W), lambda i: (0, 0)), row]
    args = [x, g.reshape(1, W), dy]
    if residual is not None:
        in_specs.append(row)
        args.append(residual)
    return pl.pallas_call(
        body, name=name,
        out_shape=(jax.ShapeDtypeStruct((T, W), out_dtype), jax.ShapeDtypeStruct((1, W), F32)),
        grid=(nt,),
        in_specs=in_specs,
        out_specs=(row, pl.BlockSpec((1, W), lambda i: (0, 0))),
        scratch_shapes=[pltpu.VMEM((8, W), F32)],
        compiler_params=_cparams("arbitrary"),
    )(*args)


def _gate_fwd(o, proj, g, *, zcol0, name):
    T, W = o.shape
    assert zcol0 % W == 0
    zb = zcol0 // W
    tm = _row_tile(T, W, 4)

    def body(o_ref, z_ref, g_ref, y_ref):
        ov = o_ref[...]
        r = lax.rsqrt(jnp.mean(ov * ov, axis=-1, keepdims=True) + EPS)
        z = z_ref[...]
        sg = 1.0 / (1.0 + jnp.exp(-z))
        y_ref[...] = (((ov * r) * g_ref[...]) * (z * sg)).astype(y_ref.dtype)

    return pl.pallas_call(
        body, name=name,
        out_shape=jax.ShapeDtypeStruct((T, W), BF16),
        grid=(T // tm,),
        in_specs=[pl.BlockSpec((tm, W), lambda i: (i, 0)), pl.BlockSpec((tm, W), lambda i: (i, zb)),
                  pl.BlockSpec((1, W), lambda i: (0, 0))],
        out_specs=pl.BlockSpec((tm, W), lambda i: (i, 0)),
        compiler_params=_cparams("parallel"),
    )(o, proj, g.reshape(1, W))


def _gate_bwd(dy, o, proj, g, *, grp, zcol0, name):
    T, W = o.shape
    assert zcol0 % W == 0
    zb = zcol0 // W
    tm = _row_tile(T, W, 8)
    nt = T // tm

    def body(dy_ref, o_ref, z_ref, g_ref, do_ref, dz_ref, dg_ref, acc_ref):
        i = pl.program_id(0)
        ov = o_ref[...]
        r = lax.rsqrt(jnp.mean(ov * ov, axis=-1, keepdims=True) + EPS)
        xh = ov * r
        gv = g_ref[...]
        z = z_ref[...]
        sg = 1.0 / (1.0 + jnp.exp(-z))
        dyv = dy_ref[...]
        dn = dyv * (z * sg)
        dz_ref[...] = (dyv * (xh * gv) * (sg * (1.0 + z * (1.0 - sg)))).astype(dz_ref.dtype)
        dxh = dn * gv
        do_ref[...] = r * (dxh - xh * jnp.mean(dxh * xh, axis=-1, keepdims=True))
        part = jnp.sum((dn * xh).reshape(tm // 8, 8, W), axis=0)

        @pl.when(i == 0)
        def _():
            acc_ref[...] = part

        @pl.when(i > 0)
        def _():
            acc_ref[...] += part

        @pl.when(i == nt - 1)
        def _():
            dg_ref[...] = jnp.sum(acc_ref[...], axis=0, keepdims=True)

    row = pl.BlockSpec((tm, W), lambda i: (i, 0))
    return pl.pallas_call(
        body, name=name,
        out_shape=(jax.ShapeDtypeStruct((T, W), F32), jax.ShapeDtypeStruct((T, W), BF16),
                   jax.ShapeDtypeStruct((1, W), F32)),
        grid=(nt,),
        in_specs=[pl.BlockSpec((tm, W), lambda i: (i, grp)), row, pl.BlockSpec((tm, W), lambda i: (i, zb)),
                  pl.BlockSpec((1, W), lambda i: (0, 0))],
        out_specs=(row, row, pl.BlockSpec((1, W), lambda i: (0, 0))),
        scratch_shapes=[pltpu.VMEM((8, W), F32)],
        compiler_params=_cparams("arbitrary"),
    )(dy, o, proj, g.reshape(1, W))


def _rope_fwd(q, kv, proj, cosp, sinp, *, H, kr_col0, name):
    T = q.shape[0]
    HB = H * V7X_LANES
    tm = ATT_BLK
    krb = kr_col0 // V7X_LANES

    def body(q_ref, kv_ref, kra_ref, krb_ref, cos_ref, sin_ref, qc_ref, kc_ref, v_ref):
        cos = cos_ref[...]
        sin = sin_ref[...]
        kr = (kra_ref[...] * cos + krb_ref[...] * sin).astype(BF16)
        for h in range(H):
            lo, hi = h * 128, (h + 1) * 128
            qc_ref[:, 2 * lo:2 * lo + 128] = q_ref[:, lo:hi].astype(BF16)
            qc_ref[:, 2 * lo + 128:2 * hi] = (q_ref[:, HB + lo:HB + hi] * cos
                                              + q_ref[:, 2 * HB + lo:2 * HB + hi] * sin).astype(BF16)
            kc_ref[:, 2 * lo:2 * lo + 128] = kv_ref[:, 2 * lo:2 * lo + 128].astype(BF16)
            kc_ref[:, 2 * lo + 128:2 * hi] = kr
            v_ref[:, lo:hi] = kv_ref[:, 2 * lo + 128:2 * hi].astype(BF16)

    tab = pl.BlockSpec((tm, 128), lambda i: (i, 0))
    return pl.pallas_call(
        body, name=name,
        out_shape=(jax.ShapeDtypeStruct((T, 2 * HB), BF16), jax.ShapeDtypeStruct((T, 2 * HB), BF16),
                   jax.ShapeDtypeStruct((T, HB), BF16)),
        grid=(T // tm,),
        in_specs=[pl.BlockSpec((tm, 3 * HB), lambda i: (i, 0)), pl.BlockSpec((tm, 2 * HB), lambda i: (i, 0)),
                  pl.BlockSpec((tm, 128), lambda i: (i, krb)), pl.BlockSpec((tm, 128), lambda i: (i, krb + 1)),
                  tab, tab],
        out_specs=(pl.BlockSpec((tm, 2 * HB), lambda i: (i, 0)), pl.BlockSpec((tm, 2 * HB), lambda i: (i, 0)),
                   pl.BlockSpec((tm, HB), lambda i: (i, 0))),
        compiler_params=_cparams("parallel"),
    )(q, kv, proj, proj, cosp, sinp)


def _rope_bwd(dqc, dkc, dv, cosp, sinp, *, H, name):
    T = dqc.shape[0]
    HB = H * V7X_LANES
    tm = ATT_BLK

    def body(dqc_ref, dkc_ref, dv_ref, cos_ref, sin_ref, dq_ref, dkv_ref, dkr_ref):
        cos = cos_ref[...]
        sin = sin_ref[...]
        gk = jnp.zeros((tm, 128), F32)
        for h in range(H):
            lo, hi = h * 128, (h + 1) * 128
            dq_ref[:, lo:hi] = dqc_ref[:, 2 * lo:2 * lo + 128].astype(BF16)
            gq = dqc_ref[:, 2 * lo + 128:2 * hi]
            dq_ref[:, HB + lo:HB + hi] = (gq * cos).astype(BF16)
            dq_ref[:, 2 * HB + lo:2 * HB + hi] = (gq * sin).astype(BF16)
            dkv_ref[:, 2 * lo:2 * lo + 128] = dkc_ref[:, 2 * lo:2 * lo + 128].astype(BF16)
            dkv_ref[:, 2 * lo + 128:2 * hi] = dv_ref[:, lo:hi].astype(BF16)
            gk = gk + dkc_ref[:, 2 * lo + 128:2 * hi]
        dkr_ref[:, 0:128] = (gk * cos).astype(BF16)
        dkr_ref[:, 128:256] = (gk * sin).astype(BF16)

    tab = pl.BlockSpec((tm, 128), lambda i: (i, 0))
    return pl.pallas_call(
        body, name=name,
        out_shape=(jax.ShapeDtypeStruct((T, 3 * HB), BF16), jax.ShapeDtypeStruct((T, 2 * HB), BF16),
                   jax.ShapeDtypeStruct((T, 256), BF16)),
        grid=(T // tm,),
        in_specs=[pl.BlockSpec((tm, 2 * HB), lambda i: (i, 0)), pl.BlockSpec((tm, 2 * HB), lambda i: (i, 0)),
                  pl.BlockSpec((tm, HB), lambda i: (i, 0)), tab, tab],
        out_specs=(pl.BlockSpec((tm, 3 * HB), lambda i: (i, 0)), pl.BlockSpec((tm, 2 * HB), lambda i: (i, 0)),
                   pl.BlockSpec((tm, 256), lambda i: (i, 0))),
        compiler_params=_cparams("parallel"),
    )(dqc, dkc, dv, cosp, sinp)


def _mla_fwd(qc, kc, v, *, H, scale, name):
    T = qc.shape[0]
    tq = tk = ATT_BLK
    nq = T // tq

    def body(q_ref, k_ref, v_ref, o_ref, lse_ref):
        i = pl.program_id(1)
        q = q_ref[...]
        row = i * tq + lax.broadcasted_iota(jnp.int32, (tq, tk), 0)
        col = lax.broadcasted_iota(jnp.int32, (tq, tk), 1)

        def step(j, carry, masked):
            m, l, acc = carry
            off = pl.multiple_of(j * tk, tk)
            ks = k_ref[pl.ds(off, tk), :]
            vs = v_ref[pl.ds(off, tk), :]
            s = lax.dot_general(q, ks, NT_DIMS, preferred_element_type=F32) * scale
            if masked:
                s = jnp.where(col + j * tk <= row, s, NEG)
            m_new = jnp.maximum(m, jnp.max(s, axis=1, keepdims=True))
            alpha = jnp.exp(m - m_new)
            p = jnp.exp(s - m_new)
            l = alpha * l + jnp.sum(p, axis=1, keepdims=True)
            acc = alpha * acc + jnp.dot(p.astype(BF16), vs, preferred_element_type=F32)
            return m_new, l, acc

        n_full = (i * tq) // tk
        n_tot = ((i + 1) * tq + tk - 1) // tk
        carry = (jnp.full((tq, 1), NEG, F32), jnp.zeros((tq, 1), F32), jnp.zeros((tq, 128), F32))
        carry = lax.fori_loop(0, n_full, functools.partial(step, masked=False), carry)
        m, l, acc = lax.fori_loop(n_full, n_tot, functools.partial(step, masked=True), carry)
        o_ref[...] = acc / l
        lse_ref[0] = m + jnp.log(l)

    return pl.pallas_call(
        body, name=name,
        out_shape=(jax.ShapeDtypeStruct((T, H * 128), F32), jax.ShapeDtypeStruct((H, T, 1), F32)),
        grid=(H, nq),
        in_specs=[pl.BlockSpec((tq, 256), lambda h, i: (i, h)), pl.BlockSpec((T, 256), lambda h, i: (0, h)),
                  pl.BlockSpec((T, 128), lambda h, i: (0, h))],
        out_specs=(pl.BlockSpec((tq, 128), lambda h, i: (i, h)), pl.BlockSpec((1, tq, 1), lambda h, i: (h, i, 0))),
        compiler_params=_cparams("parallel", "parallel"),
    )(qc, kc, v)


def _mla_bwd(qc, kc, v, o, do, lse, *, H, scale, name):
    T = qc.shape[0]
    tq = tk = ATT_BLK
    nq, nk = T // tq, T // tk

    def body(q_ref, k_ref, v_ref, o_ref, do_ref, lse_ref, dq_ref, dk_ref, dv_ref, delta_ref):
        dq_ref[...] = jnp.zeros_like(dq_ref)

        def fill_delta(i, c):
            off = pl.multiple_of(i * tq, tq)
            delta_ref[pl.ds(off, tq), :] = jnp.sum(do_ref[pl.ds(off, tq), :] * o_ref[pl.ds(off, tq), :],
                                                   axis=1, keepdims=True)
            return c

        lax.fori_loop(0, nq, fill_delta, 0)
        rowi = lax.broadcasted_iota(jnp.int32, (tq, tk), 0)
        coli = lax.broadcasted_iota(jnp.int32, (tq, tk), 1)

        def kblock(j, c):
            koff = pl.multiple_of(j * tk, tk)
            ks = k_ref[pl.ds(koff, tk), :]
            vs = v_ref[pl.ds(koff, tk), :]

            def qstep(i, carry, masked):
                dk, dv = carry
                qoff = pl.multiple_of(i * tq, tq)
                qs = q_ref[pl.ds(qoff, tq), :]
                dob = do_ref[pl.ds(qoff, tq), :].astype(BF16)
                s = lax.dot_general(qs, ks, NT_DIMS, preferred_element_type=F32) * scale
                if masked:
                    s = jnp.where(coli + j * tk <= rowi + i * tq, s, NEG)
                p = jnp.exp(s - lse_ref[0, pl.ds(qoff, tq), :])
                dv = dv + lax.dot_general(p.astype(BF16), dob, TN_DIMS, preferred_element_type=F32)
                dp = lax.dot_general(dob, vs, NT_DIMS, preferred_element_type=F32)
                ds = (p * (dp - delta_ref[pl.ds(qoff, tq), :]) * scale).astype(BF16)
                dk = dk + lax.dot_general(ds, qs, TN_DIMS, preferred_element_type=F32)
                dq_ref[pl.ds(qoff, tq), :] += jnp.dot(ds, ks, preferred_element_type=F32)
                return dk, dv

            i0 = (j * tk) // tq
            i1 = jnp.minimum(((j + 1) * tk + tq - 1) // tq, nq)
            carry = (jnp.zeros((tk, 256), F32), jnp.zeros((tk, 128), F32))
            carry = lax.fori_loop(i0, i1, functools.partial(qstep, masked=True), carry)
            dk, dv = lax.fori_loop(i1, nq, functools.partial(qstep, masked=False), carry)
            dk_ref[pl.ds(koff, tk), :] = dk
            dv_ref[pl.ds(koff, tk), :] = dv
            return c

        lax.fori_loop(0, nk, kblock, 0)

    wide = pl.BlockSpec((T, 256), lambda h: (0, h))
    narrow = pl.BlockSpec((T, 128), lambda h: (0, h))
    return pl.pallas_call(
        body, name=name,
        out_shape=(jax.ShapeDtypeStruct((T, H * 256), F32), jax.ShapeDtypeStruct((T, H * 256), F32),
                   jax.ShapeDtypeStruct((T, H * 128), F32)),
        grid=(H,),
        in_specs=[wide, wide, narrow, narrow, narrow, pl.BlockSpec((1, T, 1), lambda h: (h, 0, 0))],
        out_specs=(wide, wide, narrow),
        scratch_shapes=[pltpu.VMEM((T, 1), F32)],
        compiler_params=_cparams("parallel"),
    )(qc, kc, v, o, do, lse)


def _log_sigmoid_pair(z):
    e = jnp.exp(-jnp.abs(z))
    lb = jnp.minimum(z, 0.0) - jnp.log(1.0 + e)
    inv = 1.0 / (1.0 + e)
    sg = jnp.where(z >= 0.0, inv, e * inv)
    return lb, lb - z, sg


def _tri_dot(x, tri):
    hi = x.astype(BF16)
    lo = (x - hi.astype(F32)).astype(BF16)
    return jnp.dot(hi, tri, preferred_element_type=F32) + jnp.dot(lo, tri, preferred_element_type=F32)


def _sb_fwd(proj, *, H, qcol0, kcol0, vcol0, scale, name):
    T = proj.shape[0]
    tq = tk = ATT_BLK
    nq = T // tq
    qb, kb, vb = qcol0 // 128, kcol0 // 128, vcol0 // 128

    def body(q_ref, k_ref, v_ref, y_ref, tot_ref):
        i = pl.program_id(1)
        q = q_ref[...].astype(BF16)
        row = i * tq + lax.broadcasted_iota(jnp.int32, (tq, tk), 0)
        col = lax.broadcasted_iota(jnp.int32, (tq, tk), 1)
        r_i = lax.broadcasted_iota(jnp.int32, (tk, tk), 0)
        c_i = lax.broadcasted_iota(jnp.int32, (tk, tk), 1)
        tri_after = (r_i > c_i).astype(BF16)

        def step(idx, carry, masked, top):
            rem, acc = carry
            j = top - 1 - idx
            off = pl.multiple_of(j * tk, tk)
            ks = k_ref[pl.ds(off, tk), :].astype(BF16)
            vs = v_ref[pl.ds(off, tk), :].astype(BF16)
            z = lax.dot_general(q, ks, NT_DIMS, preferred_element_type=F32) * scale
            lb, lom, _ = _log_sigmoid_pair(z)
            if masked:
                valid = col + j * tk < row
                lom = jnp.where(valid, lom, 0.0)
            a = jnp.exp(lb + _tri_dot(lom, tri_after) + rem)
            if masked:
                a = jnp.where(valid, a, 0.0)
            acc = acc + jnp.dot(a.astype(BF16), vs, preferred_element_type=F32)
            return rem + jnp.sum(lom, axis=1, keepdims=True), acc

        n_full = (i * tq) // tk
        n_tot = ((i + 1) * tq + tk - 1) // tk
        carry = (jnp.zeros((tq, 1), F32), jnp.zeros((tq, 128), F32))
        carry = lax.fori_loop(0, n_tot - n_full, functools.partial(step, masked=True, top=n_tot), carry)
        rem, acc = lax.fori_loop(0, n_full, functools.partial(step, masked=False, top=n_full), carry)
        y_ref[...] = acc
        tot_ref[0] = rem

    return pl.pallas_call(
        body, name=name,
        out_shape=(jax.ShapeDtypeStruct((T, H * 128), F32), jax.ShapeDtypeStruct((H, T, 1), F32)),
        grid=(H, nq),
        in_specs=[pl.BlockSpec((tq, 128), lambda h, i: (i, qb + h)), pl.BlockSpec((T, 128), lambda h, i: (0, kb + h)),
                  pl.BlockSpec((T, 128), lambda h, i: (0, vb + h))],
        out_specs=(pl.BlockSpec((tq, 128), lambda h, i: (i, h)), pl.BlockSpec((1, tq, 1), lambda h, i: (h, i, 0))),
        compiler_params=_cparams("parallel", "parallel"),
    )(proj, proj, proj)


def _sb_bwd(proj, dy, tot, *, H, qcol0, kcol0, vcol0, scale, name):
    T = proj.shape[0]
    tq = tk = ATT_BLK
    nq = T // tq
    qb, kb, vb = qcol0 // 128, kcol0 // 128, vcol0 // 128

    def body(q_ref, k_ref, v_ref, dy_ref, tot_ref, dq_ref, dk_ref, dv_ref, dk_acc, dv_acc):
        dk_acc[...] = jnp.zeros_like(dk_acc)
        dv_acc[...] = jnp.zeros_like(dv_acc)
        rowi = lax.broadcasted_iota(jnp.int32, (tq, tk), 0)
        coli = lax.broadcasted_iota(jnp.int32, (tq, tk), 1)
        r_i = lax.broadcasted_iota(jnp.int32, (tk, tk), 0)
        c_i = lax.broadcasted_iota(jnp.int32, (tk, tk), 1)
        tri_upto = (r_i <= c_i).astype(BF16)
        tri_before = (r_i < c_i).astype(BF16)

        def qblock(i, c):
            qoff = pl.multiple_of(i * tq, tq)
            qs = q_ref[pl.ds(qoff, tq), :].astype(BF16)
            dyb = dy_ref[pl.ds(qoff, tq), :].astype(BF16)

            def step(j, carry, masked):
                rem, pre, dq = carry
                koff = pl.multiple_of(j * tk, tk)
                ks = k_ref[pl.ds(koff, tk), :].astype(BF16)
                vs = v_ref[pl.ds(koff, tk), :].astype(BF16)
                z = lax.dot_general(qs, ks, NT_DIMS, preferred_element_type=F32) * scale
                lb, lom, sg = _log_sigmoid_pair(z)
                if masked:
                    valid = coli + j * tk < rowi + i * tq
                    lom = jnp.where(valid, lom, 0.0)
                a = jnp.exp(lb + rem - _tri_dot(lom, tri_upto))
                if masked:
                    a = jnp.where(valid, a, 0.0)
                dv_acc[pl.ds(koff, tk), :] += lax.dot_general(a.astype(BF16), dyb, TN_DIMS,
                                                              preferred_element_type=F32)
                de = a * lax.dot_general(dyb, vs, NT_DIMS, preferred_element_type=F32)
                before = pre + _tri_dot(de, tri_before)
                dz = de * (1.0 - sg) - before * sg
                if masked:
                    dz = jnp.where(valid, dz, 0.0)
                dzb = (dz * scale).astype(BF16)
                dk_acc[pl.ds(koff, tk), :] += lax.dot_general(dzb, qs, TN_DIMS, preferred_element_type=F32)
                dq = dq + jnp.dot(dzb, ks, preferred_element_type=F32)
                return (rem - jnp.sum(lom, axis=1, keepdims=True), pre + jnp.sum(de, axis=1, keepdims=True), dq)

            n_full = (i * tq) // tk
            n_tot = ((i + 1) * tq + tk - 1) // tk
            carry = (tot_ref[0, pl.ds(qoff, tq), :], jnp.zeros((tq, 1), F32), jnp.zeros((tq, 128), F32))
            carry = lax.fori_loop(0, n_full, functools.partial(step, masked=False), carry)
            _, _, dq = lax.fori_loop(n_full, n_tot, functools.partial(step, masked=True), carry)
            dq_ref[pl.ds(qoff, tq), :] = dq.astype(dq_ref.dtype)
            return c

        lax.fori_loop(0, nq, qblock, 0)
        dk_ref[...] = dk_acc[...].astype(dk_ref.dtype)
        dv_ref[...] = dv_acc[...].astype(dv_ref.dtype)

    def seg(b):
        return pl.BlockSpec((T, 128), lambda h: (0, b + h))

    out = pl.BlockSpec((T, 128), lambda h: (0, h))
    return pl.pallas_call(
        body, name=name,
        out_shape=tuple(jax.ShapeDtypeStruct((T, H * 128), BF16) for _ in range(3)),
        grid=(H,),
        in_specs=[seg(qb), seg(kb), seg(vb), out, pl.BlockSpec((1, T, 1), lambda h: (h, 0, 0))],
        out_specs=(out, out, out),
        scratch_shapes=[pltpu.VMEM((T, 128), F32), pltpu.VMEM((T, 128), F32)],
        compiler_params=_cparams("parallel"),
    )(proj, proj, proj, dy, tot)


def _final_loss(h, g, target, *, row0, n_rows, name):
    T, D = h.shape
    tm = _row_tile(T, D, 6)
    nt = T // tm

    def body(h_ref, g_ref, t_ref, dh_ref, dg_ref, loss_ref, acc_ref, lacc_ref):
        i = pl.program_id(0)
        xv = h_ref[...]
        r = lax.rsqrt(jnp.mean(xv * xv, axis=-1, keepdims=True) + EPS)
        xh = xv * r
        gv = g_ref[...]
        rows = i * tm + lax.broadcasted_iota(jnp.int32, (tm, 1), 0)
        valid = (rows >= row0) & (rows < row0 + n_rows)
        err = jnp.where(valid, xh * gv - t_ref[...], 0.0)
        dout = err * (1.0 / D)
        dxh = dout * gv
        dh_ref[...] = r * (dxh - xh * jnp.mean(dxh * xh, axis=-1, keepdims=True))
        part = jnp.sum((dout * xh).reshape(tm // 8, 8, D), axis=0)
        lpart = jnp.sum((err * err).reshape(tm // 8, 8, D), axis=0)

        @pl.when(i == 0)
        def _():
            acc_ref[...] = part
            lacc_ref[...] = lpart

        @pl.when(i > 0)
        def _():
            acc_ref[...] += part
            lacc_ref[...] += lpart

        @pl.when(i == nt - 1)
        def _():
            dg_ref[...] = jnp.sum(acc_ref[...], axis=0, keepdims=True)
            loss_ref[...] = (0.5 / D) * jnp.sum(jnp.sum(lacc_ref[...], axis=0, keepdims=True), axis=1, keepdims=True)

    row = pl.BlockSpec((tm, D), lambda i: (i, 0))
    vec = pl.BlockSpec((1, D), lambda i: (0, 0))
    return pl.pallas_call(
        body, name=name,
        out_shape=(jax.ShapeDtypeStruct((T, D), F32), jax.ShapeDtypeStruct((1, D), F32),
                   jax.ShapeDtypeStruct((1, 1), F32)),
        grid=(nt,),
        in_specs=[row, vec, row],
        out_specs=(row, vec, pl.BlockSpec((1, 1), lambda i: (0, 0))),
        scratch_shapes=[pltpu.VMEM((8, D), F32), pltpu.VMEM((8, D), F32)],
        compiler_params=_cparams("arbitrary"),
    )(h, g.reshape(1, D), target)


def _elementwise(fn, args, out_dtypes, name):
    shape = args[0].shape
    C = shape[-1]
    R = math.prod(shape[:-1])
    n = len(args) + len(out_dtypes)
    cap = max(16, (VMEM_TILE_BUDGET // 2) // (2 * n * C * 4))
    tr = _divisor_tile(R, cap, 16)
    n_in = len(args)

    def body(*refs):
        outs = fn(*[r[...] for r in refs[:n_in]])
        for o_ref, val in zip(refs[n_in:], outs):
            o_ref[...] = val.astype(o_ref.dtype)

    spec = pl.BlockSpec((tr, C), lambda i: (i, 0))
    res = pl.pallas_call(
        body, name=name,
        out_shape=tuple(jax.ShapeDtypeStruct((R, C), dt) for dt in out_dtypes),
        grid=(R // tr,),
        in_specs=[spec] * n_in, out_specs=tuple([spec] * len(out_dtypes)),
        compiler_params=_cparams("parallel"),
    )(*[a.reshape(R, C) for a in args])
    return tuple(r.reshape(shape) for r in res)


def _adamw_math(w, g, m, v):
    m = ADAM_B1 * m + (1.0 - ADAM_B1) * g
    v = ADAM_B2 * v + (1.0 - ADAM_B2) * (g * g)
    m_hat = m / (1.0 - ADAM_B1 ** ADAM_STEP)
    v_hat = v / (1.0 - ADAM_B2 ** ADAM_STEP)
    delta = -ADAM_LR * (m_hat / (jnp.sqrt(v_hat) + ADAM_EPS) + ADAM_WD * w)
    return delta, m, v


def _adamw(w, g, m, v, name):
    return _elementwise(_adamw_math, [w, g, m, v], [F32, F32, F32], name)


ANY = pl.BlockSpec(memory_space=pl.ANY)


def _position():
    return lax.axis_index("x"), lax.axis_index("y"), lax.axis_index("c")


def _gather_shards(a, name):
    def body(a_ref, o_ref, send_sems, recv_sems, local_sem):
        x, y, c = _position()
        p = 2 * x + y
        chips = [(1 - x, y), (x, 1 - y), (1 - x, 1 - y)]

        def copy(k, src, dst, to):
            return pltpu.make_async_remote_copy(src_ref=src, dst_ref=dst, send_sem=send_sems.at[k],
                                                recv_sem=recv_sems.at[k], device_id=to, device_id_type=MESH)

        mine = pltpu.make_async_copy(a_ref, o_ref.at[p], local_sem)
        mine.start()
        first = [copy(k, a_ref.at[c], o_ref.at[p, c], (qx, qy, c)) for k, (qx, qy) in enumerate(chips)]
        for cp in first:
            cp.start()
        passed = []
        for k, (qx, qy) in enumerate(chips):
            land = o_ref.at[2 * qx + qy, c]
            copy(k, land, land, (x, y, c)).wait_recv()
            fwd = copy(3 + k, land, land, (x, y, 1 - c))
            fwd.start()
            passed.append(fwd)
        for k, (qx, qy) in enumerate(chips):
            land = o_ref.at[2 * qx + qy, 1 - c]
            copy(3 + k, land, land, (x, y, c)).wait_recv()
        for cp in first + passed:
            cp.wait_send()
        mine.wait()

    return pl.pallas_call(
        body, name=name,
        out_shape=jax.ShapeDtypeStruct((4,) + a.shape, a.dtype),
        in_specs=[ANY], out_specs=ANY,
        scratch_shapes=[pltpu.SemaphoreType.DMA((6,)), pltpu.SemaphoreType.DMA((6,)), pltpu.SemaphoreType.DMA],
    )(a)


def _swap_halves(g, name):
    def body(g_ref, o_ref, send_sem, recv_sem):
        x, y, c = _position()
        cp = pltpu.make_async_remote_copy(src_ref=g_ref.at[1 - c], dst_ref=o_ref, send_sem=send_sem,
                                          recv_sem=recv_sem, device_id=(x, y, 1 - c), device_id_type=MESH)
        cp.start()
        cp.wait()

    return pl.pallas_call(
        body, name=name,
        out_shape=jax.ShapeDtypeStruct(g.shape[1:], g.dtype),
        in_specs=[ANY], out_specs=ANY,
        scratch_shapes=[pltpu.SemaphoreType.DMA, pltpu.SemaphoreType.DMA],
    )(g)


def _scatter_to_chips(pb, name):
    def body(p_ref, o_ref, send_sems, recv_sems):
        x, y, c = _position()
        chips = [(1 - x, y), (x, 1 - y), (1 - x, 1 - y)]
        cps = [pltpu.make_async_remote_copy(src_ref=p_ref.at[2 * qx + qy], dst_ref=o_ref.at[k],
                                            send_sem=send_sems.at[k], recv_sem=recv_sems.at[k],
                                            device_id=(qx, qy, c), device_id_type=MESH)
               for k, (qx, qy) in enumerate(chips)]
        for cp in cps:
            cp.start()
        for cp in cps:
            cp.wait()

    return pl.pallas_call(
        body, name=name,
        out_shape=jax.ShapeDtypeStruct((3,) + pb.shape[1:], pb.dtype),
        in_specs=[ANY], out_specs=ANY,
        scratch_shapes=[pltpu.SemaphoreType.DMA((3,)), pltpu.SemaphoreType.DMA((3,))],
    )(pb)


def _join_halves(r, name):
    def body(r_ref, o_ref, send_sem, recv_sem, local_sem):
        x, y, c = _position()
        mine = pltpu.make_async_copy(r_ref, o_ref.at[c], local_sem)
        mine.start()
        cp = pltpu.make_async_remote_copy(src_ref=r_ref, dst_ref=o_ref.at[c], send_sem=send_sem,
                                          recv_sem=recv_sem, device_id=(x, y, 1 - c), device_id_type=MESH)
        cp.start()
        cp.wait()
        mine.wait()

    return pl.pallas_call(
        body, name=name,
        out_shape=jax.ShapeDtypeStruct((2,) + r.shape, r.dtype),
        in_specs=[ANY], out_specs=ANY,
        scratch_shapes=[pltpu.SemaphoreType.DMA, pltpu.SemaphoreType.DMA, pltpu.SemaphoreType.DMA],
    )(r)


def _reduce_to_shard(gh, tag):
    x, y, c = _position()
    p = 2 * x + y
    sib = _swap_halves(gh, f"swap_{tag}")
    mine = lax.dynamic_index_in_dim(gh, c, 0, keepdims=False)
    psum, pb = _elementwise(lambda a, b: (a + b, a + b), [mine, sib], [F32, BF16], f"pairsum_{tag}")
    got = _scatter_to_chips(pb, f"scatter_{tag}")
    own = lax.dynamic_index_in_dim(psum, p, 0, keepdims=False)
    (red,) = _elementwise(lambda o, a, b, d: (((o + a.astype(F32)) + b.astype(F32)) + d.astype(F32),),
                          [own, got[0], got[1], got[2]], [F32], f"chipsum_{tag}")
    return _join_halves(red, f"join_{tag}")


def _all_reduce_small(vec, name):
    R = vec.shape[0]

    def body(v_ref, o_ref, land_ref, send_sems, recv_sems):
        x, y, c = _position()
        me = 4 * x + 2 * y + c
        land_ref[me] = v_ref[...]
        cps = []
        for r in range(1, 8):
            rx, ry, rc = (r >> 2) & 1, (r >> 1) & 1, r & 1
            to = (x ^ rx, y ^ ry, c ^ rc)
            cps.append(pltpu.make_async_remote_copy(src_ref=v_ref, dst_ref=land_ref.at[me],
                                                    send_sem=send_sems.at[r - 1], recv_sem=recv_sems.at[r - 1],
                                                    device_id=to, device_id_type=MESH))
        for cp in cps:
            cp.start()
        for cp in cps:
            cp.wait()
        total = land_ref[0]
        for d in range(1, 8):
            total = total + land_ref[d]
        o_ref[...] = total

    return pl.pallas_call(
        body, name=name,
        out_shape=jax.ShapeDtypeStruct((R, 128), F32),
        in_specs=[pl.BlockSpec(memory_space=pltpu.VMEM)], out_specs=pl.BlockSpec(memory_space=pltpu.VMEM),
        scratch_shapes=[pltpu.VMEM((8, R, 128), F32), pltpu.SemaphoreType.DMA((7,)), pltpu.SemaphoreType.DMA((7,))],
    )(vec)


def _rot(w):
    half = ROPE_DIM // 2
    return jnp.concatenate([-w[..., half:], w[..., :half]], axis=-1)


def _unrot(g):
    half = ROPE_DIM // 2
    return jnp.concatenate([g[..., half:], -g[..., :half]], axis=-1)


class _Layout:
    def __init__(self, D, QL, KVL):
        self.D, self.QL, self.KVL = D, QL, KVL
        self.H = D // 256
        self.WG = self.H * 128
        WG = self.WG
        self.z_mla, self.q_sb, self.k_sb, self.v_sb, self.z_sb = 0, WG, 2 * WG, 3 * WG, 4 * WG
        self.c_q = 5 * WG
        self.c_kv = self.c_q + QL
        self.k_r = self.c_kv + KVL
        self.width = -(-(self.k_r + 256) // 512) * 512
        self.orig = (QL, KVL, ROPE_DIM, WG, WG, WG, WG, WG)

    def pack_w_in(self, w):
        cuts = []
        o = 0
        for s in self.orig:
            cuts.append(w[:, o:o + s])
            o += s
        c_q, c_kv, k_r, z_mla, q_sb, k_sb, v_sb, z_sb = cuts
        z64 = jnp.zeros((w.shape[0], 128 - ROPE_DIM), w.dtype)
        pad = jnp.zeros((w.shape[0], self.width - self.k_r - 256), w.dtype)
        return jnp.concatenate([z_mla, q_sb, k_sb, v_sb, z_sb, c_q, c_kv, k_r, z64, _rot(k_r), z64, pad], axis=1)

    def unpack_dw_in(self, g):
        WG = self.WG
        k_r = g[:, self.k_r:self.k_r + ROPE_DIM] + _unrot(g[:, self.k_r + 128:self.k_r + 128 + ROPE_DIM])
        return jnp.concatenate([g[:, self.c_q:self.c_q + self.QL], g[:, self.c_kv:self.c_kv + self.KVL], k_r,
                                g[:, 0:WG], g[:, WG:2 * WG], g[:, 2 * WG:3 * WG], g[:, 3 * WG:4 * WG],
                                g[:, 4 * WG:5 * WG]], axis=1)

    def pack_w_uq(self, w):
        H = self.H
        w3 = w.reshape(w.shape[0], H, 128 + ROPE_DIM)
        nope = w3[:, :, :128]
        r = w3[:, :, 128:]
        z = jnp.zeros(r.shape, w.dtype)
        a = jnp.concatenate([r, z], axis=-1)
        b = jnp.concatenate([_rot(r), z], axis=-1)
        return jnp.concatenate([nope.reshape(-1, H * 128), a.reshape(-1, H * 128), b.reshape(-1, H * 128)], axis=1)

    def unpack_dw_uq(self, g):
        H = self.H
        HB = H * 128
        nope = g[:, :HB].reshape(-1, H, 128)
        a = g[:, HB:2 * HB].reshape(-1, H, 128)[:, :, :ROPE_DIM]
        b = g[:, 2 * HB:].reshape(-1, H, 128)[:, :, :ROPE_DIM]
        return jnp.concatenate([nope, a + _unrot(b)], axis=-1).reshape(-1, H * (128 + ROPE_DIM))


def _rope_tables(T):
    inv_freq = ROPE_THETA ** (-jnp.arange(0, ROPE_DIM, 2, dtype=F32) / ROPE_DIM)
    ang = jnp.arange(T, dtype=jnp.int32).astype(F32)[:, None] * inv_freq[None, :]
    z = jnp.zeros((T, 128 - ROPE_DIM), F32)
    cos, sin = jnp.cos(ang), jnp.sin(ang)
    return jnp.concatenate([cos, cos, z], axis=1), jnp.concatenate([sin, sin, z], axis=1)


def _layer_fwd(h, wl, lay, tabs):
    g_norm, w_in, g_q, g_kv, w_uq, w_ukv, g_mla, g_sb, w_o = wl
    cosp, sinp = tabs
    H = lay.H
    u = _rms_fwd(h, g_norm, col0=0, out_dtype=BF16, name="rms_h")
    proj = _matmul(u, w_in, mode="nn", out_dtype=F32, name="mm_in")
    cqn = _rms_fwd(proj, g_q, col0=lay.c_q, out_dtype=BF16, name="rms_cq")
    ckvn = _rms_fwd(proj, g_kv, col0=lay.c_kv, out_dtype=BF16, name="rms_ckv")
    q = _matmul(cqn, w_uq, mode="nn", out_dtype=F32, name="mm_uq")
    kv = _matmul(ckvn, w_ukv, mode="nn", out_dtype=F32, name="mm_ukv")
    qc, kc, v = _rope_fwd(q, kv, proj, cosp, sinp, H=H, kr_col0=lay.k_r, name="rope_fwd")
    o_mla, lse = _mla_fwd(qc, kc, v, H=H, scale=1.0 / math.sqrt(128 + ROPE_DIM), name="mla_fwd")
    o_sb, tot = _sb_fwd(proj, H=H, qcol0=lay.q_sb, kcol0=lay.k_sb, vcol0=lay.v_sb,
                        scale=1.0 / math.sqrt(128), name="sb_fwd")
    y_mla = _gate_fwd(o_mla, proj, g_mla, zcol0=lay.z_mla, name="gate_fwd_mla")
    y_sb = _gate_fwd(o_sb, proj, g_sb, zcol0=lay.z_sb, name="gate_fwd_sb")
    y = jnp.concatenate([y_mla, y_sb], axis=1)
    h_out = _matmul(y, w_o, mode="nn", out_dtype=F32, name="mm_o", residual=h)
    saved = (h, u, proj, cqn, ckvn, qc, kc, v, o_mla, lse, o_sb, tot, y)
    return h_out, saved


def _layer_bwd(dh, saved, wl, lay, tabs):
    g_norm, w_in, g_q, g_kv, w_uq, w_ukv, g_mla, g_sb, w_o = wl
    h, u, proj, cqn, ckvn, qc, kc, v, o_mla, lse, o_sb, tot, y = saved
    cosp, sinp = tabs
    H = lay.H
    dy = _matmul(dh, w_o, mode="nt", out_dtype=F32, name="mm_o_dx")
    d_w_o = _matmul(y, dh, mode="tn", out_dtype=F32, name="mm_o_dw")
    do_mla, dz_mla, dg_mla = _gate_bwd(dy, o_mla, proj, g_mla, grp=0, zcol0=lay.z_mla, name="gate_bwd_mla")
    do_sb, dz_sb, dg_sb = _gate_bwd(dy, o_sb, proj, g_sb, grp=1, zcol0=lay.z_sb, name="gate_bwd_sb")
    dq_sb, dk_sb, dv_sb = _sb_bwd(proj, do_sb, tot, H=H, qcol0=lay.q_sb, kcol0=lay.k_sb, vcol0=lay.v_sb,
                                  scale=1.0 / math.sqrt(128), name="sb_bwd")
    dqc, dkc, dv = _mla_bwd(qc, kc, v, o_mla, do_mla, lse, H=H, scale=1.0 / math.sqrt(128 + ROPE_DIM),
                            name="mla_bwd")
    dq, dkv, dkr = _rope_bwd(dqc, dkc, dv, cosp, sinp, H=H, name="rope_bwd")
    d_w_uq = _matmul(cqn, dq, mode="tn", out_dtype=F32, name="mm_uq_dw")
    dcqn = _matmul(dq, w_uq, mode="nt", out_dtype=F32, name="mm_uq_dx")
    d_w_ukv = _matmul(ckvn, dkv, mode="tn", out_dtype=F32, name="mm_ukv_dw")
    dckvn = _matmul(dkv, w_ukv, mode="nt", out_dtype=F32, name="mm_ukv_dx")
    dcq, dg_q = _rms_bwd(proj, g_q, dcqn, col0=lay.c_q, out_dtype=BF16, name="rms_cq_bwd")
    dckv, dg_kv = _rms_bwd(proj, g_kv, dckvn, col0=lay.c_kv, out_dtype=BF16, name="rms_ckv_bwd")
    pad = jnp.zeros((dh.shape[0], lay.width - lay.k_r - 256), BF16)
    dproj = jnp.concatenate([dz_mla, dq_sb, dk_sb, dv_sb, dz_sb, dcq, dckv, dkr, pad], axis=1)
    d_w_in = _matmul(u, dproj, mode="tn", out_dtype=F32, name="mm_in_dw")
    du = _matmul(dproj, w_in, mode="nt", out_dtype=F32, name="mm_in_dx")
    dh_prev, dg_norm = _rms_bwd(h, g_norm, du, col0=0, out_dtype=F32, name="rms_h_bwd", residual=dh)
    grads = (dg_norm[0], lay.unpack_dw_in(d_w_in), dg_q[0], dg_kv[0], lay.unpack_dw_uq(d_w_uq), d_w_ukv,
             dg_mla[0], dg_sb[0], d_w_o)
    return dh_prev, grads


def _halves(a):
    return a.reshape((2, a.shape[0] // 2) + a.shape[1:])


def _gather_cols(w, name):
    L, K, n = w.shape
    g = _gather_shards(_halves(w.astype(BF16)), name).reshape(4, L, K, n)
    return [jnp.concatenate([g[q, l] for q in range(4)], axis=1) for l in range(L)]


def _cut_cols(g):
    L, K, N = g.shape
    return g.reshape(2, L // 2, K, 4, N // 4).transpose(0, 3, 1, 2, 4)


def kernel(x, meta_tokens, g_norm, w_in, g_q, g_kv, w_uq, w_ukv, g_out_mla, g_out_sb, w_o, g_final, loss_target, m_meta_tokens, m_g_norm, m_w_in, m_g_q, m_g_kv, m_w_uq, m_w_ukv, m_g_out_mla, m_g_out_sb, m_w_o, m_g_final, v_meta_tokens, v_g_norm, v_w_in, v_g_q, v_g_kv, v_w_uq, v_w_ukv, v_g_out_mla, v_g_out_sb, v_w_o, v_g_final):
    _, S, D = x.shape
    NM = meta_tokens.shape[0]
    L = g_norm.shape[0]
    lay = _Layout(D, g_q.shape[1], g_kv.shape[1])
    TP = -(-(NM + S) // ROW_ALIGN) * ROW_ALIGN
    tabs = _rope_tables(TP)

    w_in_full = [lay.pack_w_in(w) for w in _gather_cols(w_in, "gather_w_in")]
    w_uq_full = [lay.pack_w_uq(w) for w in _gather_cols(w_uq, "gather_w_uq")]
    w_ukv_full = _gather_cols(w_ukv, "gather_w_ukv")
    w_o_g = _gather_shards(_halves(w_o.astype(BF16)), "gather_w_o").reshape((4, L) + w_o.shape[1:])
    w_o_full = [jnp.concatenate([w_o_g[q, l] for q in range(4)], axis=0) for l in range(L)]
    meta_g = _gather_shards(meta_tokens.reshape(2, NM // 2, -1), "gather_meta").reshape(4, NM, -1)
    meta_full = jnp.concatenate([meta_g[q] for q in range(4)], axis=1)

    h = jnp.concatenate([meta_full, x[0], jnp.zeros((TP - NM - S, D), F32)], axis=0)
    target = jnp.pad(loss_target[0], ((NM, TP - NM - S), (0, 0)))
    weights = [(g_norm[l], w_in_full[l], g_q[l], g_kv[l], w_uq_full[l], w_ukv_full[l], g_out_mla[l],
                g_out_sb[l], w_o_full[l]) for l in range(L)]
    saved = []
    for l in range(L):
        h, s = _layer_fwd(h, weights[l], lay, tabs)
        saved.append(s)
    dh, dg_final, loss_part = _final_loss(h, g_final, target, row0=NM, n_rows=S, name="final_loss")

    layer_grads = [None] * L
    for l in reversed(range(L)):
        dh, layer_grads[l] = _layer_bwd(dh, saved[l], weights[l], lay, tabs)
    grad_x = dh[NM:NM + S][None]
    d_meta = dh[:NM]

    def stack(i):
        return jnp.stack([layer_grads[l][i] for l in range(L)], axis=0)

    small = [stack(0), stack(2), stack(3), stack(6), stack(7), dg_final[0]]
    flat = jnp.concatenate([s.reshape(-1) for s in small])
    n_flat = flat.shape[0]
    rows = -(-n_flat // (8 * 128)) * 8
    packed = jnp.pad(flat, (0, rows * 128 - n_flat)).reshape(rows, 128)
    summed = _all_reduce_small(packed, "allreduce_gains").reshape(-1)
    small_red = []
    o = 0
    for s in small:
        small_red.append(summed[o:o + s.size].reshape(s.shape))
        o += s.size
    g_g_norm, g_g_q, g_g_kv, g_g_mla, g_g_sb, g_g_final = small_red

    g_w_in = _reduce_to_shard(_cut_cols(stack(1)), "w_in").reshape(w_in.shape)
    g_w_uq = _reduce_to_shard(_cut_cols(stack(4)), "w_uq").reshape(w_uq.shape)
    g_w_ukv = _reduce_to_shard(_cut_cols(stack(5)), "w_ukv").reshape(w_ukv.shape)
    d_w_o = stack(8).reshape(2, L // 2, 4, w_o.shape[1], D).transpose(0, 2, 1, 3, 4)
    g_w_o = _reduce_to_shard(d_w_o, "w_o").reshape(w_o.shape)
    d_meta = d_meta.reshape(2, NM // 2, 4, D // 4).transpose(0, 2, 1, 3)
    g_meta = _reduce_to_shard(d_meta, "meta").reshape(meta_tokens.shape)

    loss = lax.psum(loss_part[0, 0], ("x", "y", "c"))

    names = ["meta", "g_norm", "w_in", "g_q", "g_kv", "w_uq", "w_ukv", "g_out_mla", "g_out_sb", "w_o", "g_final"]
    ws = [meta_tokens, g_norm, w_in, g_q, g_kv, w_uq, w_ukv, g_out_mla, g_out_sb, w_o, g_final]
    gs = [g_meta, g_g_norm, g_w_in, g_g_q, g_g_kv, g_w_uq, g_w_ukv, g_g_mla, g_g_sb, g_w_o, g_g_final]
    ms = [m_meta_tokens, m_g_norm, m_w_in, m_g_q, m_g_kv, m_w_uq, m_w_ukv, m_g_out_mla, m_g_out_sb, m_w_o, m_g_final]
    vs = [v_meta_tokens, v_g_norm, v_w_in, v_g_q, v_g_kv, v_w_uq, v_w_ukv, v_g_out_mla, v_g_out_sb, v_w_o, v_g_final]
    deltas, new_m, new_v = [], [], []
    for n, w, g, m, v in zip(names, ws, gs, ms, vs):
        shape = w.shape
        if w.ndim == 1:
            w, g, m, v = (a.reshape(1, -1) for a in (w, g, m, v))
        d, nm, nv = _adamw(w, g, m, v, f"adamw_{n}")
        deltas.append(d.reshape(shape))
        new_m.append(nm.reshape(shape))
        new_v.append(nv.reshape(shape))
    return (loss, grad_x, *gs, *deltas, *new_m, *new_v)
```

```python
import functools
import math

import jax
import jax.numpy as jnp
from jax import lax
from jax.experimental import pallas as pl
from jax.experimental.pallas import tpu as pltpu

F32 = jnp.float32
BF16 = jnp.bfloat16
MESH = pl.DeviceIdType.MESH

V7X_LANES = 128
VMEM_LIMIT = 56 * 1024 * 1024
VMEM_TILE_BUDGET = 40 * 1024 * 1024

ROPE_DIM = 64
ROPE_THETA = 10000.0
EPS = 1e-6
ROW_ALIGN = 384
ATT_BLK = 384
ROPE_ROWS = 128

ADAM_LR = 0.001
ADAM_B1 = 0.9
ADAM_B2 = 0.999
ADAM_EPS = 1e-08
ADAM_WD = 0.01
ADAM_STEP = 10

NEG = -1e30
NT_DIMS = (((1,), (1,)), ((), ()))
TN_DIMS = (((0,), (0,)), ((), ()))
NN_DIMS = (((1,), (0,)), ((), ()))


def _cparams(*sem):
    return pltpu.CompilerParams(dimension_semantics=sem, vmem_limit_bytes=VMEM_LIMIT)


def _divisor_tile(n, cap, align):
    best = None
    t = align
    while t <= min(n, cap):
        if n % t == 0:
            best = t
        t += align
    return best if best is not None else n


def _mm_tiles(M, N, K, a_bytes, b_bytes, o_bytes, has_res):
    best = None
    for tm in sorted({_divisor_tile(M, c, 128) for c in (1408, 1024, 704, 512, 384, 256, 128)}, reverse=True):
        for tn in sorted({_divisor_tile(N, c, 128) for c in (1024, 512, 256, 128)}, reverse=True):
            for tk in sorted({_divisor_tile(K, c, 128) for c in (4096, 2048, 1408, 1024, 704, 512, 384, 256, 128)},
                             reverse=True):
                need = 2 * (tm * tk * a_bytes + tk * tn * b_bytes + tm * tn * o_bytes)
                need += 2 * tm * tn * 4 if has_res else 0
                need += tm * tn * 4 if tk != K else 0
                need += tm * tk * 2 if a_bytes != 2 else 0
                need += tk * tn * 2 if b_bytes != 2 else 0
                need += tm * tn * 4
                if need > VMEM_TILE_BUDGET:
                    continue
                score = (tm * tn / (tm + tn), tk)
                if best is None or score > best[0]:
                    best = (score, (tm, tn, tk))
    assert best is not None, (M, N, K)
    return best[1]


def _matmul(a, b, *, mode, out_dtype, name, residual=None):
    if mode == "nn":
        (M, K), N = a.shape, b.shape[1]
    elif mode == "nt":
        (M, K), N = a.shape, b.shape[0]
    else:
        (K, M), N = a.shape, b.shape[1]
    tm, tn, tk = _mm_tiles(M, N, K, a.dtype.itemsize, b.dtype.itemsize, jnp.dtype(out_dtype).itemsize,
                           residual is not None)
    nk = K // tk
    dims = {"nn": NN_DIMS, "nt": NT_DIMS, "tn": TN_DIMS}[mode]

    def body(*refs):
        if residual is not None:
            a_ref, b_ref, r_ref, o_ref = refs[:4]
        else:
            a_ref, b_ref, o_ref = refs[:3]
            r_ref = None
        part = lax.dot_general(a_ref[...].astype(BF16), b_ref[...].astype(BF16), dims,
                               preferred_element_type=F32)
        if nk == 1:
            if r_ref is not None:
                part = part + r_ref[...]
            o_ref[...] = part.astype(o_ref.dtype)
            return
        acc_ref = refs[-1]
        k = pl.program_id(2)

        @pl.when(k == 0)
        def _():
            acc_ref[...] = part

        @pl.when(k > 0)
        def _():
            acc_ref[...] += part

        @pl.when(k == nk - 1)
        def _():
            r = acc_ref[...]
            if r_ref is not None:
                r = r + r_ref[...]
            o_ref[...] = r.astype(o_ref.dtype)

    if mode == "tn":
        a_spec = pl.BlockSpec((tk, tm), lambda i, j, k: (k, i))
    else:
        a_spec = pl.BlockSpec((tm, tk), lambda i, j, k: (i, k))
    if mode == "nt":
        b_spec = pl.BlockSpec((tn, tk), lambda i, j, k: (j, k))
    else:
        b_spec = pl.BlockSpec((tk, tn), lambda i, j, k: (k, j))
    o_spec = pl.BlockSpec((tm, tn), lambda i, j, k: (i, j))
    in_specs = [a_spec, b_spec]
    args = [a, b]
    if residual is not None:
        in_specs.append(o_spec)
        args.append(residual)
    return pl.pallas_call(
        body, name=name,
        out_shape=jax.ShapeDtypeStruct((M, N), out_dtype),
        grid=(M // tm, N // tn, nk),
        in_specs=in_specs, out_specs=o_spec,
        scratch_shapes=[pltpu.VMEM((tm, tn), F32)] if nk > 1 else [],
        compiler_params=_cparams("parallel", "parallel", "arbitrary"),
    )(*args)


def _row_tile(rows, width, n_arrays):
    cap = max(16, VMEM_TILE_BUDGET // (2 * n_arrays * width * 4))
    return _divisor_tile(rows, min(cap, 384), 16)


def _rms_fwd(x, g, *, col0, out_dtype, name):
    T = x.shape[0]
    W = g.shape[-1]
    assert col0 % W == 0
    cb = col0 // W
    tm = _row_tile(T, W, 3)

    def body(x_ref, g_ref, o_ref):
        xv = x_ref[...].astype(F32)
        r = lax.rsqrt(jnp.mean(xv * xv, axis=-1, keepdims=True) + EPS)
        o_ref[...] = ((xv * r) * g_ref[...]).astype(o_ref.dtype)

    return pl.pallas_call(
        body, name=name,
        out_shape=jax.ShapeDtypeStruct((T, W), out_dtype),
        grid=(T // tm,),
        in_specs=[pl.BlockSpec((tm, W), lambda i: (i, cb)), pl.BlockSpec((1, W), lambda i: (0, 0))],
        out_specs=pl.BlockSpec((tm, W), lambda i: (i, 0)),
        compiler_params=_cparams("parallel"),
    )(x, g.reshape(1, W))


def _rms_bwd(x, g, dy, *, col0, out_dtype, name, residual=None):
    T = x.shape[0]
    W = g.shape[-1]
    assert col0 % W == 0
    cb = col0 // W
    tm = _row_tile(T, W, 6)
    nt = T // tm

    def body(*refs):
        if residual is not None:
            x_ref, g_ref, dy_ref, r_ref, dx_ref, dg_ref, acc_ref = refs
        else:
            x_ref, g_ref, dy_ref, dx_ref, dg_ref, acc_ref = refs
            r_ref = None
        i = pl.program_id(0)
        xv = x_ref[...].astype(F32)
        r = lax.rsqrt(jnp.mean(xv * xv, axis=-1, keepdims=True) + EPS)
        xh = xv * r
        dyv = dy_ref[...].astype(F32)
        dxh = dyv * g_ref[...]
        dx = r * (dxh - xh * jnp.mean(dxh * xh, axis=-1, keepdims=True))
        if r_ref is not None:
            dx = dx + r_ref[...]
        dx_ref[...] = dx.astype(dx_ref.dtype)
        part = jnp.sum((dyv * xh).reshape(tm // 8, 8, W), axis=0)

        @pl.when(i == 0)
        def _():
            acc_ref[...] = part

        @pl.when(i > 0)
        def _():
            acc_ref[...] += part

        @pl.when(i == nt - 1)
        def _():
            dg_ref[...] = jnp.sum(acc_ref[...], axis=0, keepdims=True)

    row = pl.BlockSpec((tm, W), lambda i: (i, 0))
    in_specs = [pl.BlockSpec((tm, W), lambda i: (i, cb)), pl.BlockSpec((1, W), lambda i: (0, 0)), row]
    args = [x, g.reshape(1, W), dy]
    if residual is not None:
        in_specs.append(row)
        args.append(residual)
    return pl.pallas_call(
        body, name=name,
        out_shape=(jax.ShapeDtypeStruct((T, W), out_dtype), jax.ShapeDtypeStruct((1, W), F32)),
        grid=(nt,),
        in_specs=in_specs,
        out_specs=(row, pl.BlockSpec((1, W), lambda i: (0, 0))),
        scratch_shapes=[pltpu.VMEM((8, W), F32)],
        compiler_params=_cparams("arbitrary"),
    )(*args)


def _gate_fwd(o, proj, g, *, zcol0, name):
    T, W = o.shape
    assert zcol0 % W == 0
    zb = zcol0 // W
    tm = _row_tile(T, W, 4)

    def body(o_ref, z_ref, g_ref, y_ref):
        ov = o_ref[...]
        r = lax.rsqrt(jnp.mean(ov * ov, axis=-1, keepdims=True) + EPS)
        z = z_ref[...]
        sg = 1.0 / (1.0 + jnp.exp(-z))
        y_ref[...] = (((ov * r) * g_ref[...]) * (z * sg)).astype(y_ref.dtype)

    return pl.pallas_call(
        body, name=name,
        out_shape=jax.ShapeDtypeStruct((T, W), BF16),
        grid=(T // tm,),
        in_specs=[pl.BlockSpec((tm, W), lambda i: (i, 0)), pl.BlockSpec((tm, W), lambda i: (i, zb)),
                  pl.BlockSpec((1, W), lambda i: (0, 0))],
        out_specs=pl.BlockSpec((tm, W), lambda i: (i, 0)),
        compiler_params=_cparams("parallel"),
    )(o, proj, g.reshape(1, W))


def _gate_bwd(dy, o, proj, g, *, grp, zcol0, name):
    T, W = o.shape
    assert zcol0 % W == 0
    zb = zcol0 // W
    tm = _row_tile(T, W, 8)
    nt = T // tm

    def body(dy_ref, o_ref, z_ref, g_ref, do_ref, dz_ref, dg_ref, acc_ref):
        i = pl.program_id(0)
        ov = o_ref[...]
        r = lax.rsqrt(jnp.mean(ov * ov, axis=-1, keepdims=True) + EPS)
        xh = ov * r
        gv = g_ref[...]
        z = z_ref[...]
        sg = 1.0 / (1.0 + jnp.exp(-z))
        dyv = dy_ref[...]
        dn = dyv * (z * sg)
        dz_ref[...] = (dyv * (xh * gv) * (sg * (1.0 + z * (1.0 - sg)))).astype(dz_ref.dtype)
        dxh = dn * gv
        do_ref[...] = r * (dxh - xh * jnp.mean(dxh * xh, axis=-1, keepdims=True))
        part = jnp.sum((dn * xh).reshape(tm // 8, 8, W), axis=0)

        @pl.when(i == 0)
        def _():
            acc_ref[...] = part

        @pl.when(i > 0)
        def _():
            acc_ref[...] += part

        @pl.when(i == nt - 1)
        def _():
            dg_ref[...] = jnp.sum(acc_ref[...], axis=0, keepdims=True)

    row = pl.BlockSpec((tm, W), lambda i: (i, 0))
    return pl.pallas_call(
        body, name=name,
        out_shape=(jax.ShapeDtypeStruct((T, W), F32), jax.ShapeDtypeStruct((T, W), BF16),
                   jax.ShapeDtypeStruct((1, W), F32)),
        grid=(nt,),
        in_specs=[pl.BlockSpec((tm, W), lambda i: (i, grp)), row, pl.BlockSpec((tm, W), lambda i: (i, zb)),
                  pl.BlockSpec((1, W), lambda i: (0, 0))],
        out_specs=(row, row, pl.BlockSpec((1, W), lambda i: (0, 0))),
        scratch_shapes=[pltpu.VMEM((8, W), F32)],
        compiler_params=_cparams("arbitrary"),
    )(dy, o, proj, g.reshape(1, W))


def _rope_fwd(q, kv, proj, cosp, sinp, *, H, kr_col0, name):
    T = q.shape[0]
    HB = H * V7X_LANES
    tm = ROPE_ROWS
    krb = kr_col0 // V7X_LANES

    def body(q_ref, kv_ref, kra_ref, krb_ref, cos_ref, sin_ref, qc_ref, kc_ref, v_ref):
        cos = cos_ref[...]
        sin = sin_ref[...]
        kr = (kra_ref[...] * cos + krb_ref[...] * sin).astype(BF16)
        for h in range(H):
            lo, hi = h * 128, (h + 1) * 128
            qc_ref[:, 2 * lo:2 * lo + 128] = q_ref[:, lo:hi].astype(BF16)
            qc_ref[:, 2 * lo + 128:2 * hi] = (q_ref[:, HB + lo:HB + hi] * cos
                                              + q_ref[:, 2 * HB + lo:2 * HB + hi] * sin).astype(BF16)
            kc_ref[:, 2 * lo:2 * lo + 128] = kv_ref[:, 2 * lo:2 * lo + 128].astype(BF16)
            kc_ref[:, 2 * lo + 128:2 * hi] = kr
            v_ref[:, lo:hi] = kv_ref[:, 2 * lo + 128:2 * hi].astype(BF16)

    tab = pl.BlockSpec((tm, 128), lambda i: (i, 0))
    return pl.pallas_call(
        body, name=name,
        out_shape=(jax.ShapeDtypeStruct((T, 2 * HB), BF16), jax.ShapeDtypeStruct((T, 2 * HB), BF16),
                   jax.ShapeDtypeStruct((T, HB), BF16)),
        grid=(T // tm,),
        in_specs=[pl.BlockSpec((tm, 3 * HB), lambda i: (i, 0)), pl.BlockSpec((tm, 2 * HB), lambda i: (i, 0)),
                  pl.BlockSpec((tm, 128), lambda i: (i, krb)), pl.BlockSpec((tm, 128), lambda i: (i, krb + 1)),
                  tab, tab],
        out_specs=(pl.BlockSpec((tm, 2 * HB), lambda i: (i, 0)), pl.BlockSpec((tm, 2 * HB), lambda i: (i, 0)),
                   pl.BlockSpec((tm, HB), lambda i: (i, 0))),
        compiler_params=_cparams("parallel"),
    )(q, kv, proj, proj, cosp, sinp)


def _rope_bwd(dqc, dkc, dv, cosp, sinp, *, H, name):
    T = dqc.shape[0]
    HB = H * V7X_LANES
    tm = ROPE_ROWS

    def body(dqc_ref, dkc_ref, dv_ref, cos_ref, sin_ref, dq_ref, dkv_ref, dkr_ref):
        cos = cos_ref[...]
        sin = sin_ref[...]
        gk = jnp.zeros((tm, 128), F32)
        for h in range(H):
            lo, hi = h * 128, (h + 1) * 128
            dq_ref[:, lo:hi] = dqc_ref[:, 2 * lo:2 * lo + 128].astype(BF16)
            gq = dqc_ref[:, 2 * lo + 128:2 * hi]
            dq_ref[:, HB + lo:HB + hi] = (gq * cos).astype(BF16)
            dq_ref[:, 2 * HB + lo:2 * HB + hi] = (gq * sin).astype(BF16)
            dkv_ref[:, 2 * lo:2 * lo + 128] = dkc_ref[:, 2 * lo:2 * lo + 128].astype(BF16)
            dkv_ref[:, 2 * lo + 128:2 * hi] = dv_ref[:, lo:hi].astype(BF16)
            gk = gk + dkc_ref[:, 2 * lo + 128:2 * hi]
        dkr_ref[:, 0:128] = (gk * cos).astype(BF16)
        dkr_ref[:, 128:256] = (gk * sin).astype(BF16)

    tab = pl.BlockSpec((tm, 128), lambda i: (i, 0))
    return pl.pallas_call(
        body, name=name,
        out_shape=(jax.ShapeDtypeStruct((T, 3 * HB), BF16), jax.ShapeDtypeStruct((T, 2 * HB), BF16),
                   jax.ShapeDtypeStruct((T, 256), BF16)),
        grid=(T // tm,),
        in_specs=[pl.BlockSpec((tm, 2 * HB), lambda i: (i, 0)), pl.BlockSpec((tm, 2 * HB), lambda i: (i, 0)),
                  pl.BlockSpec((tm, HB), lambda i: (i, 0)), tab, tab],
        out_specs=(pl.BlockSpec((tm, 3 * HB), lambda i: (i, 0)), pl.BlockSpec((tm, 2 * HB), lambda i: (i, 0)),
                   pl.BlockSpec((tm, 256), lambda i: (i, 0))),
        compiler_params=_cparams("parallel"),
    )(dqc, dkc, dv, cosp, sinp)


def _mla_fwd(qc, kc, v, *, H, scale, name):
    T = qc.shape[0]
    tq = tk = ATT_BLK
    nq = T // tq

    def body(q_ref, k_ref, v_ref, o_ref, lse_ref):
        i = pl.program_id(1)
        q = q_ref[...]
        row = i * tq + lax.broadcasted_iota(jnp.int32, (tq, tk), 0)
        col = lax.broadcasted_iota(jnp.int32, (tq, tk), 1)

        def step(j, carry, masked):
            m, l, acc = carry
            off = pl.multiple_of(j * tk, tk)
            ks = k_ref[pl.ds(off, tk), :]
            vs = v_ref[pl.ds(off, tk), :]
            s = lax.dot_general(q, ks, NT_DIMS, preferred_element_type=F32) * scale
            if masked:
                s = jnp.where(col + j * tk <= row, s, NEG)
            m_new = jnp.maximum(m, jnp.max(s, axis=1, keepdims=True))
            alpha = jnp.exp(m - m_new)
            p = jnp.exp(s - m_new)
            l = alpha * l + jnp.sum(p, axis=1, keepdims=True)
            acc = alpha * acc + jnp.dot(p.astype(BF16), vs, preferred_element_type=F32)
            return m_new, l, acc

        n_full = (i * tq) // tk
        n_tot = ((i + 1) * tq + tk - 1) // tk
        carry = (jnp.full((tq, 1), NEG, F32), jnp.zeros((tq, 1), F32), jnp.zeros((tq, 128), F32))
        carry = lax.fori_loop(0, n_full, functools.partial(step, masked=False), carry)
        m, l, acc = lax.fori_loop(n_full, n_tot, functools.partial(step, masked=True), carry)
        o_ref[...] = acc / l
        lse_ref[0] = m + jnp.log(l)

    return pl.pallas_call(
        body, name=name,
        out_shape=(jax.ShapeDtypeStruct((T, H * 128), F32), jax.ShapeDtypeStruct((H, T, 1), F32)),
        grid=(H, nq),
        in_specs=[pl.BlockSpec((tq, 256), lambda h, i: (i, h)), pl.BlockSpec((T, 256), lambda h, i: (0, h)),
                  pl.BlockSpec((T, 128), lambda h, i: (0, h))],
        out_specs=(pl.BlockSpec((tq, 128), lambda h, i: (i, h)), pl.BlockSpec((1, tq, 1), lambda h, i: (h, i, 0))),
        compiler_params=_cparams("parallel", "parallel"),
    )(qc, kc, v)


def _mla_bwd(qc, kc, v, o, do, lse, *, H, scale, name):
    T = qc.shape[0]
    tq = tk = ATT_BLK
    nq, nk = T // tq, T // tk

    def body(q_ref, k_ref, v_ref, o_ref, do_ref, lse_ref, dq_ref, dk_ref, dv_ref, delta_ref):
        dq_ref[...] = jnp.zeros_like(dq_ref)

        def fill_delta(i, c):
            off = pl.multiple_of(i * tq, tq)
            delta_ref[pl.ds(off, tq), :] = jnp.sum(do_ref[pl.ds(off, tq), :] * o_ref[pl.ds(off, tq), :],
                                                   axis=1, keepdims=True)
            return c

        lax.fori_loop(0, nq, fill_delta, 0)
        rowi = lax.broadcasted_iota(jnp.int32, (tq, tk), 0)
        coli = lax.broadcasted_iota(jnp.int32, (tq, tk), 1)

        def kblock(j, c):
            koff = pl.multiple_of(j * tk, tk)
            ks = k_ref[pl.ds(koff, tk), :]
            vs = v_ref[pl.ds(koff, tk), :]

            def qstep(i, carry, masked):
                dk, dv = carry
                qoff = pl.multiple_of(i * tq, tq)
                qs = q_ref[pl.ds(qoff, tq), :]
                dob = do_ref[pl.ds(qoff, tq), :].astype(BF16)
                s = lax.dot_general(qs, ks, NT_DIMS, preferred_element_type=F32) * scale
                if masked:
                    s = jnp.where(coli + j * tk <= rowi + i * tq, s, NEG)
                p = jnp.exp(s - lse_ref[0, pl.ds(qoff, tq), :])
                dv = dv + lax.dot_general(p.astype(BF16), dob, TN_DIMS, preferred_element_type=F32)
                dp = lax.dot_general(dob, vs, NT_DIMS, preferred_element_type=F32)
                ds = (p * (dp - delta_ref[pl.ds(qoff, tq), :]) * scale).astype(BF16)
                dk = dk + lax.dot_general(ds, qs, TN_DIMS, preferred_element_type=F32)
                dq_ref[pl.ds(qoff, tq), :] += jnp.dot(ds, ks, preferred_element_type=F32)
                return dk, dv

            i0 = (j * tk) // tq
            i1 = jnp.minimum(((j + 1) * tk + tq - 1) // tq, nq)
            carry = (jnp.zeros((tk, 256), F32), jnp.zeros((tk, 128), F32))
            carry = lax.fori_loop(i0, i1, functools.partial(qstep, masked=True), carry)
            dk, dv = lax.fori_loop(i1, nq, functools.partial(qstep, masked=False), carry)
            dk_ref[pl.ds(koff, tk), :] = dk
            dv_ref[pl.ds(koff, tk), :] = dv
            return c

        lax.fori_loop(0, nk, kblock, 0)

    wide = pl.BlockSpec((T, 256), lambda h: (0, h))
    narrow = pl.BlockSpec((T, 128), lambda h: (0, h))
    return pl.pallas_call(
        body, name=name,
        out_shape=(jax.ShapeDtypeStruct((T, H * 256), F32), jax.ShapeDtypeStruct((T, H * 256), F32),
                   jax.ShapeDtypeStruct((T, H * 128), F32)),
        grid=(H,),
        in_specs=[wide, wide, narrow, narrow, narrow, pl.BlockSpec((1, T, 1), lambda h: (h, 0, 0))],
        out_specs=(wide, wide, narrow),
        scratch_shapes=[pltpu.VMEM((T, 1), F32)],
        compiler_params=_cparams("parallel"),
    )(qc, kc, v, o, do, lse)


def _log_sigmoid_pair(z):
    e = jnp.exp(-jnp.abs(z))
    lb = jnp.minimum(z, 0.0) - jnp.log(1.0 + e)
    inv = 1.0 / (1.0 + e)
    sg = jnp.where(z >= 0.0, inv, e * inv)
    return lb, lb - z, sg


def _tri_dot(x, tri):
    hi = x.astype(BF16)
    lo = (x - hi.astype(F32)).astype(BF16)
    return jnp.dot(hi, tri, preferred_element_type=F32) + jnp.dot(lo, tri, preferred_element_type=F32)


def _sb_fwd(proj, *, H, qcol0, kcol0, vcol0, scale, name):
    T = proj.shape[0]
    tq = tk = ATT_BLK
    nq = T // tq
    qb, kb, vb = qcol0 // 128, kcol0 // 128, vcol0 // 128

    def body(q_ref, k_ref, v_ref, y_ref, tot_ref):
        i = pl.program_id(1)
        q = q_ref[...].astype(BF16)
        row = i * tq + lax.broadcasted_iota(jnp.int32, (tq, tk), 0)
        col = lax.broadcasted_iota(jnp.int32, (tq, tk), 1)
        r_i = lax.broadcasted_iota(jnp.int32, (tk, tk), 0)
        c_i = lax.broadcasted_iota(jnp.int32, (tk, tk), 1)
        tri_after = (r_i > c_i).astype(BF16)

        def step(idx, carry, masked, top):
            rem, acc = carry
            j = top - 1 - idx
            off = pl.multiple_of(j * tk, tk)
            ks = k_ref[pl.ds(off, tk), :].astype(BF16)
            vs = v_ref[pl.ds(off, tk), :].astype(BF16)
            z = lax.dot_general(q, ks, NT_DIMS, preferred_element_type=F32) * scale
            lb, lom, _ = _log_sigmoid_pair(z)
            if masked:
                valid = col + j * tk < row
                lom = jnp.where(valid, lom, 0.0)
            a = jnp.exp(lb + _tri_dot(lom, tri_after) + rem)
            if masked:
                a = jnp.where(valid, a, 0.0)
            acc = acc + jnp.dot(a.astype(BF16), vs, preferred_element_type=F32)
            return rem + jnp.sum(lom, axis=1, keepdims=True), acc

        n_full = (i * tq) // tk
        n_tot = ((i + 1) * tq + tk - 1) // tk
        carry = (jnp.zeros((tq, 1), F32), jnp.zeros((tq, 128), F32))
        carry = lax.fori_loop(0, n_tot - n_full, functools.partial(step, masked=True, top=n_tot), carry)
        rem, acc = lax.fori_loop(0, n_full, functools.partial(step, masked=False, top=n_full), carry)
        y_ref[...] = acc
        tot_ref[0] = rem

    return pl.pallas_call(
        body, name=name,
        out_shape=(jax.ShapeDtypeStruct((T, H * 128), F32), jax.ShapeDtypeStruct((H, T, 1), F32)),
        grid=(H, nq),
        in_specs=[pl.BlockSpec((tq, 128), lambda h, i: (i, qb + h)), pl.BlockSpec((T, 128), lambda h, i: (0, kb + h)),
                  pl.BlockSpec((T, 128), lambda h, i: (0, vb + h))],
        out_specs=(pl.BlockSpec((tq, 128), lambda h, i: (i, h)), pl.BlockSpec((1, tq, 1), lambda h, i: (h, i, 0))),
        compiler_params=_cparams("parallel", "parallel"),
    )(proj, proj, proj)


def _sb_bwd(proj, dy, tot, *, H, qcol0, kcol0, vcol0, scale, name):
    T = proj.shape[0]
    tq = tk = ATT_BLK
    nq = T // tq
    qb, kb, vb = qcol0 // 128, kcol0 // 128, vcol0 // 128

    def body(q_ref, k_ref, v_ref, dy_ref, tot_ref, dq_ref, dk_ref, dv_ref, dk_acc, dv_acc):
        dk_acc[...] = jnp.zeros_like(dk_acc)
        dv_acc[...] = jnp.zeros_like(dv_acc)
        rowi = lax.broadcasted_iota(jnp.int32, (tq, tk), 0)
        coli = lax.broadcasted_iota(jnp.int32, (tq, tk), 1)
        r_i = lax.broadcasted_iota(jnp.int32, (tk, tk), 0)
        c_i = lax.broadcasted_iota(jnp.int32, (tk, tk), 1)
        tri_upto = (r_i <= c_i).astype(BF16)
        tri_before = (r_i < c_i).astype(BF16)

        def qblock(i, c):
            qoff = pl.multiple_of(i * tq, tq)
            qs = q_ref[pl.ds(qoff, tq), :].astype(BF16)
            dyb = dy_ref[pl.ds(qoff, tq), :].astype(BF16)

            def step(j, carry, masked):
                rem, pre, dq = carry
                koff = pl.multiple_of(j * tk, tk)
                ks = k_ref[pl.ds(koff, tk), :].astype(BF16)
                vs = v_ref[pl.ds(koff, tk), :].astype(BF16)
                z = lax.dot_general(qs, ks, NT_DIMS, preferred_element_type=F32) * scale
                lb, lom, sg = _log_sigmoid_pair(z)
                if masked:
                    valid = coli + j * tk < rowi + i * tq
                    lom = jnp.where(valid, lom, 0.0)
                a = jnp.exp(lb + rem - _tri_dot(lom, tri_upto))
                if masked:
                    a = jnp.where(valid, a, 0.0)
                dv_acc[pl.ds(koff, tk), :] += lax.dot_general(a.astype(BF16), dyb, TN_DIMS,
                                                              preferred_element_type=F32)
                de = a * lax.dot_general(dyb, vs, NT_DIMS, preferred_element_type=F32)
                before = pre + _tri_dot(de, tri_before)
                dz = de * (1.0 - sg) - before * sg
                if masked:
                    dz = jnp.where(valid, dz, 0.0)
                dzb = (dz * scale).astype(BF16)
                dk_acc[pl.ds(koff, tk), :] += lax.dot_general(dzb, qs, TN_DIMS, preferred_element_type=F32)
                dq = dq + jnp.dot(dzb, ks, preferred_element_type=F32)
                return (rem - jnp.sum(lom, axis=1, keepdims=True), pre + jnp.sum(de, axis=1, keepdims=True), dq)

            n_full = (i * tq) // tk
            n_tot = ((i + 1) * tq + tk - 1) // tk
            carry = (tot_ref[0, pl.ds(qoff, tq), :], jnp.zeros((tq, 1), F32), jnp.zeros((tq, 128), F32))
            carry = lax.fori_loop(0, n_full, functools.partial(step, masked=False), carry)
            _, _, dq = lax.fori_loop(n_full, n_tot, functools.partial(step, masked=True), carry)
            dq_ref[pl.ds(qoff, tq), :] = dq.astype(dq_ref.dtype)
            return c

        lax.fori_loop(0, nq, qblock, 0)
        dk_ref[...] = dk_acc[...].astype(dk_ref.dtype)
        dv_ref[...] = dv_acc[...].astype(dv_ref.dtype)

    def seg(b):
        return pl.BlockSpec((T, 128), lambda h: (0, b + h))

    out = pl.BlockSpec((T, 128), lambda h: (0, h))
    return pl.pallas_call(
        body, name=name,
        out_shape=tuple(jax.ShapeDtypeStruct((T, H * 128), BF16) for _ in range(3)),
        grid=(H,),
        in_specs=[seg(qb), seg(kb), seg(vb), out, pl.BlockSpec((1, T, 1), lambda h: (h, 0, 0))],
        out_specs=(out, out, out),
        scratch_shapes=[pltpu.VMEM((T, 128), F32), pltpu.VMEM((T, 128), F32)],
        compiler_params=_cparams("parallel"),
    )(proj, proj, proj, dy, tot)


def _final_loss(h, g, target, *, row0, n_rows, name):
    T, D = h.shape
    tm = _row_tile(T, D, 6)
    nt = T // tm

    def body(h_ref, g_ref, t_ref, dh_ref, dg_ref, loss_ref, acc_ref, lacc_ref):
        i = pl.program_id(0)
        xv = h_ref[...]
        r = lax.rsqrt(jnp.mean(xv * xv, axis=-1, keepdims=True) + EPS)
        xh = xv * r
        gv = g_ref[...]
        rows = i * tm + lax.broadcasted_iota(jnp.int32, (tm, 1), 0)
        valid = (rows >= row0) & (rows < row0 + n_rows)
        err = jnp.where(valid, xh * gv - t_ref[...], 0.0)
        dout = err * (1.0 / D)
        dxh = dout * gv
        dh_ref[...] = r * (dxh - xh * jnp.mean(dxh * xh, axis=-1, keepdims=True))
        part = jnp.sum((dout * xh).reshape(tm // 8, 8, D), axis=0)
        lpart = jnp.sum((err * err).reshape(tm // 8, 8, D), axis=0)

        @pl.when(i == 0)
        def _():
            acc_ref[...] = part
            lacc_ref[...] = lpart

        @pl.when(i > 0)
        def _():
            acc_ref[...] += part
            lacc_ref[...] += lpart

        @pl.when(i == nt - 1)
        def _():
            dg_ref[...] = jnp.sum(acc_ref[...], axis=0, keepdims=True)
            loss_ref[...] = (0.5 / D) * jnp.sum(jnp.sum(lacc_ref[...], axis=0, keepdims=True), axis=1, keepdims=True)

    row = pl.BlockSpec((tm, D), lambda i: (i, 0))
    vec = pl.BlockSpec((1, D), lambda i: (0, 0))
    return pl.pallas_call(
        body, name=name,
        out_shape=(jax.ShapeDtypeStruct((T, D), F32), jax.ShapeDtypeStruct((1, D), F32),
                   jax.ShapeDtypeStruct((1, 1), F32)),
        grid=(nt,),
        in_specs=[row, vec, row],
        out_specs=(row, vec, pl.BlockSpec((1, 1), lambda i: (0, 0))),
        scratch_shapes=[pltpu.VMEM((8, D), F32), pltpu.VMEM((8, D), F32)],
        compiler_params=_cparams("arbitrary"),
    )(h, g.reshape(1, D), target)


def _elementwise(fn, args, out_dtypes, name):
    shape = args[0].shape
    C = shape[-1]
    R = math.prod(shape[:-1])
    n = len(args) + len(out_dtypes)
    cap = max(16, (VMEM_TILE_BUDGET // 2) // (2 * n * C * 4))
    tr = _divisor_tile(R, cap, 16)
    n_in = len(args)

    def body(*refs):
        outs = fn(*[r[...] for r in refs[:n_in]])
        for o_ref, val in zip(refs[n_in:], outs):
            o_ref[...] = val.astype(o_ref.dtype)

    spec = pl.BlockSpec((tr, C), lambda i: (i, 0))
    res = pl.pallas_call(
        body, name=name,
        out_shape=tuple(jax.ShapeDtypeStruct((R, C), dt) for dt in out_dtypes),
        grid=(R // tr,),
        in_specs=[spec] * n_in, out_specs=tuple([spec] * len(out_dtypes)),
        compiler_params=_cparams("parallel"),
    )(*[a.reshape(R, C) for a in args])
    return tuple(r.reshape(shape) for r in res)


def _adamw_math(w, g, m, v):
    m = ADAM_B1 * m + (1.0 - ADAM_B1) * g
    v = ADAM_B2 * v + (1.0 - ADAM_B2) * (g * g)
    m_hat = m / (1.0 - ADAM_B1 ** ADAM_STEP)
    v_hat = v / (1.0 - ADAM_B2 ** ADAM_STEP)
    delta = -ADAM_LR * (m_hat / (jnp.sqrt(v_hat) + ADAM_EPS) + ADAM_WD * w)
    return delta, m, v


def _adamw(w, g, m, v, name):
    return _elementwise(_adamw_math, [w, g, m, v], [F32, F32, F32], name)


ANY = pl.BlockSpec(memory_space=pl.ANY)


def _position():
    return lax.axis_index("x"), lax.axis_index("y"), lax.axis_index("c")


def _gather_shards(a, name):
    def body(a_ref, o_ref, send_sems, recv_sems):
        x, y, c = _position()
        p = 2 * x + y
        chips = [(1 - x, y), (x, 1 - y), (1 - x, 1 - y)]

        def copy(k, src, dst, to):
            return pltpu.make_async_remote_copy(src_ref=src, dst_ref=dst, send_sem=send_sems.at[k],
                                                recv_sem=recv_sems.at[k], device_id=to, device_id_type=MESH)

        first = [copy(k, a_ref.at[c], o_ref.at[p, c], (qx, qy, c)) for k, (qx, qy) in enumerate(chips)]
        for cp in first:
            cp.start()
        passed = []
        for k, (qx, qy) in enumerate(chips):
            land = o_ref.at[2 * qx + qy, c]
            copy(k, land, land, (x, y, c)).wait_recv()
            fwd = copy(3 + k, land, land, (x, y, 1 - c))
            fwd.start()
            passed.append(fwd)
        for k, (qx, qy) in enumerate(chips):
            land = o_ref.at[2 * qx + qy, 1 - c]
            copy(3 + k, land, land, (x, y, c)).wait_recv()
        for cp in first + passed:
            cp.wait_send()

    return pl.pallas_call(
        body, name=name,
        out_shape=jax.ShapeDtypeStruct((4,) + a.shape, a.dtype),
        in_specs=[ANY], out_specs=ANY,
        scratch_shapes=[pltpu.SemaphoreType.DMA((6,)), pltpu.SemaphoreType.DMA((6,))],
    )(a)


def _with_own(gathered, own):
    x, y, _ = _position()
    p = 2 * x + y
    return [jnp.where(p == q, own, gathered[q]) for q in range(4)]


def _swap_halves(g, name):
    def body(g_ref, o_ref, send_sem, recv_sem):
        x, y, c = _position()
        cp = pltpu.make_async_remote_copy(src_ref=g_ref.at[1 - c], dst_ref=o_ref, send_sem=send_sem,
                                          recv_sem=recv_sem, device_id=(x, y, 1 - c), device_id_type=MESH)
        cp.start()
        cp.wait()

    return pl.pallas_call(
        body, name=name,
        out_shape=jax.ShapeDtypeStruct(g.shape[1:], g.dtype),
        in_specs=[ANY], out_specs=ANY,
        scratch_shapes=[pltpu.SemaphoreType.DMA, pltpu.SemaphoreType.DMA],
    )(g)


def _scatter_to_chips(pb, name):
    def body(p_ref, o_ref, send_sems, recv_sems):
        x, y, c = _position()
        chips = [(1 - x, y), (x, 1 - y), (1 - x, 1 - y)]
        cps = [pltpu.make_async_remote_copy(src_ref=p_ref.at[2 * qx + qy], dst_ref=o_ref.at[k],
                                            send_sem=send_sems.at[k], recv_sem=recv_sems.at[k],
                                            device_id=(qx, qy, c), device_id_type=MESH)
               for k, (qx, qy) in enumerate(chips)]
        for cp in cps:
            cp.start()
        for cp in cps:
            cp.wait()

    return pl.pallas_call(
        body, name=name,
        out_shape=jax.ShapeDtypeStruct((3,) + pb.shape[1:], pb.dtype),
        in_specs=[ANY], out_specs=ANY,
        scratch_shapes=[pltpu.SemaphoreType.DMA((3,)), pltpu.SemaphoreType.DMA((3,))],
    )(pb)


def _join_halves(r, name):
    def body(r_ref, o_ref, send_sem, recv_sem):
        x, y, c = _position()
        cp = pltpu.make_async_remote_copy(src_ref=r_ref, dst_ref=o_ref, send_sem=send_sem,
                                          recv_sem=recv_sem, device_id=(x, y, 1 - c), device_id_type=MESH)
        cp.start()
        cp.wait()

    other = pl.pallas_call(
        body, name=name,
        out_shape=jax.ShapeDtypeStruct(r.shape, r.dtype),
        in_specs=[ANY], out_specs=ANY,
        scratch_shapes=[pltpu.SemaphoreType.DMA, pltpu.SemaphoreType.DMA],
    )(r)
    c = lax.axis_index("c")
    return jnp.stack([jnp.where(c == 0, r, other), jnp.where(c == 0, other, r)], axis=0)


def _reduce_to_shard(gh, tag):
    x, y, c = _position()
    p = 2 * x + y
    sib = _swap_halves(gh, f"swap_{tag}")
    mine = lax.dynamic_index_in_dim(gh, c, 0, keepdims=False)
    psum, pb = _elementwise(lambda a, b: (a + b, a + b), [mine, sib], [F32, BF16], f"pairsum_{tag}")
    got = _scatter_to_chips(pb, f"scatter_{tag}")
    own = lax.dynamic_index_in_dim(psum, p, 0, keepdims=False)
    (red,) = _elementwise(lambda o, a, b, d: (((o + a.astype(F32)) + b.astype(F32)) + d.astype(F32),),
                          [own, got[0], got[1], got[2]], [F32], f"chipsum_{tag}")
    return _join_halves(red, f"join_{tag}")


def _all_reduce_small(vec, name):
    R = vec.shape[0]

    def body(v_ref, o_ref, land_ref, send_sems, recv_sems):
        x, y, c = _position()
        me = 4 * x + 2 * y + c
        land_ref[me] = v_ref[...]
        cps = []
        for r in range(1, 8):
            rx, ry, rc = (r >> 2) & 1, (r >> 1) & 1, r & 1
            to = (x ^ rx, y ^ ry, c ^ rc)
            cps.append(pltpu.make_async_remote_copy(src_ref=v_ref, dst_ref=land_ref.at[me],
                                                    send_sem=send_sems.at[r - 1], recv_sem=recv_sems.at[r - 1],
                                                    device_id=to, device_id_type=MESH))
        for cp in cps:
            cp.start()
        for cp in cps:
            cp.wait()
        total = land_ref[0]
        for d in range(1, 8):
            total = total + land_ref[d]
        o_ref[...] = total

    return pl.pallas_call(
        body, name=name,
        out_shape=jax.ShapeDtypeStruct((R, 128), F32),
        in_specs=[pl.BlockSpec(memory_space=pltpu.VMEM)], out_specs=pl.BlockSpec(memory_space=pltpu.VMEM),
        scratch_shapes=[pltpu.VMEM((8, R, 128), F32), pltpu.SemaphoreType.DMA((7,)), pltpu.SemaphoreType.DMA((7,))],
    )(vec)


def _rot(w):
    half = ROPE_DIM // 2
    return jnp.concatenate([-w[..., half:], w[..., :half]], axis=-1)


def _unrot(g):
    half = ROPE_DIM // 2
    return jnp.concatenate([g[..., half:], -g[..., :half]], axis=-1)


class _Layout:
    def __init__(self, D, QL, KVL):
        self.D, self.QL, self.KVL = D, QL, KVL
        self.H = D // 256
        self.WG = self.H * 128
        WG = self.WG
        self.z_mla, self.q_sb, self.k_sb, self.v_sb, self.z_sb = 0, WG, 2 * WG, 3 * WG, 4 * WG
        self.c_q = 5 * WG
        self.c_kv = self.c_q + QL
        self.k_r = self.c_kv + KVL
        self.width = -(-(self.k_r + 256) // 512) * 512
        self.orig = (QL, KVL, ROPE_DIM, WG, WG, WG, WG, WG)

    def pack_w_in(self, w):
        cuts = []
        o = 0
        for s in self.orig:
            cuts.append(w[:, o:o + s])
            o += s
        c_q, c_kv, k_r, z_mla, q_sb, k_sb, v_sb, z_sb = cuts
        z64 = jnp.zeros((w.shape[0], 128 - ROPE_DIM), w.dtype)
        pad = jnp.zeros((w.shape[0], self.width - self.k_r - 256), w.dtype)
        return jnp.concatenate([z_mla, q_sb, k_sb, v_sb, z_sb, c_q, c_kv, k_r, z64, _rot(k_r), z64, pad], axis=1)

    def unpack_dw_in(self, g):
        WG = self.WG
        k_r = g[:, self.k_r:self.k_r + ROPE_DIM] + _unrot(g[:, self.k_r + 128:self.k_r + 128 + ROPE_DIM])
        return jnp.concatenate([g[:, self.c_q:self.c_q + self.QL], g[:, self.c_kv:self.c_kv + self.KVL], k_r,
                                g[:, 0:WG], g[:, WG:2 * WG], g[:, 2 * WG:3 * WG], g[:, 3 * WG:4 * WG],
                                g[:, 4 * WG:5 * WG]], axis=1)

    def pack_w_uq(self, w):
        H = self.H
        w3 = w.reshape(w.shape[0], H, 128 + ROPE_DIM)
        nope = w3[:, :, :128]
        r = w3[:, :, 128:]
        z = jnp.zeros(r.shape, w.dtype)
        a = jnp.concatenate([r, z], axis=-1)
        b = jnp.concatenate([_rot(r), z], axis=-1)
        return jnp.concatenate([nope.reshape(-1, H * 128), a.reshape(-1, H * 128), b.reshape(-1, H * 128)], axis=1)

    def unpack_dw_uq(self, g):
        H = self.H
        HB = H * 128
        nope = g[:, :HB].reshape(-1, H, 128)
        a = g[:, HB:2 * HB].reshape(-1, H, 128)[:, :, :ROPE_DIM]
        b = g[:, 2 * HB:].reshape(-1, H, 128)[:, :, :ROPE_DIM]
        return jnp.concatenate([nope, a + _unrot(b)], axis=-1).reshape(-1, H * (128 + ROPE_DIM))


def _rope_tables(T):
    inv_freq = ROPE_THETA ** (-jnp.arange(0, ROPE_DIM, 2, dtype=F32) / ROPE_DIM)
    ang = jnp.arange(T, dtype=jnp.int32).astype(F32)[:, None] * inv_freq[None, :]
    z = jnp.zeros((T, 128 - ROPE_DIM), F32)
    cos, sin = jnp.cos(ang), jnp.sin(ang)
    return jnp.concatenate([cos, cos, z], axis=1), jnp.concatenate([sin, sin, z], axis=1)


def _layer_fwd(h, wl, lay, tabs):
    g_norm, w_in, g_q, g_kv, w_uq, w_ukv, g_mla, g_sb, w_o = wl
    cosp, sinp = tabs
    H = lay.H
    u = _rms_fwd(h, g_norm, col0=0, out_dtype=BF16, name="rms_h")
    proj = _matmul(u, w_in, mode="nn", out_dtype=F32, name="mm_in")
    cqn = _rms_fwd(proj, g_q, col0=lay.c_q, out_dtype=BF16, name="rms_cq")
    ckvn = _rms_fwd(proj, g_kv, col0=lay.c_kv, out_dtype=BF16, name="rms_ckv")
    q = _matmul(cqn, w_uq, mode="nn", out_dtype=F32, name="mm_uq")
    kv = _matmul(ckvn, w_ukv, mode="nn", out_dtype=F32, name="mm_ukv")
    qc, kc, v = _rope_fwd(q, kv, proj, cosp, sinp, H=H, kr_col0=lay.k_r, name="rope_fwd")
    o_mla, lse = _mla_fwd(qc, kc, v, H=H, scale=1.0 / math.sqrt(128 + ROPE_DIM), name="mla_fwd")
    o_sb, tot = _sb_fwd(proj, H=H, qcol0=lay.q_sb, kcol0=lay.k_sb, vcol0=lay.v_sb,
                        scale=1.0 / math.sqrt(128), name="sb_fwd")
    y_mla = _gate_fwd(o_mla, proj, g_mla, zcol0=lay.z_mla, name="gate_fwd_mla")
    y_sb = _gate_fwd(o_sb, proj, g_sb, zcol0=lay.z_sb, name="gate_fwd_sb")
    y = jnp.concatenate([y_mla, y_sb], axis=1)
    h_out = _matmul(y, w_o, mode="nn", out_dtype=F32, name="mm_o", residual=h)
    saved = (h, u, proj, cqn, ckvn, qc, kc, v, o_mla, lse, o_sb, tot, y)
    return h_out, saved


def _layer_bwd(dh, saved, wl, lay, tabs):
    g_norm, w_in, g_q, g_kv, w_uq, w_ukv, g_mla, g_sb, w_o = wl
    h, u, proj, cqn, ckvn, qc, kc, v, o_mla, lse, o_sb, tot, y = saved
    cosp, sinp = tabs
    H = lay.H
    dy = _matmul(dh, w_o, mode="nt", out_dtype=F32, name="mm_o_dx")
    d_w_o = _matmul(y, dh, mode="tn", out_dtype=F32, name="mm_o_dw")
    do_mla, dz_mla, dg_mla = _gate_bwd(dy, o_mla, proj, g_mla, grp=0, zcol0=lay.z_mla, name="gate_bwd_mla")
    do_sb, dz_sb, dg_sb = _gate_bwd(dy, o_sb, proj, g_sb, grp=1, zcol0=lay.z_sb, name="gate_bwd_sb")
    dq_sb, dk_sb, dv_sb = _sb_bwd(proj, do_sb, tot, H=H, qcol0=lay.q_sb, kcol0=lay.k_sb, vcol0=lay.v_sb,
                                  scale=1.0 / math.sqrt(128), name="sb_bwd")
    dqc, dkc, dv = _mla_bwd(qc, kc, v, o_mla, do_mla, lse, H=H, scale=1.0 / math.sqrt(128 + ROPE_DIM),
                            name="mla_bwd")
    dq, dkv, dkr = _rope_bwd(dqc, dkc, dv, cosp, sinp, H=H, name="rope_bwd")
    d_w_uq = _matmul(cqn, dq, mode="tn", out_dtype=F32, name="mm_uq_dw")
    dcqn = _matmul(dq, w_uq, mode="nt", out_dtype=F32, name="mm_uq_dx")
    d_w_ukv = _matmul(ckvn, dkv, mode="tn", out_dtype=F32, name="mm_ukv_dw")
    dckvn = _matmul(dkv, w_ukv, mode="nt", out_dtype=F32, name="mm_ukv_dx")
    dcq, dg_q = _rms_bwd(proj, g_q, dcqn, col0=lay.c_q, out_dtype=BF16, name="rms_cq_bwd")
    dckv, dg_kv = _rms_bwd(proj, g_kv, dckvn, col0=lay.c_kv, out_dtype=BF16, name="rms_ckv_bwd")
    pad = jnp.zeros((dh.shape[0], lay.width - lay.k_r - 256), BF16)
    dproj = jnp.concatenate([dz_mla, dq_sb, dk_sb, dv_sb, dz_sb, dcq, dckv, dkr, pad], axis=1)
    d_w_in = _matmul(u, dproj, mode="tn", out_dtype=F32, name="mm_in_dw")
    du = _matmul(dproj, w_in, mode="nt", out_dtype=F32, name="mm_in_dx")
    dh_prev, dg_norm = _rms_bwd(h, g_norm, du, col0=0, out_dtype=F32, name="rms_h_bwd", residual=dh)
    grads = (dg_norm[0], lay.unpack_dw_in(d_w_in), dg_q[0], dg_kv[0], lay.unpack_dw_uq(d_w_uq), d_w_ukv,
             dg_mla[0], dg_sb[0], d_w_o)
    return dh_prev, grads


def _halves(a):
    return a.reshape((2, a.shape[0] // 2) + a.shape[1:])


def _gather_cols(w, name):
    L, K, n = w.shape
    own = w.astype(BF16)
    g = _with_own(_gather_shards(_halves(own), name).reshape(4, L, K, n), own)
    return [jnp.concatenate([g[q][l] for q in range(4)], axis=1) for l in range(L)]


def _cut_cols(g):
    L, K, N = g.shape
    return g.reshape(2, L // 2, K, 4, N // 4).transpose(0, 3, 1, 2, 4)


def kernel(x, meta_tokens, g_norm, w_in, g_q, g_kv, w_uq, w_ukv, g_out_mla, g_out_sb, w_o, g_final, loss_target, m_meta_tokens, m_g_norm, m_w_in, m_g_q, m_g_kv, m_w_uq, m_w_ukv, m_g_out_mla, m_g_out_sb, m_w_o, m_g_final, v_meta_tokens, v_g_norm, v_w_in, v_g_q, v_g_kv, v_w_uq, v_w_ukv, v_g_out_mla, v_g_out_sb, v_w_o, v_g_final):
    _, S, D = x.shape
    NM = meta_tokens.shape[0]
    L = g_norm.shape[0]
    lay = _Layout(D, g_q.shape[1], g_kv.shape[1])
    TP = -(-(NM + S) // ROW_ALIGN) * ROW_ALIGN
    tabs = _rope_tables(TP)

    w_in_full = [lay.pack_w_in(w) for w in _gather_cols(w_in, "gather_w_in")]
    w_uq_full = [lay.pack_w_uq(w) for w in _gather_cols(w_uq, "gather_w_uq")]
    w_ukv_full = _gather_cols(w_ukv, "gather_w_ukv")
    w_o_own = w_o.astype(BF16)
    w_o_g = _with_own(_gather_shards(_halves(w_o_own), "gather_w_o").reshape((4, L) + w_o.shape[1:]), w_o_own)
    w_o_full = [jnp.concatenate([w_o_g[q][l] for q in range(4)], axis=0) for l in range(L)]
    meta_g = _with_own(_gather_shards(meta_tokens.reshape(2, NM // 2, -1), "gather_meta").reshape(4, NM, -1),
                       meta_tokens)
    meta_full = jnp.concatenate(meta_g, axis=1)

    h = jnp.concatenate([meta_full, x[0], jnp.zeros((TP - NM - S, D), F32)], axis=0)
    target = jnp.pad(loss_target[0], ((NM, TP - NM - S), (0, 0)))
    weights = [(g_norm[l], w_in_full[l], g_q[l], g_kv[l], w_uq_full[l], w_ukv_full[l], g_out_mla[l],
                g_out_sb[l], w_o_full[l]) for l in range(L)]
    saved = []
    for l in range(L):
        h, s = _layer_fwd(h, weights[l], lay, tabs)
        saved.append(s)
    dh, dg_final, loss_part = _final_loss(h, g_final, target, row0=NM, n_rows=S, name="final_loss")

    layer_grads = [None] * L
    for l in reversed(range(L)):
        dh, layer_grads[l] = _layer_bwd(dh, saved[l], weights[l], lay, tabs)
    grad_x = dh[NM:NM + S][None]
    d_meta = dh[:NM]

    def stack(i):
        return jnp.stack([layer_grads[l][i] for l in range(L)], axis=0)

    small = [stack(0), stack(2), stack(3), stack(6), stack(7), dg_final[0]]
    flat = jnp.concatenate([s.reshape(-1) for s in small])
    n_flat = flat.shape[0]
    rows = -(-n_flat // (8 * 128)) * 8
    packed = jnp.pad(flat, (0, rows * 128 - n_flat)).reshape(rows, 128)
    summed = _all_reduce_small(packed, "allreduce_gains").reshape(-1)
    small_red = []
    o = 0
    for s in small:
        small_red.append(summed[o:o + s.size].reshape(s.shape))
        o += s.size
    g_g_norm, g_g_q, g_g_kv, g_g_mla, g_g_sb, g_g_final = small_red

    g_w_in = _reduce_to_shard(_cut_cols(stack(1)), "w_in").reshape(w_in.shape)
    g_w_uq = _reduce_to_shard(_cut_cols(stack(4)), "w_uq").reshape(w_uq.shape)
    g_w_ukv = _reduce_to_shard(_cut_cols(stack(5)), "w_ukv").reshape(w_ukv.shape)
    d_w_o = stack(8).reshape(2, L // 2, 4, w_o.shape[1], D).transpose(0, 2, 1, 3, 4)
    g_w_o = _reduce_to_shard(d_w_o, "w_o").reshape(w_o.shape)
    d_meta = d_meta.reshape(2, NM // 2, 4, D // 4).transpose(0, 2, 1, 3)
    g_meta = _reduce_to_shard(d_meta, "meta").reshape(meta_tokens.shape)

    loss = lax.psum(loss_part[0, 0], ("x", "y", "c"))

    names = ["meta", "g_norm", "w_in", "g_q", "g_kv", "w_uq", "w_ukv", "g_out_mla", "g_out_sb", "w_o", "g_final"]
    ws = [meta_tokens, g_norm, w_in, g_q, g_kv, w_uq, w_ukv, g_out_mla, g_out_sb, w_o, g_final]
    gs = [g_meta, g_g_norm, g_w_in, g_g_q, g_g_kv, g_w_uq, g_w_ukv, g_g_mla, g_g_sb, g_w_o, g_g_final]
    ms = [m_meta_tokens, m_g_norm, m_w_in, m_g_q, m_g_kv, m_w_uq, m_w_ukv, m_g_out_mla, m_g_out_sb, m_w_o, m_g_final]
    vs = [v_meta_tokens, v_g_norm, v_w_in, v_g_q, v_g_kv, v_w_uq, v_w_ukv, v_g_out_mla, v_g_out_sb, v_w_o, v_g_final]
    deltas, new_m, new_v = [], [], []
    for n, w, g, m, v in zip(names, ws, gs, ms, vs):
        shape = w.shape
        if w.ndim == 1:
            w, g, m, v = (a.reshape(1, -1) for a in (w, g, m, v))
        d, nm, nv = _adamw(w, g, m, v, f"adamw_{n}")
        deltas.append(d.reshape(shape))
        new_m.append(nm.reshape(shape))
        new_v.append(nv.reshape(shape))
    return (loss, grad_x, *gs, *deltas, *new_m, *new_v)
```

```python
import functools
import math

import jax
import jax.numpy as jnp
from jax import lax
from jax.experimental import pallas as pl
from jax.experimental.pallas import tpu as pltpu

F32 = jnp.float32
BF16 = jnp.bfloat16
MESH = pl.DeviceIdType.MESH

V7X_LANES = 128
VMEM_LIMIT = 56 * 1024 * 1024
VMEM_TILE_BUDGET = 40 * 1024 * 1024

ROPE_DIM = 64
ROPE_THETA = 10000.0
EPS = 1e-6
ROW_ALIGN = 384
ATT_BLK = 384
ROPE_ROWS = 128

ADAM_LR = 0.001
ADAM_B1 = 0.9
ADAM_B2 = 0.999
ADAM_EPS = 1e-08
ADAM_WD = 0.01
ADAM_STEP = 10

NEG = -1e30
SB_DEAD = -104.0
NT_DIMS = (((1,), (1,)), ((), ()))
TN_DIMS = (((0,), (0,)), ((), ()))
NN_DIMS = (((1,), (0,)), ((), ()))


def _cparams(*sem):
    return pltpu.CompilerParams(dimension_semantics=sem, vmem_limit_bytes=VMEM_LIMIT)


def _divisor_tile(n, cap, align):
    best = None
    t = align
    while t <= min(n, cap):
        if n % t == 0:
            best = t
        t += align
    return best if best is not None else n


def _mm_tiles(M, N, K, a_bytes, b_bytes, o_bytes, has_res):
    best = None
    for tm in sorted({_divisor_tile(M, c, 128) for c in (1408, 1024, 704, 512, 384, 256, 128)}, reverse=True):
        for tn in sorted({_divisor_tile(N, c, 128) for c in (1024, 512, 256, 128)}, reverse=True):
            for tk in sorted({_divisor_tile(K, c, 128) for c in (4096, 2048, 1408, 1024, 704, 512, 384, 256, 128)},
                             reverse=True):
                need = 2 * (tm * tk * a_bytes + tk * tn * b_bytes + tm * tn * o_bytes)
                need += 2 * tm * tn * 4 if has_res else 0
                need += tm * tn * 4 if tk != K else 0
                need += tm * tk * 2 if a_bytes != 2 else 0
                need += tk * tn * 2 if b_bytes != 2 else 0
                need += tm * tn * 4
                if need > VMEM_TILE_BUDGET:
                    continue
                score = (tm * tn / (tm + tn), tk)
                if best is None or score > best[0]:
                    best = (score, (tm, tn, tk))
    assert best is not None, (M, N, K)
    return best[1]


def _matmul(a, b, *, mode, out_dtype, name, residual=None):
    if mode == "nn":
        (M, K), N = a.shape, b.shape[1]
    elif mode == "nt":
        (M, K), N = a.shape, b.shape[0]
    else:
        (K, M), N = a.shape, b.shape[1]
    tm, tn, tk = _mm_tiles(M, N, K, a.dtype.itemsize, b.dtype.itemsize, jnp.dtype(out_dtype).itemsize,
                           residual is not None)
    nk = K // tk
    dims = {"nn": NN_DIMS, "nt": NT_DIMS, "tn": TN_DIMS}[mode]

    def body(*refs):
        if residual is not None:
            a_ref, b_ref, r_ref, o_ref = refs[:4]
        else:
            a_ref, b_ref, o_ref = refs[:3]
            r_ref = None
        part = lax.dot_general(a_ref[...].astype(BF16), b_ref[...].astype(BF16), dims,
                               preferred_element_type=F32)
        if nk == 1:
            if r_ref is not None:
                part = part + r_ref[...]
            o_ref[...] = part.astype(o_ref.dtype)
            return
        acc_ref = refs[-1]
        k = pl.program_id(2)

        @pl.when(k == 0)
        def _():
            acc_ref[...] = part

        @pl.when(k > 0)
        def _():
            acc_ref[...] += part

        @pl.when(k == nk - 1)
        def _():
            r = acc_ref[...]
            if r_ref is not None:
                r = r + r_ref[...]
            o_ref[...] = r.astype(o_ref.dtype)

    if mode == "tn":
        a_spec = pl.BlockSpec((tk, tm), lambda i, j, k: (k, i))
    else:
        a_spec = pl.BlockSpec((tm, tk), lambda i, j, k: (i, k))
    if mode == "nt":
        b_spec = pl.BlockSpec((tn, tk), lambda i, j, k: (j, k))
    else:
        b_spec = pl.BlockSpec((tk, tn), lambda i, j, k: (k, j))
    o_spec = pl.BlockSpec((tm, tn), lambda i, j, k: (i, j))
    in_specs = [a_spec, b_spec]
    args = [a, b]
    if residual is not None:
        in_specs.append(o_spec)
        args.append(residual)
    return pl.pallas_call(
        body, name=name,
        out_shape=jax.ShapeDtypeStruct((M, N), out_dtype),
        grid=(M // tm, N // tn, nk),
        in_specs=in_specs, out_specs=o_spec,
        scratch_shapes=[pltpu.VMEM((tm, tn), F32)] if nk > 1 else [],
        compiler_params=_cparams("parallel", "parallel", "arbitrary"),
    )(*args)


def _row_tile(rows, width, n_arrays):
    cap = max(16, VMEM_TILE_BUDGET // (2 * n_arrays * width * 4))
    return _divisor_tile(rows, min(cap, 384), 16)


def _rms_fwd(x, g, *, col0, out_dtype, name):
    T = x.shape[0]
    W = g.shape[-1]
    assert col0 % W == 0
    cb = col0 // W
    tm = _row_tile(T, W, 3)

    def body(x_ref, g_ref, o_ref):
        xv = x_ref[...].astype(F32)
        r = lax.rsqrt(jnp.mean(xv * xv, axis=-1, keepdims=True) + EPS)
        o_ref[...] = ((xv * r) * g_ref[...]).astype(o_ref.dtype)

    return pl.pallas_call(
        body, name=name,
        out_shape=jax.ShapeDtypeStruct((T, W), out_dtype),
        grid=(T // tm,),
        in_specs=[pl.BlockSpec((tm, W), lambda i: (i, cb)), pl.BlockSpec((1, W), lambda i: (0, 0))],
        out_specs=pl.BlockSpec((tm, W), lambda i: (i, 0)),
        compiler_params=_cparams("parallel"),
    )(x, g.reshape(1, W))


def _rms_bwd(x, g, dy, *, col0, out_dtype, name, residual=None):
    T = x.shape[0]
    W = g.shape[-1]
    assert col0 % W == 0
    cb = col0 // W
    tm = _row_tile(T, W, 6)
    nt = T // tm

    def body(*refs):
        if residual is not None:
            x_ref, g_ref, dy_ref, r_ref, dx_ref, dg_ref, acc_ref = refs
        else:
            x_ref, g_ref, dy_ref, dx_ref, dg_ref, acc_ref = refs
            r_ref = None
        i = pl.program_id(0)
        xv = x_ref[...].astype(F32)
        r = lax.rsqrt(jnp.mean(xv * xv, axis=-1, keepdims=True) + EPS)
        xh = xv * r
        dyv = dy_ref[...].astype(F32)
        dxh = dyv * g_ref[...]
        dx = r * (dxh - xh * jnp.mean(dxh * xh, axis=-1, keepdims=True))
        if r_ref is not None:
            dx = dx + r_ref[...]
        dx_ref[...] = dx.astype(dx_ref.dtype)
        part = jnp.sum((dyv * xh).reshape(tm // 8, 8, W), axis=0)

        @pl.when(i == 0)
        def _():
            acc_ref[...] = part

        @pl.when(i > 0)
        def _():
            acc_ref[...] += part

        @pl.when(i == nt - 1)
        def _():
            dg_ref[...] = jnp.sum(acc_ref[...], axis=0, keepdims=True)

    row = pl.BlockSpec((tm, W), lambda i: (i, 0))
    in_specs = [pl.BlockSpec((tm, W), lambda i: (i, cb)), pl.BlockSpec((1, W), lambda i: (0, 0)), row]
    args = [x, g.reshape(1, W), dy]
    if residual is not None:
        in_specs.append(row)
        args.append(residual)
    return pl.pallas_call(
        body, name=name,
        out_shape=(jax.ShapeDtypeStruct((T, W), out_dtype), jax.ShapeDtypeStruct((1, W), F32)),
        grid=(nt,),
        in_specs=in_specs,
        out_specs=(row, pl.BlockSpec((1, W), lambda i: (0, 0))),
        scratch_shapes=[pltpu.VMEM((8, W), F32)],
        compiler_params=_cparams("arbitrary"),
    )(*args)


def _gate_fwd(o, proj, g, *, zcol0, name):
    T, W = o.shape
    assert zcol0 % W == 0
    zb = zcol0 // W
    tm = _row_tile(T, W, 4)

    def body(o_ref, z_ref, g_ref, y_ref):
        ov = o_ref[...]
        r = lax.rsqrt(jnp.mean(ov * ov, axis=-1, keepdims=True) + EPS)
        z = z_ref[...]
        sg = 1.0 / (1.0 + jnp.exp(-z))
        y_ref[...] = (((ov * r) * g_ref[...]) * (z * sg)).astype(y_ref.dtype)

    return pl.pallas_call(
        body, name=name,
        out_shape=jax.ShapeDtypeStruct((T, W), BF16),
        grid=(T // tm,),
        in_specs=[pl.BlockSpec((tm, W), lambda i: (i, 0)), pl.BlockSpec((tm, W), lambda i: (i, zb)),
                  pl.BlockSpec((1, W), lambda i: (0, 0))],
        out_specs=pl.BlockSpec((tm, W), lambda i: (i, 0)),
        compiler_params=_cparams("parallel"),
    )(o, proj, g.reshape(1, W))


def _gate_bwd(dy, o, proj, g, *, grp, zcol0, name):
    T, W = o.shape
    assert zcol0 % W == 0
    zb = zcol0 // W
    tm = _row_tile(T, W, 8)
    nt = T // tm

    def body(dy_ref, o_ref, z_ref, g_ref, do_ref, dz_ref, dg_ref, acc_ref):
        i = pl.program_id(0)
        ov = o_ref[...]
        r = lax.rsqrt(jnp.mean(ov * ov, axis=-1, keepdims=True) + EPS)
        xh = ov * r
        gv = g_ref[...]
        z = z_ref[...]
        sg = 1.0 / (1.0 + jnp.exp(-z))
        dyv = dy_ref[...]
        dn = dyv * (z * sg)
        dz_ref[...] = (dyv * (xh * gv) * (sg * (1.0 + z * (1.0 - sg)))).astype(dz_ref.dtype)
        dxh = dn * gv
        do_ref[...] = r * (dxh - xh * jnp.mean(dxh * xh, axis=-1, keepdims=True))
        part = jnp.sum((dn * xh).reshape(tm // 8, 8, W), axis=0)

        @pl.when(i == 0)
        def _():
            acc_ref[...] = part

        @pl.when(i > 0)
        def _():
            acc_ref[...] += part

        @pl.when(i == nt - 1)
        def _():
            dg_ref[...] = jnp.sum(acc_ref[...], axis=0, keepdims=True)

    row = pl.BlockSpec((tm, W), lambda i: (i, 0))
    return pl.pallas_call(
        body, name=name,
        out_shape=(jax.ShapeDtypeStruct((T, W), F32), jax.ShapeDtypeStruct((T, W), BF16),
                   jax.ShapeDtypeStruct((1, W), F32)),
        grid=(nt,),
        in_specs=[pl.BlockSpec((tm, W), lambda i: (i, grp)), row, pl.BlockSpec((tm, W), lambda i: (i, zb)),
                  pl.BlockSpec((1, W), lambda i: (0, 0))],
        out_specs=(row, row, pl.BlockSpec((1, W), lambda i: (0, 0))),
        scratch_shapes=[pltpu.VMEM((8, W), F32)],
        compiler_params=_cparams("arbitrary"),
    )(dy, o, proj, g.reshape(1, W))


def _rope_fwd(q, kv, proj, cosp, sinp, *, H, kr_col0, name):
    T = q.shape[0]
    HB = H * V7X_LANES
    tm = ROPE_ROWS
    krb = kr_col0 // V7X_LANES

    def body(q_ref, kv_ref, kra_ref, krb_ref, cos_ref, sin_ref, qc_ref, kc_ref, v_ref):
        cos = cos_ref[...]
        sin = sin_ref[...]
        kr = (kra_ref[...] * cos + krb_ref[...] * sin).astype(BF16)
        for h in range(H):
            lo, hi = h * 128, (h + 1) * 128
            qc_ref[:, 2 * lo:2 * lo + 128] = q_ref[:, lo:hi].astype(BF16)
            qc_ref[:, 2 * lo + 128:2 * hi] = (q_ref[:, HB + lo:HB + hi] * cos
                                              + q_ref[:, 2 * HB + lo:2 * HB + hi] * sin).astype(BF16)
            kc_ref[:, 2 * lo:2 * lo + 128] = kv_ref[:, 2 * lo:2 * lo + 128].astype(BF16)
            kc_ref[:, 2 * lo + 128:2 * hi] = kr
            v_ref[:, lo:hi] = kv_ref[:, 2 * lo + 128:2 * hi].astype(BF16)

    tab = pl.BlockSpec((tm, 128), lambda i: (i, 0))
    return pl.pallas_call(
        body, name=name,
        out_shape=(jax.ShapeDtypeStruct((T, 2 * HB), BF16), jax.ShapeDtypeStruct((T, 2 * HB), BF16),
                   jax.ShapeDtypeStruct((T, HB), BF16)),
        grid=(T // tm,),
        in_specs=[pl.BlockSpec((tm, 3 * HB), lambda i: (i, 0)), pl.BlockSpec((tm, 2 * HB), lambda i: (i, 0)),
                  pl.BlockSpec((tm, 128), lambda i: (i, krb)), pl.BlockSpec((tm, 128), lambda i: (i, krb + 1)),
                  tab, tab],
        out_specs=(pl.BlockSpec((tm, 2 * HB), lambda i: (i, 0)), pl.BlockSpec((tm, 2 * HB), lambda i: (i, 0)),
                   pl.BlockSpec((tm, HB), lambda i: (i, 0))),
        compiler_params=_cparams("parallel"),
    )(q, kv, proj, proj, cosp, sinp)


def _rope_bwd(dqc, dkc, dv, cosp, sinp, *, H, name):
    T = dqc.shape[0]
    HB = H * V7X_LANES
    tm = ROPE_ROWS

    def body(dqc_ref, dkc_ref, dv_ref, cos_ref, sin_ref, dq_ref, dkv_ref, dkr_ref):
        cos = cos_ref[...]
        sin = sin_ref[...]
        gk = jnp.zeros((tm, 128), F32)
        for h in range(H):
            lo, hi = h * 128, (h + 1) * 128
            dq_ref[:, lo:hi] = dqc_ref[:, 2 * lo:2 * lo + 128].astype(BF16)
            gq = dqc_ref[:, 2 * lo + 128:2 * hi]
            dq_ref[:, HB + lo:HB + hi] = (gq * cos).astype(BF16)
            dq_ref[:, 2 * HB + lo:2 * HB + hi] = (gq * sin).astype(BF16)
            dkv_ref[:, 2 * lo:2 * lo + 128] = dkc_ref[:, 2 * lo:2 * lo + 128].astype(BF16)
            dkv_ref[:, 2 * lo + 128:2 * hi] = dv_ref[:, lo:hi].astype(BF16)
            gk = gk + dkc_ref[:, 2 * lo + 128:2 * hi]
        dkr_ref[:, 0:128] = (gk * cos).astype(BF16)
        dkr_ref[:, 128:256] = (gk * sin).astype(BF16)

    tab = pl.BlockSpec((tm, 128), lambda i: (i, 0))
    return pl.pallas_call(
        body, name=name,
        out_shape=(jax.ShapeDtypeStruct((T, 3 * HB), BF16), jax.ShapeDtypeStruct((T, 2 * HB), BF16),
                   jax.ShapeDtypeStruct((T, 256), BF16)),
        grid=(T // tm,),
        in_specs=[pl.BlockSpec((tm, 2 * HB), lambda i: (i, 0)), pl.BlockSpec((tm, 2 * HB), lambda i: (i, 0)),
                  pl.BlockSpec((tm, HB), lambda i: (i, 0)), tab, tab],
        out_specs=(pl.BlockSpec((tm, 3 * HB), lambda i: (i, 0)), pl.BlockSpec((tm, 2 * HB), lambda i: (i, 0)),
                   pl.BlockSpec((tm, 256), lambda i: (i, 0))),
        compiler_params=_cparams("parallel"),
    )(dqc, dkc, dv, cosp, sinp)


def _mla_fwd(qc, kc, v, *, H, scale, name):
    T = qc.shape[0]
    tq = tk = ATT_BLK
    nq = T // tq

    def body(q_ref, k_ref, v_ref, o_ref, lse_ref):
        i = pl.program_id(1)
        q = q_ref[...]
        row = i * tq + lax.broadcasted_iota(jnp.int32, (tq, tk), 0)
        col = lax.broadcasted_iota(jnp.int32, (tq, tk), 1)

        def step(j, carry, masked):
            m, l, acc = carry
            off = pl.multiple_of(j * tk, tk)
            ks = k_ref[pl.ds(off, tk), :]
            vs = v_ref[pl.ds(off, tk), :]
            s = lax.dot_general(q, ks, NT_DIMS, preferred_element_type=F32) * scale
            if masked:
                s = jnp.where(col + j * tk <= row, s, NEG)
            m_new = jnp.maximum(m, jnp.max(s, axis=1, keepdims=True))
            alpha = jnp.exp(m - m_new)
            p = jnp.exp(s - m_new)
            l = alpha * l + jnp.sum(p, axis=1, keepdims=True)
            acc = alpha * acc + jnp.dot(p.astype(BF16), vs, preferred_element_type=F32)
            return m_new, l, acc

        n_full = (i * tq) // tk
        n_tot = ((i + 1) * tq + tk - 1) // tk
        carry = (jnp.full((tq, 1), NEG, F32), jnp.zeros((tq, 1), F32), jnp.zeros((tq, 128), F32))
        carry = lax.fori_loop(0, n_full, functools.partial(step, masked=False), carry)
        m, l, acc = lax.fori_loop(n_full, n_tot, functools.partial(step, masked=True), carry)
        o_ref[...] = acc / l
        lse_ref[0] = m + jnp.log(l)

    return pl.pallas_call(
        body, name=name,
        out_shape=(jax.ShapeDtypeStruct((T, H * 128), F32), jax.ShapeDtypeStruct((H, T, 1), F32)),
        grid=(H, nq),
        in_specs=[pl.BlockSpec((tq, 256), lambda h, i: (i, h)), pl.BlockSpec((T, 256), lambda h, i: (0, h)),
                  pl.BlockSpec((T, 128), lambda h, i: (0, h))],
        out_specs=(pl.BlockSpec((tq, 128), lambda h, i: (i, h)), pl.BlockSpec((1, tq, 1), lambda h, i: (h, i, 0))),
        compiler_params=_cparams("parallel", "parallel"),
    )(qc, kc, v)


def _mla_bwd(qc, kc, v, o, do, lse, *, H, scale, name):
    T = qc.shape[0]
    tq = tk = ATT_BLK
    nq, nk = T // tq, T // tk

    def body(q_ref, k_ref, v_ref, o_ref, do_ref, lse_ref, dq_ref, dk_ref, dv_ref, delta_ref):
        dq_ref[...] = jnp.zeros_like(dq_ref)

        def fill_delta(i, c):
            off = pl.multiple_of(i * tq, tq)
            delta_ref[pl.ds(off, tq), :] = jnp.sum(do_ref[pl.ds(off, tq), :] * o_ref[pl.ds(off, tq), :],
                                                   axis=1, keepdims=True)
            return c

        lax.fori_loop(0, nq, fill_delta, 0)
        rowi = lax.broadcasted_iota(jnp.int32, (tq, tk), 0)
        coli = lax.broadcasted_iota(jnp.int32, (tq, tk), 1)

        def kblock(j, c):
            koff = pl.multiple_of(j * tk, tk)
            ks = k_ref[pl.ds(koff, tk), :]
            vs = v_ref[pl.ds(koff, tk), :]

            def qstep(i, carry, masked):
                dk, dv = carry
                qoff = pl.multiple_of(i * tq, tq)
                qs = q_ref[pl.ds(qoff, tq), :]
                dob = do_ref[pl.ds(qoff, tq), :].astype(BF16)
                s = lax.dot_general(qs, ks, NT_DIMS, preferred_element_type=F32) * scale
                if masked:
                    s = jnp.where(coli + j * tk <= rowi + i * tq, s, NEG)
                p = jnp.exp(s - lse_ref[0, pl.ds(qoff, tq), :])
                dv = dv + lax.dot_general(p.astype(BF16), dob, TN_DIMS, preferred_element_type=F32)
                dp = lax.dot_general(dob, vs, NT_DIMS, preferred_element_type=F32)
                ds = (p * (dp - delta_ref[pl.ds(qoff, tq), :]) * scale).astype(BF16)
                dk = dk + lax.dot_general(ds, qs, TN_DIMS, preferred_element_type=F32)
                dq_ref[pl.ds(qoff, tq), :] += jnp.dot(ds, ks, preferred_element_type=F32)
                return dk, dv

            i0 = (j * tk) // tq
            i1 = jnp.minimum(((j + 1) * tk + tq - 1) // tq, nq)
            carry = (jnp.zeros((tk, 256), F32), jnp.zeros((tk, 128), F32))
            carry = lax.fori_loop(i0, i1, functools.partial(qstep, masked=True), carry)
            dk, dv = lax.fori_loop(i1, nq, functools.partial(qstep, masked=False), carry)
            dk_ref[pl.ds(koff, tk), :] = dk
            dv_ref[pl.ds(koff, tk), :] = dv
            return c

        lax.fori_loop(0, nk, kblock, 0)

    wide = pl.BlockSpec((T, 256), lambda h: (0, h))
    narrow = pl.BlockSpec((T, 128), lambda h: (0, h))
    return pl.pallas_call(
        body, name=name,
        out_shape=(jax.ShapeDtypeStruct((T, H * 256), F32), jax.ShapeDtypeStruct((T, H * 256), F32),
                   jax.ShapeDtypeStruct((T, H * 128), F32)),
        grid=(H,),
        in_specs=[wide, wide, narrow, narrow, narrow, pl.BlockSpec((1, T, 1), lambda h: (h, 0, 0))],
        out_specs=(wide, wide, narrow),
        scratch_shapes=[pltpu.VMEM((T, 1), F32)],
        compiler_params=_cparams("parallel"),
    )(qc, kc, v, o, do, lse)


def _log_sigmoid_pair(z):
    e = jnp.exp(-jnp.abs(z))
    lb = jnp.minimum(z, 0.0) - jnp.log(1.0 + e)
    inv = 1.0 / (1.0 + e)
    sg = jnp.where(z >= 0.0, inv, e * inv)
    return lb, lb - z, sg


def _tri_dot(x, tri):
    hi = x.astype(BF16)
    lo = (x - hi.astype(F32)).astype(BF16)
    return jnp.dot(hi, tri, preferred_element_type=F32) + jnp.dot(lo, tri, preferred_element_type=F32)


def _sb_fwd(proj, *, H, qcol0, kcol0, vcol0, scale, name):
    T = proj.shape[0]
    tq = tk = ATT_BLK
    nq = T // tq
    qb, kb, vb = qcol0 // 128, kcol0 // 128, vcol0 // 128

    assert T // tk <= V7X_LANES

    def body(q_ref, k_ref, v_ref, y_ref, rems_ref):
        i = pl.program_id(1)
        q = q_ref[...].astype(BF16)
        row = i * tq + lax.broadcasted_iota(jnp.int32, (tq, tk), 0)
        col = lax.broadcasted_iota(jnp.int32, (tq, tk), 1)
        r_i = lax.broadcasted_iota(jnp.int32, (tk, tk), 0)
        c_i = lax.broadcasted_iota(jnp.int32, (tk, tk), 1)
        tri_after = (r_i > c_i).astype(BF16)
        lane = lax.broadcasted_iota(jnp.int32, (tq, V7X_LANES), 1)

        def step(j, carry, masked):
            rem, acc, tab = carry
            off = pl.multiple_of(j * tk, tk)
            ks = k_ref[pl.ds(off, tk), :].astype(BF16)
            vs = v_ref[pl.ds(off, tk), :].astype(BF16)
            z = lax.dot_general(q, ks, NT_DIMS, preferred_element_type=F32) * scale
            lb, lom, _ = _log_sigmoid_pair(z)
            if masked:
                valid = col + j * tk < row
                lom = jnp.where(valid, lom, 0.0)
            a = jnp.exp(lb + _tri_dot(lom, tri_after) + rem)
            if masked:
                a = jnp.where(valid, a, 0.0)
            acc = acc + jnp.dot(a.astype(BF16), vs, preferred_element_type=F32)
            rem = rem + jnp.sum(lom, axis=1, keepdims=True)
            return rem, acc, jnp.where(lane == j, rem, tab)

        n_full = (i * tq) // tk
        n_tot = ((i + 1) * tq + tk - 1) // tk
        carry = (jnp.zeros((tq, 1), F32), jnp.zeros((tq, 128), F32), jnp.full((tq, V7X_LANES), NEG, F32))
        carry = lax.fori_loop(0, n_tot - n_full, lambda idx, c: step(n_tot - 1 - idx, c, True), carry)

        def alive(rem):
            return (jnp.max(rem) >= SB_DEAD).astype(jnp.int32)

        def more(state):
            idx, live, _ = state
            return (idx < n_full) & (live > 0)

        def back(state):
            idx, _, c = state
            c = step(n_full - 1 - idx, c, False)
            return idx + 1, alive(c[0]), c

        _, _, (rem, acc, tab) = lax.while_loop(more, back, (jnp.int32(0), alive(carry[0]), carry))
        y_ref[...] = acc
        rems_ref[0] = tab

    return pl.pallas_call(
        body, name=name,
        out_shape=(jax.ShapeDtypeStruct((T, H * 128), F32), jax.ShapeDtypeStruct((H, T, V7X_LANES), F32)),
        grid=(H, nq),
        in_specs=[pl.BlockSpec((tq, 128), lambda h, i: (i, qb + h)), pl.BlockSpec((T, 128), lambda h, i: (0, kb + h)),
                  pl.BlockSpec((T, 128), lambda h, i: (0, vb + h))],
        out_specs=(pl.BlockSpec((tq, 128), lambda h, i: (i, h)),
                   pl.BlockSpec((1, tq, V7X_LANES), lambda h, i: (h, i, 0))),
        compiler_params=_cparams("parallel", "parallel"),
    )(proj, proj, proj)


def _sb_bwd(proj, dy, rems, *, H, qcol0, kcol0, vcol0, scale, name):
    T = proj.shape[0]
    tq = tk = ATT_BLK
    nq = T // tq
    qb, kb, vb = qcol0 // 128, kcol0 // 128, vcol0 // 128

    def body(q_ref, k_ref, v_ref, dy_ref, rems_ref, dq_ref, dk_ref, dv_ref, dk_acc, dv_acc):
        dk_acc[...] = jnp.zeros_like(dk_acc)
        dv_acc[...] = jnp.zeros_like(dv_acc)
        rowi = lax.broadcasted_iota(jnp.int32, (tq, tk), 0)
        coli = lax.broadcasted_iota(jnp.int32, (tq, tk), 1)
        r_i = lax.broadcasted_iota(jnp.int32, (tk, tk), 0)
        c_i = lax.broadcasted_iota(jnp.int32, (tk, tk), 1)
        tri_upto = (r_i <= c_i).astype(BF16)
        tri_before = (r_i < c_i).astype(BF16)
        lane = lax.broadcasted_iota(jnp.int32, (tq, V7X_LANES), 1)
        lane1 = lax.broadcasted_iota(jnp.int32, (1, V7X_LANES), 1)

        def qblock(i, c):
            qoff = pl.multiple_of(i * tq, tq)
            qs = q_ref[pl.ds(qoff, tq), :].astype(BF16)
            dyb = dy_ref[pl.ds(qoff, tq), :].astype(BF16)
            tab = rems_ref[0, pl.ds(qoff, tq), :]

            def step(j, carry, masked):
                pre, dq = carry
                rem = jnp.sum(jnp.where(lane == j, tab, 0.0), axis=1, keepdims=True)
                koff = pl.multiple_of(j * tk, tk)
                ks = k_ref[pl.ds(koff, tk), :].astype(BF16)
                vs = v_ref[pl.ds(koff, tk), :].astype(BF16)
                z = lax.dot_general(qs, ks, NT_DIMS, preferred_element_type=F32) * scale
                lb, lom, sg = _log_sigmoid_pair(z)
                if masked:
                    valid = coli + j * tk < rowi + i * tq
                    lom = jnp.where(valid, lom, 0.0)
                a = jnp.exp(lb + rem - _tri_dot(lom, tri_upto))
                if masked:
                    a = jnp.where(valid, a, 0.0)
                dv_acc[pl.ds(koff, tk), :] += lax.dot_general(a.astype(BF16), dyb, TN_DIMS,
                                                              preferred_element_type=F32)
                de = a * lax.dot_general(dyb, vs, NT_DIMS, preferred_element_type=F32)
                before = pre + _tri_dot(de, tri_before)
                dz = de * (1.0 - sg) - before * sg
                if masked:
                    dz = jnp.where(valid, dz, 0.0)
                dzb = (dz * scale).astype(BF16)
                dk_acc[pl.ds(koff, tk), :] += lax.dot_general(dzb, qs, TN_DIMS, preferred_element_type=F32)
                dq = dq + jnp.dot(dzb, ks, preferred_element_type=F32)
                return pre + jnp.sum(de, axis=1, keepdims=True), dq

            n_full = (i * tq) // tk
            n_tot = ((i + 1) * tq + tk - 1) // tk
            colmax = jnp.max(tab, axis=0, keepdims=True)
            dead = (lane1 >= 1) & (lane1 <= n_full) & (colmax < SB_DEAD)
            j0 = jnp.sum(dead.astype(jnp.int32))
            carry = (jnp.zeros((tq, 1), F32), jnp.zeros((tq, 128), F32))
            carry = lax.fori_loop(j0, n_full, functools.partial(step, masked=False), carry)
            _, dq = lax.fori_loop(n_full, n_tot, functools.partial(step, masked=True), carry)
            dq_ref[pl.ds(qoff, tq), :] = dq.astype(dq_ref.dtype)
            return c

        lax.fori_loop(0, nq, qblock, 0)
        dk_ref[...] = dk_acc[...].astype(dk_ref.dtype)
        dv_ref[...] = dv_acc[...].astype(dv_ref.dtype)

    def seg(b):
        return pl.BlockSpec((T, 128), lambda h: (0, b + h))

    out = pl.BlockSpec((T, 128), lambda h: (0, h))
    return pl.pallas_call(
        body, name=name,
        out_shape=tuple(jax.ShapeDtypeStruct((T, H * 128), BF16) for _ in range(3)),
        grid=(H,),
        in_specs=[seg(qb), seg(kb), seg(vb), out, pl.BlockSpec((1, T, V7X_LANES), lambda h: (h, 0, 0))],
        out_specs=(out, out, out),
        scratch_shapes=[pltpu.VMEM((T, 128), F32), pltpu.VMEM((T, 128), F32)],
        compiler_params=_cparams("parallel"),
    )(proj, proj, proj, dy, rems)


def _final_loss(h, g, target, *, row0, n_rows, name):
    T, D = h.shape
    tm = _row_tile(T, D, 6)
    nt = T // tm

    def body(h_ref, g_ref, t_ref, dh_ref, dg_ref, loss_ref, acc_ref, lacc_ref):
        i = pl.program_id(0)
        xv = h_ref[...]
        r = lax.rsqrt(jnp.mean(xv * xv, axis=-1, keepdims=True) + EPS)
        xh = xv * r
        gv = g_ref[...]
        rows = i * tm + lax.broadcasted_iota(jnp.int32, (tm, 1), 0)
        valid = (rows >= row0) & (rows < row0 + n_rows)
        err = jnp.where(valid, xh * gv - t_ref[...], 0.0)
        dout = err * (1.0 / D)
        dxh = dout * gv
        dh_ref[...] = r * (dxh - xh * jnp.mean(dxh * xh, axis=-1, keepdims=True))
        part = jnp.sum((dout * xh).reshape(tm // 8, 8, D), axis=0)
        lpart = jnp.sum((err * err).reshape(tm // 8, 8, D), axis=0)

        @pl.when(i == 0)
        def _():
            acc_ref[...] = part
            lacc_ref[...] = lpart

        @pl.when(i > 0)
        def _():
            acc_ref[...] += part
            lacc_ref[...] += lpart

        @pl.when(i == nt - 1)
        def _():
            dg_ref[...] = jnp.sum(acc_ref[...], axis=0, keepdims=True)
            loss_ref[...] = (0.5 / D) * jnp.sum(jnp.sum(lacc_ref[...], axis=0, keepdims=True), axis=1, keepdims=True)

    row = pl.BlockSpec((tm, D), lambda i: (i, 0))
    vec = pl.BlockSpec((1, D), lambda i: (0, 0))
    return pl.pallas_call(
        body, name=name,
        out_shape=(jax.ShapeDtypeStruct((T, D), F32), jax.ShapeDtypeStruct((1, D), F32),
                   jax.ShapeDtypeStruct((1, 1), F32)),
        grid=(nt,),
        in_specs=[row, vec, row],
        out_specs=(row, vec, pl.BlockSpec((1, 1), lambda i: (0, 0))),
        scratch_shapes=[pltpu.VMEM((8, D), F32), pltpu.VMEM((8, D), F32)],
        compiler_params=_cparams("arbitrary"),
    )(h, g.reshape(1, D), target)


def _elementwise(fn, args, out_dtypes, name):
    shape = args[0].shape
    C = shape[-1]
    R = math.prod(shape[:-1])
    n = len(args) + len(out_dtypes)
    cap = max(16, (VMEM_TILE_BUDGET // 2) // (2 * n * C * 4))
    tr = _divisor_tile(R, cap, 16)
    n_in = len(args)

    def body(*refs):
        outs = fn(*[r[...] for r in refs[:n_in]])
        for o_ref, val in zip(refs[n_in:], outs):
            o_ref[...] = val.astype(o_ref.dtype)

    spec = pl.BlockSpec((tr, C), lambda i: (i, 0))
    res = pl.pallas_call(
        body, name=name,
        out_shape=tuple(jax.ShapeDtypeStruct((R, C), dt) for dt in out_dtypes),
        grid=(R // tr,),
        in_specs=[spec] * n_in, out_specs=tuple([spec] * len(out_dtypes)),
        compiler_params=_cparams("parallel"),
    )(*[a.reshape(R, C) for a in args])
    return tuple(r.reshape(shape) for r in res)


def _adamw_math(w, g, m, v):
    m = ADAM_B1 * m + (1.0 - ADAM_B1) * g
    v = ADAM_B2 * v + (1.0 - ADAM_B2) * (g * g)
    m_hat = m / (1.0 - ADAM_B1 ** ADAM_STEP)
    v_hat = v / (1.0 - ADAM_B2 ** ADAM_STEP)
    delta = -ADAM_LR * (m_hat / (jnp.sqrt(v_hat) + ADAM_EPS) + ADAM_WD * w)
    return delta, m, v


def _adamw(w, g, m, v, name):
    return _elementwise(_adamw_math, [w, g, m, v], [F32, F32, F32], name)


ANY = pl.BlockSpec(memory_space=pl.ANY)


def _position():
    return lax.axis_index("x"), lax.axis_index("y"), lax.axis_index("c")


def _gather_shards(a, name):
    def body(a_ref, o_ref, send_sems, recv_sems):
        x, y, c = _position()
        p = 2 * x + y
        chips = [(1 - x, y), (x, 1 - y), (1 - x, 1 - y)]

        def copy(k, src, dst, to):
            return pltpu.make_async_remote_copy(src_ref=src, dst_ref=dst, send_sem=send_sems.at[k],
                                                recv_sem=recv_sems.at[k], device_id=to, device_id_type=MESH)

        first = [copy(k, a_ref.at[c], o_ref.at[p, c], (qx, qy, c)) for k, (qx, qy) in enumerate(chips)]
        for cp in first:
            cp.start()
        passed = []
        for k, (qx, qy) in enumerate(chips):
            land = o_ref.at[2 * qx + qy, c]
            copy(k, land, land, (x, y, c)).wait_recv()
            fwd = copy(3 + k, land, land, (x, y, 1 - c))
            fwd.start()
            passed.append(fwd)
        for k, (qx, qy) in enumerate(chips):
            land = o_ref.at[2 * qx + qy, 1 - c]
            copy(3 + k, land, land, (x, y, c)).wait_recv()
        for cp in first + passed:
            cp.wait_send()

    return pl.pallas_call(
        body, name=name,
        out_shape=jax.ShapeDtypeStruct((4,) + a.shape, a.dtype),
        in_specs=[ANY], out_specs=ANY,
        scratch_shapes=[pltpu.SemaphoreType.DMA((6,)), pltpu.SemaphoreType.DMA((6,))],
    )(a)


def _with_own(gathered, own):
    x, y, _ = _position()
    p = 2 * x + y
    return [jnp.where(p == q, own, gathered[q]) for q in range(4)]


def _swap_halves(g, name):
    def body(g_ref, o_ref, send_sem, recv_sem):
        x, y, c = _position()
        cp = pltpu.make_async_remote_copy(src_ref=g_ref.at[1 - c], dst_ref=o_ref, send_sem=send_sem,
                                          recv_sem=recv_sem, device_id=(x, y, 1 - c), device_id_type=MESH)
        cp.start()
        cp.wait()

    return pl.pallas_call(
        body, name=name,
        out_shape=jax.ShapeDtypeStruct(g.shape[1:], g.dtype),
        in_specs=[ANY], out_specs=ANY,
        scratch_shapes=[pltpu.SemaphoreType.DMA, pltpu.SemaphoreType.DMA],
    )(g)


def _scatter_to_chips(pb, name):
    def body(p_ref, o_ref, send_sems, recv_sems):
        x, y, c = _position()
        chips = [(1 - x, y), (x, 1 - y), (1 - x, 1 - y)]
        cps = [pltpu.make_async_remote_copy(src_ref=p_ref.at[2 * qx + qy], dst_ref=o_ref.at[k],
                                            send_sem=send_sems.at[k], recv_sem=recv_sems.at[k],
                                            device_id=(qx, qy, c), device_id_type=MESH)
               for k, (qx, qy) in enumerate(chips)]
        for cp in cps:
            cp.start()
        for cp in cps:
            cp.wait()

    return pl.pallas_call(
        body, name=name,
        out_shape=jax.ShapeDtypeStruct((3,) + pb.shape[1:], pb.dtype),
        in_specs=[ANY], out_specs=ANY,
        scratch_shapes=[pltpu.SemaphoreType.DMA((3,)), pltpu.SemaphoreType.DMA((3,))],
    )(pb)


def _join_halves(r, name):
    def body(r_ref, o_ref, send_sem, recv_sem):
        x, y, c = _position()
        cp = pltpu.make_async_remote_copy(src_ref=r_ref, dst_ref=o_ref, send_sem=send_sem,
                                          recv_sem=recv_sem, device_id=(x, y, 1 - c), device_id_type=MESH)
        cp.start()
        cp.wait()

    other = pl.pallas_call(
        body, name=name,
        out_shape=jax.ShapeDtypeStruct(r.shape, r.dtype),
        in_specs=[ANY], out_specs=ANY,
        scratch_shapes=[pltpu.SemaphoreType.DMA, pltpu.SemaphoreType.DMA],
    )(r)
    c = lax.axis_index("c")
    return jnp.stack([jnp.where(c == 0, r, other), jnp.where(c == 0, other, r)], axis=0)


def _reduce_to_shard(gh, tag):
    x, y, c = _position()
    p = 2 * x + y
    sib = _swap_halves(gh, f"swap_{tag}")
    mine = lax.dynamic_index_in_dim(gh, c, 0, keepdims=False)
    psum, pb = _elementwise(lambda a, b: (a + b, a + b), [mine, sib], [F32, BF16], f"pairsum_{tag}")
    got = _scatter_to_chips(pb, f"scatter_{tag}")
    own = lax.dynamic_index_in_dim(psum, p, 0, keepdims=False)
    (red,) = _elementwise(lambda o, a, b, d: (((o + a.astype(F32)) + b.astype(F32)) + d.astype(F32),),
                          [own, got[0], got[1], got[2]], [F32], f"chipsum_{tag}")
    return _join_halves(red, f"join_{tag}")


def _all_reduce_small(vec, name):
    R = vec.shape[0]

    def body(v_ref, o_ref, land_ref, send_sems, recv_sems):
        x, y, c = _position()
        me = 4 * x + 2 * y + c
        land_ref[me] = v_ref[...]
        cps = []
        for r in range(1, 8):
            rx, ry, rc = (r >> 2) & 1, (r >> 1) & 1, r & 1
            to = (x ^ rx, y ^ ry, c ^ rc)
            cps.append(pltpu.make_async_remote_copy(src_ref=v_ref, dst_ref=land_ref.at[me],
                                                    send_sem=send_sems.at[r - 1], recv_sem=recv_sems.at[r - 1],
                                                    device_id=to, device_id_type=MESH))
        for cp in cps:
            cp.start()
        for cp in cps:
            cp.wait()
        total = land_ref[0]
        for d in range(1, 8):
            total = total + land_ref[d]
        o_ref[...] = total

    return pl.pallas_call(
        body, name=name,
        out_shape=jax.ShapeDtypeStruct((R, 128), F32),
        in_specs=[pl.BlockSpec(memory_space=pltpu.VMEM)], out_specs=pl.BlockSpec(memory_space=pltpu.VMEM),
        scratch_shapes=[pltpu.VMEM((8, R, 128), F32), pltpu.SemaphoreType.DMA((7,)), pltpu.SemaphoreType.DMA((7,))],
    )(vec)


def _rot(w):
    half = ROPE_DIM // 2
    return jnp.concatenate([-w[..., half:], w[..., :half]], axis=-1)


def _unrot(g):
    half = ROPE_DIM // 2
    return jnp.concatenate([g[..., half:], -g[..., :half]], axis=-1)


class _Layout:
    def __init__(self, D, QL, KVL):
        self.D, self.QL, self.KVL = D, QL, KVL
        self.H = D // 256
        self.WG = self.H * 128
        WG = self.WG
        self.z_mla, self.q_sb, self.k_sb, self.v_sb, self.z_sb = 0, WG, 2 * WG, 3 * WG, 4 * WG
        self.c_q = 5 * WG
        self.c_kv = self.c_q + QL
        self.k_r = self.c_kv + KVL
        self.width = -(-(self.k_r + 256) // 512) * 512
        self.orig = (QL, KVL, ROPE_DIM, WG, WG, WG, WG, WG)

    def pack_w_in(self, w):
        cuts = []
        o = 0
        for s in self.orig:
            cuts.append(w[:, o:o + s])
            o += s
        c_q, c_kv, k_r, z_mla, q_sb, k_sb, v_sb, z_sb = cuts
        z64 = jnp.zeros((w.shape[0], 128 - ROPE_DIM), w.dtype)
        pad = jnp.zeros((w.shape[0], self.width - self.k_r - 256), w.dtype)
        return jnp.concatenate([z_mla, q_sb, k_sb, v_sb, z_sb, c_q, c_kv, k_r, z64, _rot(k_r), z64, pad], axis=1)

    def unpack_dw_in(self, g):
        WG = self.WG
        k_r = g[:, self.k_r:self.k_r + ROPE_DIM] + _unrot(g[:, self.k_r + 128:self.k_r + 128 + ROPE_DIM])
        return jnp.concatenate([g[:, self.c_q:self.c_q + self.QL], g[:, self.c_kv:self.c_kv + self.KVL], k_r,
                                g[:, 0:WG], g[:, WG:2 * WG], g[:, 2 * WG:3 * WG], g[:, 3 * WG:4 * WG],
                                g[:, 4 * WG:5 * WG]], axis=1)

    def pack_w_uq(self, w):
        H = self.H
        w3 = w.reshape(w.shape[0], H, 128 + ROPE_DIM)
        nope = w3[:, :, :128]
        r = w3[:, :, 128:]
        z = jnp.zeros(r.shape, w.dtype)
        a = jnp.concatenate([r, z], axis=-1)
        b = jnp.concatenate([_rot(r), z], axis=-1)
        return jnp.concatenate([nope.reshape(-1, H * 128), a.reshape(-1, H * 128), b.reshape(-1, H * 128)], axis=1)

    def unpack_dw_uq(self, g):
        H = self.H
        HB = H * 128
        nope = g[:, :HB].reshape(-1, H, 128)
        a = g[:, HB:2 * HB].reshape(-1, H, 128)[:, :, :ROPE_DIM]
        b = g[:, 2 * HB:].reshape(-1, H, 128)[:, :, :ROPE_DIM]
        return jnp.concatenate([nope, a + _unrot(b)], axis=-1).reshape(-1, H * (128 + ROPE_DIM))


def _rope_tables(T):
    inv_freq = ROPE_THETA ** (-jnp.arange(0, ROPE_DIM, 2, dtype=F32) / ROPE_DIM)
    ang = jnp.arange(T, dtype=jnp.int32).astype(F32)[:, None] * inv_freq[None, :]
    z = jnp.zeros((T, 128 - ROPE_DIM), F32)
    cos, sin = jnp.cos(ang), jnp.sin(ang)
    return jnp.concatenate([cos, cos, z], axis=1), jnp.concatenate([sin, sin, z], axis=1)


def _layer_fwd(h, wl, lay, tabs):
    g_norm, w_in, g_q, g_kv, w_uq, w_ukv, g_mla, g_sb, w_o = wl
    cosp, sinp = tabs
    H = lay.H
    u = _rms_fwd(h, g_norm, col0=0, out_dtype=BF16, name="rms_h")
    proj = _matmul(u, w_in, mode="nn", out_dtype=F32, name="mm_in")
    cqn = _rms_fwd(proj, g_q, col0=lay.c_q, out_dtype=BF16, name="rms_cq")
    ckvn = _rms_fwd(proj, g_kv, col0=lay.c_kv, out_dtype=BF16, name="rms_ckv")
    q = _matmul(cqn, w_uq, mode="nn", out_dtype=F32, name="mm_uq")
    kv = _matmul(ckvn, w_ukv, mode="nn", out_dtype=F32, name="mm_ukv")
    qc, kc, v = _rope_fwd(q, kv, proj, cosp, sinp, H=H, kr_col0=lay.k_r, name="rope_fwd")
    o_mla, lse = _mla_fwd(qc, kc, v, H=H, scale=1.0 / math.sqrt(128 + ROPE_DIM), name="mla_fwd")
    o_sb, rems = _sb_fwd(proj, H=H, qcol0=lay.q_sb, kcol0=lay.k_sb, vcol0=lay.v_sb,
                        scale=1.0 / math.sqrt(128), name="sb_fwd")
    y_mla = _gate_fwd(o_mla, proj, g_mla, zcol0=lay.z_mla, name="gate_fwd_mla")
    y_sb = _gate_fwd(o_sb, proj, g_sb, zcol0=lay.z_sb, name="gate_fwd_sb")
    y = jnp.concatenate([y_mla, y_sb], axis=1)
    h_out = _matmul(y, w_o, mode="nn", out_dtype=F32, name="mm_o", residual=h)
    saved = (h, u, proj, cqn, ckvn, qc, kc, v, o_mla, lse, o_sb, rems, y)
    return h_out, saved


def _layer_bwd(dh, saved, wl, lay, tabs):
    g_norm, w_in, g_q, g_kv, w_uq, w_ukv, g_mla, g_sb, w_o = wl
    h, u, proj, cqn, ckvn, qc, kc, v, o_mla, lse, o_sb, rems, y = saved
    cosp, sinp = tabs
    H = lay.H
    dy = _matmul(dh, w_o, mode="nt", out_dtype=F32, name="mm_o_dx")
    d_w_o = _matmul(y, dh, mode="tn", out_dtype=F32, name="mm_o_dw")
    do_mla, dz_mla, dg_mla = _gate_bwd(dy, o_mla, proj, g_mla, grp=0, zcol0=lay.z_mla, name="gate_bwd_mla")
    do_sb, dz_sb, dg_sb = _gate_bwd(dy, o_sb, proj, g_sb, grp=1, zcol0=lay.z_sb, name="gate_bwd_sb")
    dq_sb, dk_sb, dv_sb = _sb_bwd(proj, do_sb, rems, H=H, qcol0=lay.q_sb, kcol0=lay.k_sb, vcol0=lay.v_sb,
                                  scale=1.0 / math.sqrt(128), name="sb_bwd")
    dqc, dkc, dv = _mla_bwd(qc, kc, v, o_mla, do_mla, lse, H=H, scale=1.0 / math.sqrt(128 + ROPE_DIM),
                            name="mla_bwd")
    dq, dkv, dkr = _rope_bwd(dqc, dkc, dv, cosp, sinp, H=H, name="rope_bwd")
    d_w_uq = _matmul(cqn, dq, mode="tn", out_dtype=F32, name="mm_uq_dw")
    dcqn = _matmul(dq, w_uq, mode="nt", out_dtype=F32, name="mm_uq_dx")
    d_w_ukv = _matmul(ckvn, dkv, mode="tn", out_dtype=F32, name="mm_ukv_dw")
    dckvn = _matmul(dkv, w_ukv, mode="nt", out_dtype=F32, name="mm_ukv_dx")
    dcq, dg_q = _rms_bwd(proj, g_q, dcqn, col0=lay.c_q, out_dtype=BF16, name="rms_cq_bwd")
    dckv, dg_kv = _rms_bwd(proj, g_kv, dckvn, col0=lay.c_kv, out_dtype=BF16, name="rms_ckv_bwd")
    pad = jnp.zeros((dh.shape[0], lay.width - lay.k_r - 256), BF16)
    dproj = jnp.concatenate([dz_mla, dq_sb, dk_sb, dv_sb, dz_sb, dcq, dckv, dkr, pad], axis=1)
    d_w_in = _matmul(u, dproj, mode="tn", out_dtype=F32, name="mm_in_dw")
    du = _matmul(dproj, w_in, mode="nt", out_dtype=F32, name="mm_in_dx")
    dh_prev, dg_norm = _rms_bwd(h, g_norm, du, col0=0, out_dtype=F32, name="rms_h_bwd", residual=dh)
    grads = (dg_norm[0], lay.unpack_dw_in(d_w_in), dg_q[0], dg_kv[0], lay.unpack_dw_uq(d_w_uq), d_w_ukv,
             dg_mla[0], dg_sb[0], d_w_o)
    return dh_prev, grads


def _halves(a):
    return a.reshape((2, a.shape[0] // 2) + a.shape[1:])


def _gather_cols(w, name):
    L, K, n = w.shape
    own = w.astype(BF16)
    g = _with_own(_gather_shards(_halves(own), name).reshape(4, L, K, n), own)
    return [jnp.concatenate([g[q][l] for q in range(4)], axis=1) for l in range(L)]


def _cut_cols(g):
    L, K, N = g.shape
    return g.reshape(2, L // 2, K, 4, N // 4).transpose(0, 3, 1, 2, 4)


def kernel(x, meta_tokens, g_norm, w_in, g_q, g_kv, w_uq, w_ukv, g_out_mla, g_out_sb, w_o, g_final, loss_target, m_meta_tokens, m_g_norm, m_w_in, m_g_q, m_g_kv, m_w_uq, m_w_ukv, m_g_out_mla, m_g_out_sb, m_w_o, m_g_final, v_meta_tokens, v_g_norm, v_w_in, v_g_q, v_g_kv, v_w_uq, v_w_ukv, v_g_out_mla, v_g_out_sb, v_w_o, v_g_final):
    _, S, D = x.shape
    NM = meta_tokens.shape[0]
    L = g_norm.shape[0]
    lay = _Layout(D, g_q.shape[1], g_kv.shape[1])
    TP = -(-(NM + S) // ROW_ALIGN) * ROW_ALIGN
    tabs = _rope_tables(TP)

    w_in_full = [lay.pack_w_in(w) for w in _gather_cols(w_in, "gather_w_in")]
    w_uq_full = [lay.pack_w_uq(w) for w in _gather_cols(w_uq, "gather_w_uq")]
    w_ukv_full = _gather_cols(w_ukv, "gather_w_ukv")
    w_o_own = w_o.astype(BF16)
    w_o_g = _with_own(_gather_shards(_halves(w_o_own), "gather_w_o").reshape((4, L) + w_o.shape[1:]), w_o_own)
    w_o_full = [jnp.concatenate([w_o_g[q][l] for q in range(4)], axis=0) for l in range(L)]
    meta_g = _with_own(_gather_shards(meta_tokens.reshape(2, NM // 2, -1), "gather_meta").reshape(4, NM, -1),
                       meta_tokens)
    meta_full = jnp.concatenate(meta_g, axis=1)

    h = jnp.concatenate([meta_full, x[0], jnp.zeros((TP - NM - S, D), F32)], axis=0)
    target = jnp.pad(loss_target[0], ((NM, TP - NM - S), (0, 0)))
    weights = [(g_norm[l], w_in_full[l], g_q[l], g_kv[l], w_uq_full[l], w_ukv_full[l], g_out_mla[l],
                g_out_sb[l], w_o_full[l]) for l in range(L)]
    saved = []
    for l in range(L):
        h, s = _layer_fwd(h, weights[l], lay, tabs)
        saved.append(s)
    dh, dg_final, loss_part = _final_loss(h, g_final, target, row0=NM, n_rows=S, name="final_loss")

    layer_grads = [None] * L
    for l in reversed(range(L)):
        dh, layer_grads[l] = _layer_bwd(dh, saved[l], weights[l], lay, tabs)
    grad_x = dh[NM:NM + S][None]
    d_meta = dh[:NM]

    def stack(i):
        return jnp.stack([layer_grads[l][i] for l in range(L)], axis=0)

    small = [stack(0), stack(2), stack(3), stack(6), stack(7), dg_final[0]]
    flat = jnp.concatenate([s.reshape(-1) for s in small])
    n_flat = flat.shape[0]
    rows = -(-n_flat // (8 * 128)) * 8
    packed = jnp.pad(flat, (0, rows * 128 - n_flat)).reshape(rows, 128)
    summed = _all_reduce_small(packed, "allreduce_gains").reshape(-1)
    small_red = []
    o = 0
    for s in small:
        small_red.append(summed[o:o + s.size].reshape(s.shape))
        o += s.size
    g_g_norm, g_g_q, g_g_kv, g_g_mla, g_g_sb, g_g_final = small_red

    g_w_in = _reduce_to_shard(_cut_cols(stack(1)), "w_in").reshape(w_in.shape)
    g_w_uq = _reduce_to_shard(_cut_cols(stack(4)), "w_uq").reshape(w_uq.shape)
    g_w_ukv = _reduce_to_shard(_cut_cols(stack(5)), "w_ukv").reshape(w_ukv.shape)
    d_w_o = stack(8).reshape(2, L // 2, 4, w_o.shape[1], D).transpose(0, 2, 1, 3, 4)
    g_w_o = _reduce_to_shard(d_w_o, "w_o").reshape(w_o.shape)
    d_meta = d_meta.reshape(2, NM // 2, 4, D // 4).transpose(0, 2, 1, 3)
    g_meta = _reduce_to_shard(d_meta, "meta").reshape(meta_tokens.shape)

    loss = lax.psum(loss_part[0, 0], ("x", "y", "c"))

    names = ["meta", "g_norm", "w_in", "g_q", "g_kv", "w_uq", "w_ukv", "g_out_mla", "g_out_sb", "w_o", "g_final"]
    ws = [meta_tokens, g_norm, w_in, g_q, g_kv, w_uq, w_ukv, g_out_mla, g_out_sb, w_o, g_final]
    gs = [g_meta, g_g_norm, g_w_in, g_g_q, g_g_kv, g_w_uq, g_w_ukv, g_g_mla, g_g_sb, g_w_o, g_g_final]
    ms = [m_meta_tokens, m_g_norm, m_w_in, m_g_q, m_g_kv, m_w_uq, m_w_ukv, m_g_out_mla, m_g_out_sb, m_w_o, m_g_final]
    vs = [v_meta_tokens, v_g_norm, v_w_in, v_g_q, v_g_kv, v_w_uq, v_w_ukv, v_g_out_mla, v_g_out_sb, v_w_o, v_g_final]
    deltas, new_m, new_v = [], [], []
    for n, w, g, m, v in zip(names, ws, gs, ms, vs):
        shape = w.shape
        if w.ndim == 1:
            w, g, m, v = (a.reshape(1, -1) for a in (w, g, m, v))
        d, nm, nv = _adamw(w, g, m, v, f"adamw_{n}")
        deltas.append(d.reshape(shape))
        new_m.append(nm.reshape(shape))
        new_v.append(nv.reshape(shape))
    return (loss, grad_x, *gs, *deltas, *new_m, *new_v)
```

```python
import functools
import math

import jax
import jax.numpy as jnp
from jax import lax
from jax.experimental import pallas as pl
from jax.experimental.pallas import tpu as pltpu

F32 = jnp.float32
BF16 = jnp.bfloat16
MESH = pl.DeviceIdType.MESH

V7X_LANES = 128
VMEM_LIMIT = 56 * 1024 * 1024
VMEM_TILE_BUDGET = 40 * 1024 * 1024

ROPE_DIM = 64
ROPE_THETA = 10000.0
EPS = 1e-6
ROW_ALIGN = 384
ATT_BLK = 384
ROPE_ROWS = 128

ADAM_LR = 0.001
ADAM_B1 = 0.9
ADAM_B2 = 0.999
ADAM_EPS = 1e-08
ADAM_WD = 0.01
ADAM_STEP = 10

NEG = -1e30
SB_DEAD = -104.0
NT_DIMS = (((1,), (1,)), ((), ()))
TN_DIMS = (((0,), (0,)), ((), ()))
NN_DIMS = (((1,), (0,)), ((), ()))


def _cparams(*sem):
    return pltpu.CompilerParams(dimension_semantics=sem, vmem_limit_bytes=VMEM_LIMIT)


def _divisor_tile(n, cap, align):
    best = None
    t = align
    while t <= min(n, cap):
        if n % t == 0:
            best = t
        t += align
    return best if best is not None else n


def _mm_tiles(M, N, K, a_bytes, b_bytes, o_bytes, has_res):
    best = None
    for tm in sorted({_divisor_tile(M, c, 128) for c in (1408, 1024, 704, 512, 384, 256, 128)}, reverse=True):
        for tn in sorted({_divisor_tile(N, c, 128) for c in (1024, 512, 256, 128)}, reverse=True):
            for tk in sorted({_divisor_tile(K, c, 128) for c in (4096, 2048, 1408, 1024, 704, 512, 384, 256, 128)},
                             reverse=True):
                need = 2 * (tm * tk * a_bytes + tk * tn * b_bytes + tm * tn * o_bytes)
                need += 2 * tm * tn * 4 if has_res else 0
                need += tm * tn * 4 if tk != K else 0
                need += tm * tk * 2 if a_bytes != 2 else 0
                need += tk * tn * 2 if b_bytes != 2 else 0
                need += tm * tn * 4
                if need > VMEM_TILE_BUDGET:
                    continue
                score = (tm * tn / (tm + tn), tk)
                if best is None or score > best[0]:
                    best = (score, (tm, tn, tk))
    assert best is not None, (M, N, K)
    return best[1]


def _matmul(a, b, *, mode, out_dtype, name, residual=None, out_cols=None):
    if mode == "nn":
        (M, K), N = a.shape, b.shape[1]
    elif mode == "nt":
        (M, K), N = a.shape, b.shape[0]
    else:
        (K, M), N = a.shape, b.shape[1]
    tm, tn, tk = _mm_tiles(M, N, K, a.dtype.itemsize, b.dtype.itemsize, jnp.dtype(out_dtype).itemsize,
                           residual is not None)
    nk = K // tk
    assert out_cols is None or 0 <= N - out_cols < tn
    dims = {"nn": NN_DIMS, "nt": NT_DIMS, "tn": TN_DIMS}[mode]

    def body(*refs):
        if residual is not None:
            a_ref, b_ref, r_ref, o_ref = refs[:4]
        else:
            a_ref, b_ref, o_ref = refs[:3]
            r_ref = None
        part = lax.dot_general(a_ref[...].astype(BF16), b_ref[...].astype(BF16), dims,
                               preferred_element_type=F32)
        if nk == 1:
            if r_ref is not None:
                part = part + r_ref[...]
            o_ref[...] = part.astype(o_ref.dtype)
            return
        acc_ref = refs[-1]
        k = pl.program_id(2)

        @pl.when(k == 0)
        def _():
            acc_ref[...] = part

        @pl.when(k > 0)
        def _():
            acc_ref[...] += part

        @pl.when(k == nk - 1)
        def _():
            r = acc_ref[...]
            if r_ref is not None:
                r = r + r_ref[...]
            o_ref[...] = r.astype(o_ref.dtype)

    if mode == "tn":
        a_spec = pl.BlockSpec((tk, tm), lambda i, j, k: (k, i))
    else:
        a_spec = pl.BlockSpec((tm, tk), lambda i, j, k: (i, k))
    if mode == "nt":
        b_spec = pl.BlockSpec((tn, tk), lambda i, j, k: (j, k))
    else:
        b_spec = pl.BlockSpec((tk, tn), lambda i, j, k: (k, j))
    o_spec = pl.BlockSpec((tm, tn), lambda i, j, k: (i, j))
    in_specs = [a_spec, b_spec]
    args = [a, b]
    if residual is not None:
        in_specs.append(o_spec)
        args.append(residual)
    return pl.pallas_call(
        body, name=name,
        out_shape=jax.ShapeDtypeStruct((M, N if out_cols is None else out_cols), out_dtype),
        grid=(M // tm, N // tn, nk),
        in_specs=in_specs, out_specs=o_spec,
        scratch_shapes=[pltpu.VMEM((tm, tn), F32)] if nk > 1 else [],
        compiler_params=_cparams("parallel", "parallel", "arbitrary"),
    )(*args)


def _row_tile(rows, width, n_arrays):
    cap = max(16, VMEM_TILE_BUDGET // (2 * n_arrays * width * 4))
    return _divisor_tile(rows, min(cap, 384), 16)


def _rms_fwd(x, g, *, col0, out_dtype, name):
    T = x.shape[0]
    W = g.shape[-1]
    assert col0 % W == 0
    cb = col0 // W
    tm = _row_tile(T, W, 3)

    def body(x_ref, g_ref, o_ref):
        xv = x_ref[...].astype(F32)
        r = lax.rsqrt(jnp.mean(xv * xv, axis=-1, keepdims=True) + EPS)
        o_ref[...] = ((xv * r) * g_ref[...]).astype(o_ref.dtype)

    return pl.pallas_call(
        body, name=name,
        out_shape=jax.ShapeDtypeStruct((T, W), out_dtype),
        grid=(T // tm,),
        in_specs=[pl.BlockSpec((tm, W), lambda i: (i, cb)), pl.BlockSpec((1, W), lambda i: (0, 0))],
        out_specs=pl.BlockSpec((tm, W), lambda i: (i, 0)),
        compiler_params=_cparams("parallel"),
    )(x, g.reshape(1, W))


def _rms_bwd(x, g, dy, *, col0, out_dtype, name, residual=None):
    T = x.shape[0]
    W = g.shape[-1]
    assert col0 % W == 0
    cb = col0 // W
    tm = _row_tile(T, W, 6)
    nt = T // tm

    def body(*refs):
        if residual is not None:
            x_ref, g_ref, dy_ref, r_ref, dx_ref, dg_ref, acc_ref = refs
        else:
            x_ref, g_ref, dy_ref, dx_ref, dg_ref, acc_ref = refs
            r_ref = None
        i = pl.program_id(0)
        xv = x_ref[...].astype(F32)
        r = lax.rsqrt(jnp.mean(xv * xv, axis=-1, keepdims=True) + EPS)
        xh = xv * r
        dyv = dy_ref[...].astype(F32)
        dxh = dyv * g_ref[...]
        dx = r * (dxh - xh * jnp.mean(dxh * xh, axis=-1, keepdims=True))
        if r_ref is not None:
            dx = dx + r_ref[...]
        dx_ref[...] = dx.astype(dx_ref.dtype)
        part = jnp.sum((dyv * xh).reshape(tm // 8, 8, W), axis=0)

        @pl.when(i == 0)
        def _():
            acc_ref[...] = part

        @pl.when(i > 0)
        def _():
            acc_ref[...] += part

        @pl.when(i == nt - 1)
        def _():
            dg_ref[...] = jnp.sum(acc_ref[...], axis=0, keepdims=True)

    row = pl.BlockSpec((tm, W), lambda i: (i, 0))
    in_specs = [pl.BlockSpec((tm, W), lambda i: (i, cb)), pl.BlockSpec((1, W), lambda i: (0, 0)), row]
    args = [x, g.reshape(1, W), dy]
    if residual is not None:
        in_specs.append(row)
        args.append(residual)
    return pl.pallas_call(
        body, name=name,
        out_shape=(jax.ShapeDtypeStruct((T, W), out_dtype), jax.ShapeDtypeStruct((1, W), F32)),
        grid=(nt,),
        in_specs=in_specs,
        out_specs=(row, pl.BlockSpec((1, W), lambda i: (0, 0))),
        scratch_shapes=[pltpu.VMEM((8, W), F32)],
        compiler_params=_cparams("arbitrary"),
    )(*args)


def _gate_fwd(o, proj, g, *, zcol0, name):
    T, W = o.shape
    assert zcol0 % W == 0
    zb = zcol0 // W
    tm = _row_tile(T, W, 4)

    def body(o_ref, z_ref, g_ref, y_ref):
        ov = o_ref[...]
        r = lax.rsqrt(jnp.mean(ov * ov, axis=-1, keepdims=True) + EPS)
        z = z_ref[...]
        sg = 1.0 / (1.0 + jnp.exp(-z))
        y_ref[...] = (((ov * r) * g_ref[...]) * (z * sg)).astype(y_ref.dtype)

    return pl.pallas_call(
        body, name=name,
        out_shape=jax.ShapeDtypeStruct((T, W), BF16),
        grid=(T // tm,),
        in_specs=[pl.BlockSpec((tm, W), lambda i: (i, 0)), pl.BlockSpec((tm, W), lambda i: (i, zb)),
                  pl.BlockSpec((1, W), lambda i: (0, 0))],
        out_specs=pl.BlockSpec((tm, W), lambda i: (i, 0)),
        compiler_params=_cparams("parallel"),
    )(o, proj, g.reshape(1, W))


def _gate_bwd(dy, o, proj, g, *, grp, zcol0, name):
    T, W = o.shape
    assert zcol0 % W == 0
    zb = zcol0 // W
    tm = _row_tile(T, W, 8)
    nt = T // tm

    def body(dy_ref, o_ref, z_ref, g_ref, do_ref, dz_ref, dg_ref, acc_ref):
        i = pl.program_id(0)
        ov = o_ref[...]
        r = lax.rsqrt(jnp.mean(ov * ov, axis=-1, keepdims=True) + EPS)
        xh = ov * r
        gv = g_ref[...]
        z = z_ref[...]
        sg = 1.0 / (1.0 + jnp.exp(-z))
        dyv = dy_ref[...]
        dn = dyv * (z * sg)
        dz_ref[...] = (dyv * (xh * gv) * (sg * (1.0 + z * (1.0 - sg)))).astype(dz_ref.dtype)
        dxh = dn * gv
        do_ref[...] = r * (dxh - xh * jnp.mean(dxh * xh, axis=-1, keepdims=True))
        part = jnp.sum((dn * xh).reshape(tm // 8, 8, W), axis=0)

        @pl.when(i == 0)
        def _():
            acc_ref[...] = part

        @pl.when(i > 0)
        def _():
            acc_ref[...] += part

        @pl.when(i == nt - 1)
        def _():
            dg_ref[...] = jnp.sum(acc_ref[...], axis=0, keepdims=True)

    row = pl.BlockSpec((tm, W), lambda i: (i, 0))
    return pl.pallas_call(
        body, name=name,
        out_shape=(jax.ShapeDtypeStruct((T, W), F32), jax.ShapeDtypeStruct((T, W), BF16),
                   jax.ShapeDtypeStruct((1, W), F32)),
        grid=(nt,),
        in_specs=[pl.BlockSpec((tm, W), lambda i: (i, grp)), row, pl.BlockSpec((tm, W), lambda i: (i, zb)),
                  pl.BlockSpec((1, W), lambda i: (0, 0))],
        out_specs=(row, row, pl.BlockSpec((1, W), lambda i: (0, 0))),
        scratch_shapes=[pltpu.VMEM((8, W), F32)],
        compiler_params=_cparams("arbitrary"),
    )(dy, o, proj, g.reshape(1, W))


def _rope_fwd(q, kv, proj, cosp, sinp, *, H, kr_col0, name):
    T = q.shape[0]
    HB = H * V7X_LANES
    tm = ROPE_ROWS
    krb = kr_col0 // V7X_LANES

    def body(q_ref, kv_ref, kra_ref, krb_ref, cos_ref, sin_ref, qc_ref, kc_ref, v_ref):
        cos = cos_ref[...]
        sin = sin_ref[...]
        kr = (kra_ref[...] * cos + krb_ref[...] * sin).astype(BF16)
        for h in range(H):
            lo, hi = h * 128, (h + 1) * 128
            qc_ref[:, 2 * lo:2 * lo + 128] = q_ref[:, lo:hi].astype(BF16)
            qc_ref[:, 2 * lo + 128:2 * hi] = (q_ref[:, HB + lo:HB + hi] * cos
                                              + q_ref[:, 2 * HB + lo:2 * HB + hi] * sin).astype(BF16)
            kc_ref[:, 2 * lo:2 * lo + 128] = kv_ref[:, 2 * lo:2 * lo + 128].astype(BF16)
            kc_ref[:, 2 * lo + 128:2 * hi] = kr
            v_ref[:, lo:hi] = kv_ref[:, 2 * lo + 128:2 * hi].astype(BF16)

    tab = pl.BlockSpec((tm, 128), lambda i: (i, 0))
    return pl.pallas_call(
        body, name=name,
        out_shape=(jax.ShapeDtypeStruct((T, 2 * HB), BF16), jax.ShapeDtypeStruct((T, 2 * HB), BF16),
                   jax.ShapeDtypeStruct((T, HB), BF16)),
        grid=(T // tm,),
        in_specs=[pl.BlockSpec((tm, 3 * HB), lambda i: (i, 0)), pl.BlockSpec((tm, 2 * HB), lambda i: (i, 0)),
                  pl.BlockSpec((tm, 128), lambda i: (i, krb)), pl.BlockSpec((tm, 128), lambda i: (i, krb + 1)),
                  tab, tab],
        out_specs=(pl.BlockSpec((tm, 2 * HB), lambda i: (i, 0)), pl.BlockSpec((tm, 2 * HB), lambda i: (i, 0)),
                   pl.BlockSpec((tm, HB), lambda i: (i, 0))),
        compiler_params=_cparams("parallel"),
    )(q, kv, proj, proj, cosp, sinp)


def _rope_bwd(dqc, dkc, dv, cosp, sinp, *, H, name):
    T = dqc.shape[0]
    HB = H * V7X_LANES
    tm = ROPE_ROWS

    def body(dqc_ref, dkc_ref, dv_ref, cos_ref, sin_ref, dq_ref, dkv_ref, dkr_ref):
        cos = cos_ref[...]
        sin = sin_ref[...]
        gk = jnp.zeros((tm, 128), F32)
        for h in range(H):
            lo, hi = h * 128, (h + 1) * 128
            dq_ref[:, lo:hi] = dqc_ref[:, 2 * lo:2 * lo + 128].astype(BF16)
            gq = dqc_ref[:, 2 * lo + 128:2 * hi]
            dq_ref[:, HB + lo:HB + hi] = (gq * cos).astype(BF16)
            dq_ref[:, 2 * HB + lo:2 * HB + hi] = (gq * sin).astype(BF16)
            dkv_ref[:, 2 * lo:2 * lo + 128] = dkc_ref[:, 2 * lo:2 * lo + 128].astype(BF16)
            dkv_ref[:, 2 * lo + 128:2 * hi] = dv_ref[:, lo:hi].astype(BF16)
            gk = gk + dkc_ref[:, 2 * lo + 128:2 * hi]
        dkr_ref[:, 0:128] = (gk * cos).astype(BF16)
        dkr_ref[:, 128:256] = (gk * sin).astype(BF16)

    tab = pl.BlockSpec((tm, 128), lambda i: (i, 0))
    return pl.pallas_call(
        body, name=name,
        out_shape=(jax.ShapeDtypeStruct((T, 3 * HB), BF16), jax.ShapeDtypeStruct((T, 2 * HB), BF16),
                   jax.ShapeDtypeStruct((T, 256), BF16)),
        grid=(T // tm,),
        in_specs=[pl.BlockSpec((tm, 2 * HB), lambda i: (i, 0)), pl.BlockSpec((tm, 2 * HB), lambda i: (i, 0)),
                  pl.BlockSpec((tm, HB), lambda i: (i, 0)), tab, tab],
        out_specs=(pl.BlockSpec((tm, 3 * HB), lambda i: (i, 0)), pl.BlockSpec((tm, 2 * HB), lambda i: (i, 0)),
                   pl.BlockSpec((tm, 256), lambda i: (i, 0))),
        compiler_params=_cparams("parallel"),
    )(dqc, dkc, dv, cosp, sinp)


def _mla_fwd(qc, kc, v, *, H, scale, name):
    T = qc.shape[0]
    tq = tk = ATT_BLK
    nq = T // tq

    def body(q_ref, k_ref, v_ref, o_ref, lse_ref):
        i = pl.program_id(1)
        q = q_ref[...]
        row = i * tq + lax.broadcasted_iota(jnp.int32, (tq, tk), 0)
        col = lax.broadcasted_iota(jnp.int32, (tq, tk), 1)

        def step(j, carry, masked):
            m, l, acc = carry
            off = pl.multiple_of(j * tk, tk)
            ks = k_ref[pl.ds(off, tk), :]
            vs = v_ref[pl.ds(off, tk), :]
            s = lax.dot_general(q, ks, NT_DIMS, preferred_element_type=F32) * scale
            if masked:
                s = jnp.where(col + j * tk <= row, s, NEG)
            m_new = jnp.maximum(m, jnp.max(s, axis=1, keepdims=True))
            alpha = jnp.exp(m - m_new)
            p = jnp.exp(s - m_new)
            l = alpha * l + jnp.sum(p, axis=1, keepdims=True)
            acc = alpha * acc + jnp.dot(p.astype(BF16), vs, preferred_element_type=F32)
            return m_new, l, acc

        n_full = (i * tq) // tk
        n_tot = ((i + 1) * tq + tk - 1) // tk
        carry = (jnp.full((tq, 1), NEG, F32), jnp.zeros((tq, 1), F32), jnp.zeros((tq, 128), F32))
        carry = lax.fori_loop(0, n_full, functools.partial(step, masked=False), carry)
        m, l, acc = lax.fori_loop(n_full, n_tot, functools.partial(step, masked=True), carry)
        o_ref[...] = acc / l
        lse_ref[0] = m + jnp.log(l)

    return pl.pallas_call(
        body, name=name,
        out_shape=(jax.ShapeDtypeStruct((T, H * 128), F32), jax.ShapeDtypeStruct((H, T, 1), F32)),
        grid=(H, nq),
        in_specs=[pl.BlockSpec((tq, 256), lambda h, i: (i, h)), pl.BlockSpec((T, 256), lambda h, i: (0, h)),
                  pl.BlockSpec((T, 128), lambda h, i: (0, h))],
        out_specs=(pl.BlockSpec((tq, 128), lambda h, i: (i, h)), pl.BlockSpec((1, tq, 1), lambda h, i: (h, i, 0))),
        compiler_params=_cparams("parallel", "parallel"),
    )(qc, kc, v)


def _mla_bwd(qc, kc, v, o, do, lse, *, H, scale, name, pieces=None):
    T = qc.shape[0]
    tq = tk = ATT_BLK
    nq, nk = T // tq, T // tk

    def body(*refs):
        if pieces is not None:
            (q_ref, k_ref, v_ref, o_ref, do_ref, lse_ref, p_ref, dq_ref, dk_ref, dv_ref, got_ref, delta_ref,
             send_sems, recv_sems) = refs
            cps = _scatter_copies(p_ref, got_ref, send_sems, recv_sems)

            @pl.when(pl.program_id(0) == 0)
            def _():
                for cp in cps:
                    cp.start()
        else:
            q_ref, k_ref, v_ref, o_ref, do_ref, lse_ref, dq_ref, dk_ref, dv_ref, delta_ref = refs
        dq_ref[...] = jnp.zeros_like(dq_ref)

        def fill_delta(i, c):
            off = pl.multiple_of(i * tq, tq)
            delta_ref[pl.ds(off, tq), :] = jnp.sum(do_ref[pl.ds(off, tq), :] * o_ref[pl.ds(off, tq), :],
                                                   axis=1, keepdims=True)
            return c

        lax.fori_loop(0, nq, fill_delta, 0)
        rowi = lax.broadcasted_iota(jnp.int32, (tq, tk), 0)
        coli = lax.broadcasted_iota(jnp.int32, (tq, tk), 1)

        def kblock(j, c):
            koff = pl.multiple_of(j * tk, tk)
            ks = k_ref[pl.ds(koff, tk), :]
            vs = v_ref[pl.ds(koff, tk), :]

            def qstep(i, carry, masked):
                dk, dv = carry
                qoff = pl.multiple_of(i * tq, tq)
                qs = q_ref[pl.ds(qoff, tq), :]
                dob = do_ref[pl.ds(qoff, tq), :].astype(BF16)
                s = lax.dot_general(qs, ks, NT_DIMS, preferred_element_type=F32) * scale
                if masked:
                    s = jnp.where(coli + j * tk <= rowi + i * tq, s, NEG)
                p = jnp.exp(s - lse_ref[0, pl.ds(qoff, tq), :])
                dv = dv + lax.dot_general(p.astype(BF16), dob, TN_DIMS, preferred_element_type=F32)
                dp = lax.dot_general(dob, vs, NT_DIMS, preferred_element_type=F32)
                ds = (p * (dp - delta_ref[pl.ds(qoff, tq), :]) * scale).astype(BF16)
                dk = dk + lax.dot_general(ds, qs, TN_DIMS, preferred_element_type=F32)
                dq_ref[pl.ds(qoff, tq), :] += jnp.dot(ds, ks, preferred_element_type=F32)
                return dk, dv

            i0 = (j * tk) // tq
            i1 = jnp.minimum(((j + 1) * tk + tq - 1) // tq, nq)
            carry = (jnp.zeros((tk, 256), F32), jnp.zeros((tk, 128), F32))
            carry = lax.fori_loop(i0, i1, functools.partial(qstep, masked=True), carry)
            dk, dv = lax.fori_loop(i1, nq, functools.partial(qstep, masked=False), carry)
            dk_ref[pl.ds(koff, tk), :] = dk
            dv_ref[pl.ds(koff, tk), :] = dv
            return c

        lax.fori_loop(0, nk, kblock, 0)
        if pieces is not None:
            @pl.when(pl.program_id(0) == H - 1)
            def _():
                for cp in cps:
                    cp.wait()

    wide = pl.BlockSpec((T, 256), lambda h: (0, h))
    narrow = pl.BlockSpec((T, 128), lambda h: (0, h))
    out_shape = [jax.ShapeDtypeStruct((T, H * 256), F32), jax.ShapeDtypeStruct((T, H * 256), F32),
                 jax.ShapeDtypeStruct((T, H * 128), F32)]
    in_specs = [wide, wide, narrow, narrow, narrow, pl.BlockSpec((1, T, 1), lambda h: (h, 0, 0))]
    out_specs = [wide, wide, narrow]
    scratch = [pltpu.VMEM((T, 1), F32)]
    args = [qc, kc, v, o, do, lse]
    if pieces is not None:
        out_shape.append(jax.ShapeDtypeStruct((3,) + pieces.shape[1:], pieces.dtype))
        in_specs.append(ANY)
        out_specs.append(ANY)
        scratch += [pltpu.SemaphoreType.DMA((3,)), pltpu.SemaphoreType.DMA((3,))]
        args.append(pieces)
    return pl.pallas_call(
        body, name=name,
        out_shape=tuple(out_shape),
        grid=(H,),
        in_specs=in_specs, out_specs=tuple(out_specs),
        scratch_shapes=scratch,
        compiler_params=_cparams("arbitrary"),
    )(*args)


def _log_sigmoid_pair(z):
    e = jnp.exp(-jnp.abs(z))
    lb = jnp.minimum(z, 0.0) - jnp.log(1.0 + e)
    inv = 1.0 / (1.0 + e)
    sg = jnp.where(z >= 0.0, inv, e * inv)
    return lb, lb - z, sg


def _tri_dot(x, tri):
    hi = x.astype(BF16)
    lo = (x - hi.astype(F32)).astype(BF16)
    return jnp.dot(hi, tri, preferred_element_type=F32) + jnp.dot(lo, tri, preferred_element_type=F32)


def _sb_fwd(proj, *, H, qcol0, kcol0, vcol0, scale, name):
    T = proj.shape[0]
    tq = tk = ATT_BLK
    nq = T // tq
    qb, kb, vb = qcol0 // 128, kcol0 // 128, vcol0 // 128

    assert T // tk <= V7X_LANES

    def body(q_ref, k_ref, v_ref, y_ref, rems_ref):
        i = pl.program_id(1)
        q = q_ref[...].astype(BF16)
        row = i * tq + lax.broadcasted_iota(jnp.int32, (tq, tk), 0)
        col = lax.broadcasted_iota(jnp.int32, (tq, tk), 1)
        r_i = lax.broadcasted_iota(jnp.int32, (tk, tk), 0)
        c_i = lax.broadcasted_iota(jnp.int32, (tk, tk), 1)
        tri_after = (r_i > c_i).astype(BF16)
        lane = lax.broadcasted_iota(jnp.int32, (tq, V7X_LANES), 1)

        def step(j, carry, masked):
            rem, acc, tab = carry
            off = pl.multiple_of(j * tk, tk)
            ks = k_ref[pl.ds(off, tk), :].astype(BF16)
            vs = v_ref[pl.ds(off, tk), :].astype(BF16)
            z = lax.dot_general(q, ks, NT_DIMS, preferred_element_type=F32) * scale
            lb, lom, _ = _log_sigmoid_pair(z)
            if masked:
                valid = col + j * tk < row
                lom = jnp.where(valid, lom, 0.0)
            a = jnp.exp(lb + _tri_dot(lom, tri_after) + rem)
            if masked:
                a = jnp.where(valid, a, 0.0)
            acc = acc + jnp.dot(a.astype(BF16), vs, preferred_element_type=F32)
            rem = rem + jnp.sum(lom, axis=1, keepdims=True)
            return rem, acc, jnp.where(lane == j, rem, tab)

        n_full = (i * tq) // tk
        n_tot = ((i + 1) * tq + tk - 1) // tk
        carry = (jnp.zeros((tq, 1), F32), jnp.zeros((tq, 128), F32), jnp.full((tq, V7X_LANES), NEG, F32))
        carry = lax.fori_loop(0, n_tot - n_full, lambda idx, c: step(n_tot - 1 - idx, c, True), carry)

        def alive(rem):
            return (jnp.max(rem) >= SB_DEAD).astype(jnp.int32)

        def more(state):
            idx, live, _ = state
            return (idx < n_full) & (live > 0)

        def back(state):
            idx, _, c = state
            c = step(n_full - 1 - idx, c, False)
            return idx + 1, alive(c[0]), c

        _, _, (rem, acc, tab) = lax.while_loop(more, back, (jnp.int32(0), alive(carry[0]), carry))
        y_ref[...] = acc
        rems_ref[0] = tab

    return pl.pallas_call(
        body, name=name,
        out_shape=(jax.ShapeDtypeStruct((T, H * 128), F32), jax.ShapeDtypeStruct((H, T, V7X_LANES), F32)),
        grid=(H, nq),
        in_specs=[pl.BlockSpec((tq, 128), lambda h, i: (i, qb + h)), pl.BlockSpec((T, 128), lambda h, i: (0, kb + h)),
                  pl.BlockSpec((T, 128), lambda h, i: (0, vb + h))],
        out_specs=(pl.BlockSpec((tq, 128), lambda h, i: (i, h)),
                   pl.BlockSpec((1, tq, V7X_LANES), lambda h, i: (h, i, 0))),
        compiler_params=_cparams("parallel", "parallel"),
    )(proj, proj, proj)


def _sb_bwd(proj, dy, rems, *, H, qcol0, kcol0, vcol0, scale, name):
    T = proj.shape[0]
    tq = tk = ATT_BLK
    nq = T // tq
    qb, kb, vb = qcol0 // 128, kcol0 // 128, vcol0 // 128

    def body(q_ref, k_ref, v_ref, dy_ref, rems_ref, dq_ref, dk_ref, dv_ref, dk_acc, dv_acc):
        dk_acc[...] = jnp.zeros_like(dk_acc)
        dv_acc[...] = jnp.zeros_like(dv_acc)
        rowi = lax.broadcasted_iota(jnp.int32, (tq, tk), 0)
        coli = lax.broadcasted_iota(jnp.int32, (tq, tk), 1)
        r_i = lax.broadcasted_iota(jnp.int32, (tk, tk), 0)
        c_i = lax.broadcasted_iota(jnp.int32, (tk, tk), 1)
        tri_upto = (r_i <= c_i).astype(BF16)
        tri_before = (r_i < c_i).astype(BF16)
        lane = lax.broadcasted_iota(jnp.int32, (tq, V7X_LANES), 1)
        lane1 = lax.broadcasted_iota(jnp.int32, (1, V7X_LANES), 1)

        def qblock(i, c):
            qoff = pl.multiple_of(i * tq, tq)
            qs = q_ref[pl.ds(qoff, tq), :].astype(BF16)
            dyb = dy_ref[pl.ds(qoff, tq), :].astype(BF16)
            tab = rems_ref[0, pl.ds(qoff, tq), :]

            def step(j, carry, masked):
                pre, dq = carry
                rem = jnp.sum(jnp.where(lane == j, tab, 0.0), axis=1, keepdims=True)
                koff = pl.multiple_of(j * tk, tk)
                ks = k_ref[pl.ds(koff, tk), :].astype(BF16)
                vs = v_ref[pl.ds(koff, tk), :].astype(BF16)
                z = lax.dot_general(qs, ks, NT_DIMS, preferred_element_type=F32) * scale
                lb, lom, sg = _log_sigmoid_pair(z)
                if masked:
                    valid = coli + j * tk < rowi + i * tq
                    lom = jnp.where(valid, lom, 0.0)
                a = jnp.exp(lb + rem - _tri_dot(lom, tri_upto))
                if masked:
                    a = jnp.where(valid, a, 0.0)
                dv_acc[pl.ds(koff, tk), :] += lax.dot_general(a.astype(BF16), dyb, TN_DIMS,
                                                              preferred_element_type=F32)
                de = a * lax.dot_general(dyb, vs, NT_DIMS, preferred_element_type=F32)
                before = pre + _tri_dot(de, tri_before)
                dz = de * (1.0 - sg) - before * sg
                if masked:
                    dz = jnp.where(valid, dz, 0.0)
                dzb = (dz * scale).astype(BF16)
                dk_acc[pl.ds(koff, tk), :] += lax.dot_general(dzb, qs, TN_DIMS, preferred_element_type=F32)
                dq = dq + jnp.dot(dzb, ks, preferred_element_type=F32)
                return pre + jnp.sum(de, axis=1, keepdims=True), dq

            n_full = (i * tq) // tk
            n_tot = ((i + 1) * tq + tk - 1) // tk
            colmax = jnp.max(tab, axis=0, keepdims=True)
            dead = (lane1 >= 1) & (lane1 <= n_full) & (colmax < SB_DEAD)
            j0 = jnp.sum(dead.astype(jnp.int32))
            carry = (jnp.zeros((tq, 1), F32), jnp.zeros((tq, 128), F32))
            carry = lax.fori_loop(j0, n_full, functools.partial(step, masked=False), carry)
            _, dq = lax.fori_loop(n_full, n_tot, functools.partial(step, masked=True), carry)
            dq_ref[pl.ds(qoff, tq), :] = dq.astype(dq_ref.dtype)
            return c

        lax.fori_loop(0, nq, qblock, 0)
        dk_ref[...] = dk_acc[...].astype(dk_ref.dtype)
        dv_ref[...] = dv_acc[...].astype(dv_ref.dtype)

    def seg(b):
        return pl.BlockSpec((T, 128), lambda h: (0, b + h))

    out = pl.BlockSpec((T, 128), lambda h: (0, h))
    return pl.pallas_call(
        body, name=name,
        out_shape=tuple(jax.ShapeDtypeStruct((T, H * 128), BF16) for _ in range(3)),
        grid=(H,),
        in_specs=[seg(qb), seg(kb), seg(vb), out, pl.BlockSpec((1, T, V7X_LANES), lambda h: (h, 0, 0))],
        out_specs=(out, out, out),
        scratch_shapes=[pltpu.VMEM((T, 128), F32), pltpu.VMEM((T, 128), F32)],
        compiler_params=_cparams("parallel"),
    )(proj, proj, proj, dy, rems)


def _final_loss(h, g, target, *, row0, n_rows, name):
    T, D = h.shape
    tm = _row_tile(T, D, 6)
    nt = T // tm

    def body(h_ref, g_ref, t_ref, dh_ref, dg_ref, loss_ref, acc_ref, lacc_ref):
        i = pl.program_id(0)
        xv = h_ref[...]
        r = lax.rsqrt(jnp.mean(xv * xv, axis=-1, keepdims=True) + EPS)
        xh = xv * r
        gv = g_ref[...]
        rows = i * tm + lax.broadcasted_iota(jnp.int32, (tm, 1), 0)
        valid = (rows >= row0) & (rows < row0 + n_rows)
        err = jnp.where(valid, xh * gv - t_ref[...], 0.0)
        dout = err * (1.0 / D)
        dxh = dout * gv
        dh_ref[...] = r * (dxh - xh * jnp.mean(dxh * xh, axis=-1, keepdims=True))
        part = jnp.sum((dout * xh).reshape(tm // 8, 8, D), axis=0)
        lpart = jnp.sum((err * err).reshape(tm // 8, 8, D), axis=0)

        @pl.when(i == 0)
        def _():
            acc_ref[...] = part
            lacc_ref[...] = lpart

        @pl.when(i > 0)
        def _():
            acc_ref[...] += part
            lacc_ref[...] += lpart

        @pl.when(i == nt - 1)
        def _():
            dg_ref[...] = jnp.sum(acc_ref[...], axis=0, keepdims=True)
            loss_ref[...] = (0.5 / D) * jnp.sum(jnp.sum(lacc_ref[...], axis=0, keepdims=True), axis=1, keepdims=True)

    row = pl.BlockSpec((tm, D), lambda i: (i, 0))
    vec = pl.BlockSpec((1, D), lambda i: (0, 0))
    return pl.pallas_call(
        body, name=name,
        out_shape=(jax.ShapeDtypeStruct((T, D), F32), jax.ShapeDtypeStruct((1, D), F32),
                   jax.ShapeDtypeStruct((1, 1), F32)),
        grid=(nt,),
        in_specs=[row, vec, row],
        out_specs=(row, vec, pl.BlockSpec((1, 1), lambda i: (0, 0))),
        scratch_shapes=[pltpu.VMEM((8, D), F32), pltpu.VMEM((8, D), F32)],
        compiler_params=_cparams("arbitrary"),
    )(h, g.reshape(1, D), target)


def _elementwise(fn, args, out_dtypes, name):
    shape = args[0].shape
    C = shape[-1]
    R = math.prod(shape[:-1])
    n = len(args) + len(out_dtypes)
    cap = max(16, (VMEM_TILE_BUDGET // 2) // (2 * n * C * 4))
    tr = _divisor_tile(R, cap, 16)
    n_in = len(args)

    def body(*refs):
        outs = fn(*[r[...] for r in refs[:n_in]])
        for o_ref, val in zip(refs[n_in:], outs):
            o_ref[...] = val.astype(o_ref.dtype)

    spec = pl.BlockSpec((tr, C), lambda i: (i, 0))
    res = pl.pallas_call(
        body, name=name,
        out_shape=tuple(jax.ShapeDtypeStruct((R, C), dt) for dt in out_dtypes),
        grid=(R // tr,),
        in_specs=[spec] * n_in, out_specs=tuple([spec] * len(out_dtypes)),
        compiler_params=_cparams("parallel"),
    )(*[a.reshape(R, C) for a in args])
    return tuple(r.reshape(shape) for r in res)


def _adamw_math(w, g, m, v):
    m = ADAM_B1 * m + (1.0 - ADAM_B1) * g
    v = ADAM_B2 * v + (1.0 - ADAM_B2) * (g * g)
    m_hat = m / (1.0 - ADAM_B1 ** ADAM_STEP)
    v_hat = v / (1.0 - ADAM_B2 ** ADAM_STEP)
    delta = -ADAM_LR * (m_hat / (jnp.sqrt(v_hat) + ADAM_EPS) + ADAM_WD * w)
    return delta, m, v


def _adamw(w, g, m, v, name):
    return _elementwise(_adamw_math, [w, g, m, v], [F32, F32, F32], name)


ANY = pl.BlockSpec(memory_space=pl.ANY)


def _position():
    return lax.axis_index("x"), lax.axis_index("y"), lax.axis_index("c")


def _gather_shards(a, name):
    def body(a_ref, o_ref, send_sems, recv_sems):
        x, y, c = _position()
        p = 2 * x + y
        chips = [(1 - x, y), (x, 1 - y), (1 - x, 1 - y)]

        def copy(k, src, dst, to):
            return pltpu.make_async_remote_copy(src_ref=src, dst_ref=dst, send_sem=send_sems.at[k],
                                                recv_sem=recv_sems.at[k], device_id=to, device_id_type=MESH)

        first = [copy(k, a_ref.at[c], o_ref.at[p, c], (qx, qy, c)) for k, (qx, qy) in enumerate(chips)]
        for cp in first:
            cp.start()
        passed = []
        for k, (qx, qy) in enumerate(chips):
            land = o_ref.at[2 * qx + qy, c]
            copy(k, land, land, (x, y, c)).wait_recv()
            fwd = copy(3 + k, land, land, (x, y, 1 - c))
            fwd.start()
            passed.append(fwd)
        for k, (qx, qy) in enumerate(chips):
            land = o_ref.at[2 * qx + qy, 1 - c]
            copy(3 + k, land, land, (x, y, c)).wait_recv()
        for cp in first + passed:
            cp.wait_send()

    return pl.pallas_call(
        body, name=name,
        out_shape=jax.ShapeDtypeStruct((4,) + a.shape, a.dtype),
        in_specs=[ANY], out_specs=ANY,
        scratch_shapes=[pltpu.SemaphoreType.DMA((6,)), pltpu.SemaphoreType.DMA((6,))],
    )(a)


def _with_own(gathered, own):
    x, y, _ = _position()
    p = 2 * x + y
    return [jnp.where(p == q, own, gathered[q]) for q in range(4)]


def _swap_halves(g, name):
    def body(g_ref, o_ref, send_sem, recv_sem):
        x, y, c = _position()
        cp = pltpu.make_async_remote_copy(src_ref=g_ref.at[1 - c], dst_ref=o_ref, send_sem=send_sem,
                                          recv_sem=recv_sem, device_id=(x, y, 1 - c), device_id_type=MESH)
        cp.start()
        cp.wait()

    return pl.pallas_call(
        body, name=name,
        out_shape=jax.ShapeDtypeStruct(g.shape[1:], g.dtype),
        in_specs=[ANY], out_specs=ANY,
        scratch_shapes=[pltpu.SemaphoreType.DMA, pltpu.SemaphoreType.DMA],
    )(g)


def _scatter_copies(p_ref, o_ref, send_sems, recv_sems):
    x, y, c = _position()
    chips = [(1 - x, y), (x, 1 - y), (1 - x, 1 - y)]
    return [pltpu.make_async_remote_copy(src_ref=p_ref.at[2 * qx + qy], dst_ref=o_ref.at[k],
                                         send_sem=send_sems.at[k], recv_sem=recv_sems.at[k],
                                         device_id=(qx, qy, c), device_id_type=MESH)
            for k, (qx, qy) in enumerate(chips)]


def _scatter_to_chips(pb, name):
    def body(p_ref, o_ref, send_sems, recv_sems):
        cps = _scatter_copies(p_ref, o_ref, send_sems, recv_sems)
        for cp in cps:
            cp.start()
        for cp in cps:
            cp.wait()

    return pl.pallas_call(
        body, name=name,
        out_shape=jax.ShapeDtypeStruct((3,) + pb.shape[1:], pb.dtype),
        in_specs=[ANY], out_specs=ANY,
        scratch_shapes=[pltpu.SemaphoreType.DMA((3,)), pltpu.SemaphoreType.DMA((3,))],
    )(pb)


def _join_halves(r, name):
    def body(r_ref, o_ref, send_sem, recv_sem):
        x, y, c = _position()
        cp = pltpu.make_async_remote_copy(src_ref=r_ref, dst_ref=o_ref, send_sem=send_sem,
                                          recv_sem=recv_sem, device_id=(x, y, 1 - c), device_id_type=MESH)
        cp.start()
        cp.wait()

    other = pl.pallas_call(
        body, name=name,
        out_shape=jax.ShapeDtypeStruct(r.shape, r.dtype),
        in_specs=[ANY], out_specs=ANY,
        scratch_shapes=[pltpu.SemaphoreType.DMA, pltpu.SemaphoreType.DMA],
    )(r)
    c = lax.axis_index("c")
    return jnp.stack([jnp.where(c == 0, r, other), jnp.where(c == 0, other, r)], axis=0)


def _reduce_to_shard(gh, tag):
    x, y, c = _position()
    p = 2 * x + y
    sib = _swap_halves(gh, f"swap_{tag}")
    mine = lax.dynamic_index_in_dim(gh, c, 0, keepdims=False)
    psum, pb = _elementwise(lambda a, b: (a + b, a + b), [mine, sib], [F32, BF16], f"pairsum_{tag}")
    got = _scatter_to_chips(pb, f"scatter_{tag}")
    own = lax.dynamic_index_in_dim(psum, p, 0, keepdims=False)
    (red,) = _elementwise(lambda o, a, b, d: (((o + a.astype(F32)) + b.astype(F32)) + d.astype(F32),),
                          [own, got[0], got[1], got[2]], [F32], f"chipsum_{tag}")
    return _join_halves(red, f"join_{tag}")


def _swap_rows(g, name):
    K, N = g.shape
    Kh = K // 2

    def body(g_ref, o_ref, send_sem, recv_sem):
        x, y, c = _position()
        theirs = g_ref.at[pl.ds(pl.multiple_of((1 - c) * Kh, 8), Kh)]
        cp = pltpu.make_async_remote_copy(src_ref=theirs, dst_ref=o_ref, send_sem=send_sem,
                                          recv_sem=recv_sem, device_id=(x, y, 1 - c), device_id_type=MESH)
        cp.start()
        cp.wait()

    return pl.pallas_call(
        body, name=name,
        out_shape=jax.ShapeDtypeStruct((Kh, N), g.dtype),
        in_specs=[ANY], out_specs=ANY,
        scratch_shapes=[pltpu.SemaphoreType.DMA, pltpu.SemaphoreType.DMA],
    )(g)


def _pairsum_rows(g, sib, name):
    K, N = g.shape
    Kh = K // 2
    cap = max(16, (VMEM_TILE_BUDGET // 2) // (2 * 4 * N * 4))
    tr = _divisor_tile(Kh, cap, 16)
    nb = Kh // tr

    def body(c_ref, g_ref, s_ref, p_ref, pb_ref):
        s = g_ref[...] + s_ref[...]
        p_ref[...] = s
        pb_ref[...] = s.astype(BF16)

    mine = pl.BlockSpec((tr, N), lambda i, c_ref: (i + c_ref[0] * nb, 0))
    row = pl.BlockSpec((tr, N), lambda i, c_ref: (i, 0))
    return pl.pallas_call(
        body, name=name,
        out_shape=(jax.ShapeDtypeStruct((Kh, N), F32), jax.ShapeDtypeStruct((Kh, N), BF16)),
        grid_spec=pltpu.PrefetchScalarGridSpec(num_scalar_prefetch=1, grid=(nb,), in_specs=[mine, row],
                                               out_specs=(row, row)),
        compiler_params=_cparams("parallel"),
    )(lax.axis_index("c").astype(jnp.int32).reshape(1), g, sib)


def _reduce_rows_begin(g, tag):
    K, N = g.shape
    n = N // 4
    x, y, _ = _position()
    sib = _swap_rows(g, f"swap_{tag}")
    psum, pb = _pairsum_rows(g, sib, f"pairsum_{tag}")
    pieces = pb.reshape(K // 2, 4, n).transpose(1, 0, 2)
    own = lax.dynamic_slice_in_dim(psum, (2 * x + y) * n, n, axis=1)
    return pieces, own


def _reduce_rows_end(own, got, tag):
    (red,) = _elementwise(lambda o, a, b, d: (((o + a.astype(F32)) + b.astype(F32)) + d.astype(F32),),
                          [own, got[0], got[1], got[2]], [F32], f"chipsum_{tag}")
    return _join_halves(red, f"join_{tag}").reshape(2 * own.shape[0], own.shape[1])


def _all_reduce_small(vec, name):
    R = vec.shape[0]

    def body(v_ref, o_ref, land_ref, send_sems, recv_sems):
        x, y, c = _position()
        me = 4 * x + 2 * y + c
        land_ref[me] = v_ref[...]
        cps = []
        for r in range(1, 8):
            rx, ry, rc = (r >> 2) & 1, (r >> 1) & 1, r & 1
            to = (x ^ rx, y ^ ry, c ^ rc)
            cps.append(pltpu.make_async_remote_copy(src_ref=v_ref, dst_ref=land_ref.at[me],
                                                    send_sem=send_sems.at[r - 1], recv_sem=recv_sems.at[r - 1],
                                                    device_id=to, device_id_type=MESH))
        for cp in cps:
            cp.start()
        for cp in cps:
            cp.wait()
        total = land_ref[0]
        for d in range(1, 8):
            total = total + land_ref[d]
        o_ref[...] = total

    return pl.pallas_call(
        body, name=name,
        out_shape=jax.ShapeDtypeStruct((R, 128), F32),
        in_specs=[pl.BlockSpec(memory_space=pltpu.VMEM)], out_specs=pl.BlockSpec(memory_space=pltpu.VMEM),
        scratch_shapes=[pltpu.VMEM((8, R, 128), F32), pltpu.SemaphoreType.DMA((7,)), pltpu.SemaphoreType.DMA((7,))],
    )(vec)


def _rot(w):
    half = ROPE_DIM // 2
    return jnp.concatenate([-w[..., half:], w[..., :half]], axis=-1)


def _unrot(g):
    half = ROPE_DIM // 2
    return jnp.concatenate([g[..., half:], -g[..., :half]], axis=-1)


class _Layout:
    def __init__(self, D, QL, KVL):
        self.D, self.QL, self.KVL = D, QL, KVL
        self.H = D // 256
        self.WG = self.H * 128
        WG = self.WG
        self.z_mla, self.q_sb, self.k_sb, self.v_sb, self.z_sb = 0, WG, 2 * WG, 3 * WG, 4 * WG
        self.c_q = 5 * WG
        self.c_kv = self.c_q + QL
        self.k_r = self.c_kv + KVL
        self.width = -(-(self.k_r + 256) // 512) * 512
        self.orig = (QL, KVL, ROPE_DIM, WG, WG, WG, WG, WG)
        self.din = sum(self.orig)

    def pack_w_in(self, w):
        cuts = []
        o = 0
        for s in self.orig:
            cuts.append(w[:, o:o + s])
            o += s
        c_q, c_kv, k_r, z_mla, q_sb, k_sb, v_sb, z_sb = cuts
        z64 = jnp.zeros((w.shape[0], 128 - ROPE_DIM), w.dtype)
        pad = jnp.zeros((w.shape[0], self.width - self.k_r - 256), w.dtype)
        return jnp.concatenate([z_mla, q_sb, k_sb, v_sb, z_sb, c_q, c_kv, k_r, z64, _rot(k_r), z64, pad], axis=1)

    def pack_w_uq(self, w):
        H = self.H
        w3 = w.reshape(w.shape[0], H, 128 + ROPE_DIM)
        nope = w3[:, :, :128]
        r = w3[:, :, 128:]
        z = jnp.zeros(r.shape, w.dtype)
        a = jnp.concatenate([r, z], axis=-1)
        b = jnp.concatenate([_rot(r), z], axis=-1)
        return jnp.concatenate([nope.reshape(-1, H * 128), a.reshape(-1, H * 128), b.reshape(-1, H * 128)], axis=1)

    def unpack_dw_uq(self, g):
        H = self.H
        HB = H * 128
        nope = g[:, :HB].reshape(-1, H, 128)
        a = g[:, HB:2 * HB].reshape(-1, H, 128)[:, :, :ROPE_DIM]
        b = g[:, 2 * HB:].reshape(-1, H, 128)[:, :, :ROPE_DIM]
        return jnp.concatenate([nope, a + _unrot(b)], axis=-1).reshape(-1, H * (128 + ROPE_DIM))


def _rope_tables(T):
    inv_freq = ROPE_THETA ** (-jnp.arange(0, ROPE_DIM, 2, dtype=F32) / ROPE_DIM)
    ang = jnp.arange(T, dtype=jnp.int32).astype(F32)[:, None] * inv_freq[None, :]
    z = jnp.zeros((T, 128 - ROPE_DIM), F32)
    cos, sin = jnp.cos(ang), jnp.sin(ang)
    return jnp.concatenate([cos, cos, z], axis=1), jnp.concatenate([sin, sin, z], axis=1)


def _layer_fwd(h, wl, lay, tabs):
    g_norm, w_in, g_q, g_kv, w_uq, w_ukv, g_mla, g_sb, w_o = wl
    cosp, sinp = tabs
    H = lay.H
    u = _rms_fwd(h, g_norm, col0=0, out_dtype=BF16, name="rms_h")
    proj = _matmul(u, w_in, mode="nn", out_dtype=F32, name="mm_in")
    cqn = _rms_fwd(proj, g_q, col0=lay.c_q, out_dtype=BF16, name="rms_cq")
    ckvn = _rms_fwd(proj, g_kv, col0=lay.c_kv, out_dtype=BF16, name="rms_ckv")
    q = _matmul(cqn, w_uq, mode="nn", out_dtype=F32, name="mm_uq")
    kv = _matmul(ckvn, w_ukv, mode="nn", out_dtype=F32, name="mm_ukv")
    qc, kc, v = _rope_fwd(q, kv, proj, cosp, sinp, H=H, kr_col0=lay.k_r, name="rope_fwd")
    o_mla, lse = _mla_fwd(qc, kc, v, H=H, scale=1.0 / math.sqrt(128 + ROPE_DIM), name="mla_fwd")
    o_sb, rems = _sb_fwd(proj, H=H, qcol0=lay.q_sb, kcol0=lay.k_sb, vcol0=lay.v_sb,
                        scale=1.0 / math.sqrt(128), name="sb_fwd")
    y_mla = _gate_fwd(o_mla, proj, g_mla, zcol0=lay.z_mla, name="gate_fwd_mla")
    y_sb = _gate_fwd(o_sb, proj, g_sb, zcol0=lay.z_sb, name="gate_fwd_sb")
    y = jnp.concatenate([y_mla, y_sb], axis=1)
    h_out = _matmul(y, w_o, mode="nn", out_dtype=F32, name="mm_o", residual=h)
    saved = (h, u, proj, cqn, ckvn, qc, kc, v, o_mla, lse, o_sb, rems, y)
    return h_out, saved


def _layer_bwd(dh, saved, wl, lay, tabs, carry):
    g_norm, w_in, g_q, g_kv, w_uq, w_ukv, g_mla, g_sb, w_o = wl
    h, u, proj, cqn, ckvn, qc, kc, v, o_mla, lse, o_sb, rems, y = saved
    cosp, sinp = tabs
    H = lay.H
    dy = _matmul(dh, w_o, mode="nt", out_dtype=F32, name="mm_o_dx")
    d_w_o = _matmul(y, dh, mode="tn", out_dtype=F32, name="mm_o_dw")
    do_mla, dz_mla, dg_mla = _gate_bwd(dy, o_mla, proj, g_mla, grp=0, zcol0=lay.z_mla, name="gate_bwd_mla")
    do_sb, dz_sb, dg_sb = _gate_bwd(dy, o_sb, proj, g_sb, grp=1, zcol0=lay.z_sb, name="gate_bwd_sb")
    dq_sb, dk_sb, dv_sb = _sb_bwd(proj, do_sb, rems, H=H, qcol0=lay.q_sb, kcol0=lay.k_sb, vcol0=lay.v_sb,
                                  scale=1.0 / math.sqrt(128), name="sb_bwd")
    if carry is None:
        dqc, dkc, dv = _mla_bwd(qc, kc, v, o_mla, do_mla, lse, H=H, scale=1.0 / math.sqrt(128 + ROPE_DIM),
                                name="mla_bwd")
        g_w_in_above = None
    else:
        dqc, dkc, dv, got = _mla_bwd(qc, kc, v, o_mla, do_mla, lse, H=H, scale=1.0 / math.sqrt(128 + ROPE_DIM),
                                     name="mla_bwd_scatter", pieces=carry[0])
        g_w_in_above = _reduce_rows_end(carry[1], got, "w_in")
    dq, dkv, dkr = _rope_bwd(dqc, dkc, dv, cosp, sinp, H=H, name="rope_bwd")
    d_w_uq = _matmul(cqn, dq, mode="tn", out_dtype=F32, name="mm_uq_dw")
    dcqn = _matmul(dq, w_uq, mode="nt", out_dtype=F32, name="mm_uq_dx")
    d_w_ukv = _matmul(ckvn, dkv, mode="tn", out_dtype=F32, name="mm_ukv_dw")
    dckvn = _matmul(dkv, w_ukv, mode="nt", out_dtype=F32, name="mm_ukv_dx")
    dcq, dg_q = _rms_bwd(proj, g_q, dcqn, col0=lay.c_q, out_dtype=BF16, name="rms_cq_bwd")
    dckv, dg_kv = _rms_bwd(proj, g_kv, dckvn, col0=lay.c_kv, out_dtype=BF16, name="rms_ckv_bwd")
    pad = jnp.zeros((dh.shape[0], lay.width - lay.k_r - 256), BF16)
    dproj = jnp.concatenate([dz_mla, dq_sb, dk_sb, dv_sb, dz_sb, dcq, dckv, dkr, pad], axis=1)
    du = _matmul(dproj, w_in, mode="nt", out_dtype=F32, name="mm_in_dx")
    dkr_o = (dkr[:, :ROPE_DIM].astype(F32) + _unrot(dkr[:, 128:128 + ROPE_DIM].astype(F32))).astype(BF16)
    pad_o = jnp.zeros((dh.shape[0], lay.width - lay.din), BF16)
    dproj_o = jnp.concatenate([dcq, dckv, dkr_o, dz_mla, dq_sb, dk_sb, dv_sb, dz_sb, pad_o], axis=1)
    d_w_in = _matmul(u, dproj_o, mode="tn", out_dtype=F32, name="mm_in_dw", out_cols=lay.din)
    dh_prev, dg_norm = _rms_bwd(h, g_norm, du, col0=0, out_dtype=F32, name="rms_h_bwd", residual=dh)
    grads = (dg_norm[0], None, dg_q[0], dg_kv[0], lay.unpack_dw_uq(d_w_uq), d_w_ukv, dg_mla[0], dg_sb[0], d_w_o)
    return dh_prev, grads, _reduce_rows_begin(d_w_in, "w_in"), g_w_in_above


def _halves(a):
    return a.reshape((2, a.shape[0] // 2) + a.shape[1:])


def _gather_cols(w, name):
    L, K, n = w.shape
    own = w.astype(BF16)
    g = _with_own(_gather_shards(_halves(own), name).reshape(4, L, K, n), own)
    return [jnp.concatenate([g[q][l] for q in range(4)], axis=1) for l in range(L)]


def _cut_cols(g):
    L, K, N = g.shape
    return g.reshape(2, L // 2, K, 4, N // 4).transpose(0, 3, 1, 2, 4)


def kernel(x, meta_tokens, g_norm, w_in, g_q, g_kv, w_uq, w_ukv, g_out_mla, g_out_sb, w_o, g_final, loss_target, m_meta_tokens, m_g_norm, m_w_in, m_g_q, m_g_kv, m_w_uq, m_w_ukv, m_g_out_mla, m_g_out_sb, m_w_o, m_g_final, v_meta_tokens, v_g_norm, v_w_in, v_g_q, v_g_kv, v_w_uq, v_w_ukv, v_g_out_mla, v_g_out_sb, v_w_o, v_g_final):
    _, S, D = x.shape
    NM = meta_tokens.shape[0]
    L = g_norm.shape[0]
    lay = _Layout(D, g_q.shape[1], g_kv.shape[1])
    TP = -(-(NM + S) // ROW_ALIGN) * ROW_ALIGN
    tabs = _rope_tables(TP)

    w_in_full = [lay.pack_w_in(w) for w in _gather_cols(w_in, "gather_w_in")]
    w_uq_full = [lay.pack_w_uq(w) for w in _gather_cols(w_uq, "gather_w_uq")]
    w_ukv_full = _gather_cols(w_ukv, "gather_w_ukv")
    w_o_own = w_o.astype(BF16)
    w_o_g = _with_own(_gather_shards(_halves(w_o_own), "gather_w_o").reshape((4, L) + w_o.shape[1:]), w_o_own)
    w_o_full = [jnp.concatenate([w_o_g[q][l] for q in range(4)], axis=0) for l in range(L)]
    meta_g = _with_own(_gather_shards(meta_tokens.reshape(2, NM // 2, -1), "gather_meta").reshape(4, NM, -1),
                       meta_tokens)
    meta_full = jnp.concatenate(meta_g, axis=1)

    h = jnp.concatenate([meta_full, x[0], jnp.zeros((TP - NM - S, D), F32)], axis=0)
    target = jnp.pad(loss_target[0], ((NM, TP - NM - S), (0, 0)))
    weights = [(g_norm[l], w_in_full[l], g_q[l], g_kv[l], w_uq_full[l], w_ukv_full[l], g_out_mla[l],
                g_out_sb[l], w_o_full[l]) for l in range(L)]
    saved = []
    for l in range(L):
        h, s = _layer_fwd(h, weights[l], lay, tabs)
        saved.append(s)
    dh, dg_final, loss_part = _final_loss(h, g_final, target, row0=NM, n_rows=S, name="final_loss")

    layer_grads = [None] * L
    g_w_in_layers = [None] * L
    carry = None
    for l in reversed(range(L)):
        dh, layer_grads[l], carry, g_above = _layer_bwd(dh, saved[l], weights[l], lay, tabs, carry)
        if g_above is not None:
            g_w_in_layers[l + 1] = g_above
    g_w_in_layers[0] = _reduce_rows_end(carry[1], _scatter_to_chips(carry[0], "scatter_w_in"), "w_in")
    grad_x = dh[NM:NM + S][None]
    d_meta = dh[:NM]

    def stack(i):
        return jnp.stack([layer_grads[l][i] for l in range(L)], axis=0)

    small = [stack(0), stack(2), stack(3), stack(6), stack(7), dg_final[0]]
    flat = jnp.concatenate([s.reshape(-1) for s in small])
    n_flat = flat.shape[0]
    rows = -(-n_flat // (8 * 128)) * 8
    packed = jnp.pad(flat, (0, rows * 128 - n_flat)).reshape(rows, 128)
    summed = _all_reduce_small(packed, "allreduce_gains").reshape(-1)
    small_red = []
    o = 0
    for s in small:
        small_red.append(summed[o:o + s.size].reshape(s.shape))
        o += s.size
    g_g_norm, g_g_q, g_g_kv, g_g_mla, g_g_sb, g_g_final = small_red

    g_w_in = jnp.stack(g_w_in_layers, axis=0)
    g_w_uq = _reduce_to_shard(_cut_cols(stack(4)), "w_uq").reshape(w_uq.shape)
    g_w_ukv = _reduce_to_shard(_cut_cols(stack(5)), "w_ukv").reshape(w_ukv.shape)
    d_w_o = stack(8).reshape(2, L // 2, 4, w_o.shape[1], D).transpose(0, 2, 1, 3, 4)
    g_w_o = _reduce_to_shard(d_w_o, "w_o").reshape(w_o.shape)
    d_meta = d_meta.reshape(2, NM // 2, 4, D // 4).transpose(0, 2, 1, 3)
    g_meta = _reduce_to_shard(d_meta, "meta").reshape(meta_tokens.shape)

    loss = lax.psum(loss_part[0, 0], ("x", "y", "c"))

    names = ["meta", "g_norm", "w_in", "g_q", "g_kv", "w_uq", "w_ukv", "g_out_mla", "g_out_sb", "w_o", "g_final"]
    ws = [meta_tokens, g_norm, w_in, g_q, g_kv, w_uq, w_ukv, g_out_mla, g_out_sb, w_o, g_final]
    gs = [g_meta, g_g_norm, g_w_in, g_g_q, g_g_kv, g_w_uq, g_w_ukv, g_g_mla, g_g_sb, g_w_o, g_g_final]
    ms = [m_meta_tokens, m_g_norm, m_w_in, m_g_q, m_g_kv, m_w_uq, m_w_ukv, m_g_out_mla, m_g_out_sb, m_w_o, m_g_final]
    vs = [v_meta_tokens, v_g_norm, v_w_in, v_g_q, v_g_kv, v_w_uq, v_w_ukv, v_g_out_mla, v_g_out_sb, v_w_o, v_g_final]
    deltas, new_m, new_v = [], [], []
    for n, w, g, m, v in zip(names, ws, gs, ms, vs):
        shape = w.shape
        if w.ndim == 1:
            w, g, m, v = (a.reshape(1, -1) for a in (w, g, m, v))
        d, nm, nv = _adamw(w, g, m, v, f"adamw_{n}")
        deltas.append(d.reshape(shape))
        new_m.append(nm.reshape(shape))
        new_v.append(nv.reshape(shape))
    return (loss, grad_x, *gs, *deltas, *new_m, *new_v)
```

```python
import functools
import math

import jax
import jax.numpy as jnp
from jax import lax
from jax.experimental import pallas as pl
from jax.experimental.pallas import tpu as pltpu

F32 = jnp.float32
BF16 = jnp.bfloat16
MESH = pl.DeviceIdType.MESH

V7X_LANES = 128
VMEM_LIMIT = 56 * 1024 * 1024
VMEM_TILE_BUDGET = 40 * 1024 * 1024

ROPE_DIM = 64
ROPE_THETA = 10000.0
EPS = 1e-6
ROW_ALIGN = 384
ATT_BLK = 384
ROPE_ROWS = 128

ADAM_LR = 0.001
ADAM_B1 = 0.9
ADAM_B2 = 0.999
ADAM_EPS = 1e-08
ADAM_WD = 0.01
ADAM_STEP = 10

NEG = -1e30
SB_DEAD = -104.0
NT_DIMS = (((1,), (1,)), ((), ()))
TN_DIMS = (((0,), (0,)), ((), ()))
NN_DIMS = (((1,), (0,)), ((), ()))


def _cparams(*sem):
    return pltpu.CompilerParams(dimension_semantics=sem, vmem_limit_bytes=VMEM_LIMIT)


def _divisor_tile(n, cap, align):
    best = None
    t = align
    while t <= min(n, cap):
        if n % t == 0:
            best = t
        t += align
    return best if best is not None else n


def _mm_tiles(M, N, K, a_bytes, b_bytes, o_bytes, has_res):
    best = None
    for tm in sorted({_divisor_tile(M, c, 128) for c in (1408, 1024, 704, 512, 384, 256, 128)}, reverse=True):
        for tn in sorted({_divisor_tile(N, c, 128) for c in (1024, 512, 256, 128)}, reverse=True):
            for tk in sorted({_divisor_tile(K, c, 128) for c in (4096, 2048, 1408, 1024, 704, 512, 384, 256, 128)},
                             reverse=True):
                need = 2 * (tm * tk * a_bytes + tk * tn * b_bytes + tm * tn * o_bytes)
                need += 2 * tm * tn * 4 if has_res else 0
                need += tm * tn * 4 if tk != K else 0
                need += tm * tk * 2 if a_bytes != 2 else 0
                need += tk * tn * 2 if b_bytes != 2 else 0
                need += tm * tn * 4
                if need > VMEM_TILE_BUDGET:
                    continue
                score = (tm * tn / (tm + tn), tk)
                if best is None or score > best[0]:
                    best = (score, (tm, tn, tk))
    assert best is not None, (M, N, K)
    return best[1]


def _matmul(a, b, *, mode, out_dtype, name, residual=None, out_cols=None):
    if mode == "nn":
        (M, K), N = a.shape, b.shape[1]
    elif mode == "nt":
        (M, K), N = a.shape, b.shape[0]
    else:
        (K, M), N = a.shape, b.shape[1]
    tm, tn, tk = _mm_tiles(M, N, K, a.dtype.itemsize, b.dtype.itemsize, jnp.dtype(out_dtype).itemsize,
                           residual is not None)
    nk = K // tk
    assert out_cols is None or 0 <= N - out_cols < tn
    dims = {"nn": NN_DIMS, "nt": NT_DIMS, "tn": TN_DIMS}[mode]

    def body(*refs):
        if residual is not None:
            a_ref, b_ref, r_ref, o_ref = refs[:4]
        else:
            a_ref, b_ref, o_ref = refs[:3]
            r_ref = None
        part = lax.dot_general(a_ref[...].astype(BF16), b_ref[...].astype(BF16), dims,
                               preferred_element_type=F32)
        if nk == 1:
            if r_ref is not None:
                part = part + r_ref[...]
            o_ref[...] = part.astype(o_ref.dtype)
            return
        acc_ref = refs[-1]
        k = pl.program_id(2)

        @pl.when(k == 0)
        def _():
            acc_ref[...] = part

        @pl.when(k > 0)
        def _():
            acc_ref[...] += part

        @pl.when(k == nk - 1)
        def _():
            r = acc_ref[...]
            if r_ref is not None:
                r = r + r_ref[...]
            o_ref[...] = r.astype(o_ref.dtype)

    if mode == "tn":
        a_spec = pl.BlockSpec((tk, tm), lambda i, j, k: (k, i))
    else:
        a_spec = pl.BlockSpec((tm, tk), lambda i, j, k: (i, k))
    if mode == "nt":
        b_spec = pl.BlockSpec((tn, tk), lambda i, j, k: (j, k))
    else:
        b_spec = pl.BlockSpec((tk, tn), lambda i, j, k: (k, j))
    o_spec = pl.BlockSpec((tm, tn), lambda i, j, k: (i, j))
    in_specs = [a_spec, b_spec]
    args = [a, b]
    if residual is not None:
        in_specs.append(o_spec)
        args.append(residual)
    return pl.pallas_call(
        body, name=name,
        out_shape=jax.ShapeDtypeStruct((M, N if out_cols is None else out_cols), out_dtype),
        grid=(M // tm, N // tn, nk),
        in_specs=in_specs, out_specs=o_spec,
        scratch_shapes=[pltpu.VMEM((tm, tn), F32)] if nk > 1 else [],
        compiler_params=_cparams("parallel", "parallel", "arbitrary"),
    )(*args)


def _row_tile(rows, width, n_arrays):
    cap = max(16, VMEM_TILE_BUDGET // (2 * n_arrays * width * 4))
    return _divisor_tile(rows, min(cap, 384), 16)


def _rms_fwd(x, g, *, col0, out_dtype, name):
    T = x.shape[0]
    W = g.shape[-1]
    assert col0 % W == 0
    cb = col0 // W
    tm = _row_tile(T, W, 3)

    def body(x_ref, g_ref, o_ref):
        xv = x_ref[...].astype(F32)
        r = lax.rsqrt(jnp.mean(xv * xv, axis=-1, keepdims=True) + EPS)
        o_ref[...] = ((xv * r) * g_ref[...]).astype(o_ref.dtype)

    return pl.pallas_call(
        body, name=name,
        out_shape=jax.ShapeDtypeStruct((T, W), out_dtype),
        grid=(T // tm,),
        in_specs=[pl.BlockSpec((tm, W), lambda i: (i, cb)), pl.BlockSpec((1, W), lambda i: (0, 0))],
        out_specs=pl.BlockSpec((tm, W), lambda i: (i, 0)),
        compiler_params=_cparams("parallel"),
    )(x, g.reshape(1, W))


def _rms_bwd(x, g, dy, *, col0, out_dtype, name, residual=None):
    T = x.shape[0]
    W = g.shape[-1]
    assert col0 % W == 0
    cb = col0 // W
    tm = _row_tile(T, W, 6)
    nt = T // tm

    def body(*refs):
        if residual is not None:
            x_ref, g_ref, dy_ref, r_ref, dx_ref, dg_ref, acc_ref = refs
        else:
            x_ref, g_ref, dy_ref, dx_ref, dg_ref, acc_ref = refs
            r_ref = None
        i = pl.program_id(0)
        xv = x_ref[...].astype(F32)
        r = lax.rsqrt(jnp.mean(xv * xv, axis=-1, keepdims=True) + EPS)
        xh = xv * r
        dyv = dy_ref[...].astype(F32)
        dxh = dyv * g_ref[...]
        dx = r * (dxh - xh * jnp.mean(dxh * xh, axis=-1, keepdims=True))
        if r_ref is not None:
            dx = dx + r_ref[...]
        dx_ref[...] = dx.astype(dx_ref.dtype)
        part = jnp.sum((dyv * xh).reshape(tm // 8, 8, W), axis=0)

        @pl.when(i == 0)
        def _():
            acc_ref[...] = part

        @pl.when(i > 0)
        def _():
            acc_ref[...] += part

        @pl.when(i == nt - 1)
        def _():
            dg_ref[...] = jnp.sum(acc_ref[...], axis=0, keepdims=True)

    row = pl.BlockSpec((tm, W), lambda i: (i, 0))
    in_specs = [pl.BlockSpec((tm, W), lambda i: (i, cb)), pl.BlockSpec((1, W), lambda i: (0, 0)), row]
    args = [x, g.reshape(1, W), dy]
    if residual is not None:
        in_specs.append(row)
        args.append(residual)
    return pl.pallas_call(
        body, name=name,
        out_shape=(jax.ShapeDtypeStruct((T, W), out_dtype), jax.ShapeDtypeStruct((1, W), F32)),
        grid=(nt,),
        in_specs=in_specs,
        out_specs=(row, pl.BlockSpec((1, W), lambda i: (0, 0))),
        scratch_shapes=[pltpu.VMEM((8, W), F32)],
        compiler_params=_cparams("arbitrary"),
    )(*args)


def _gate_fwd(o, proj, g, *, zcol0, name):
    T, W = o.shape
    assert zcol0 % W == 0
    zb = zcol0 // W
    tm = _row_tile(T, W, 4)

    def body(o_ref, z_ref, g_ref, y_ref):
        ov = o_ref[...]
        r = lax.rsqrt(jnp.mean(ov * ov, axis=-1, keepdims=True) + EPS)
        z = z_ref[...]
        sg = 1.0 / (1.0 + jnp.exp(-z))
        y_ref[...] = (((ov * r) * g_ref[...]) * (z * sg)).astype(y_ref.dtype)

    return pl.pallas_call(
        body, name=name,
        out_shape=jax.ShapeDtypeStruct((T, W), BF16),
        grid=(T // tm,),
        in_specs=[pl.BlockSpec((tm, W), lambda i: (i, 0)), pl.BlockSpec((tm, W), lambda i: (i, zb)),
                  pl.BlockSpec((1, W), lambda i: (0, 0))],
        out_specs=pl.BlockSpec((tm, W), lambda i: (i, 0)),
        compiler_params=_cparams("parallel"),
    )(o, proj, g.reshape(1, W))


def _gate_bwd(dy, o, proj, g, *, grp, zcol0, name):
    T, W = o.shape
    assert zcol0 % W == 0
    zb = zcol0 // W
    tm = _row_tile(T, W, 8)
    nt = T // tm

    def body(dy_ref, o_ref, z_ref, g_ref, do_ref, dz_ref, dg_ref, acc_ref):
        i = pl.program_id(0)
        ov = o_ref[...]
        r = lax.rsqrt(jnp.mean(ov * ov, axis=-1, keepdims=True) + EPS)
        xh = ov * r
        gv = g_ref[...]
        z = z_ref[...]
        sg = 1.0 / (1.0 + jnp.exp(-z))
        dyv = dy_ref[...]
        dn = dyv * (z * sg)
        dz_ref[...] = (dyv * (xh * gv) * (sg * (1.0 + z * (1.0 - sg)))).astype(dz_ref.dtype)
        dxh = dn * gv
        do_ref[...] = r * (dxh - xh * jnp.mean(dxh * xh, axis=-1, keepdims=True))
        part = jnp.sum((dn * xh).reshape(tm // 8, 8, W), axis=0)

        @pl.when(i == 0)
        def _():
            acc_ref[...] = part

        @pl.when(i > 0)
        def _():
            acc_ref[...] += part

        @pl.when(i == nt - 1)
        def _():
            dg_ref[...] = jnp.sum(acc_ref[...], axis=0, keepdims=True)

    row = pl.BlockSpec((tm, W), lambda i: (i, 0))
    return pl.pallas_call(
        body, name=name,
        out_shape=(jax.ShapeDtypeStruct((T, W), F32), jax.ShapeDtypeStruct((T, W), BF16),
                   jax.ShapeDtypeStruct((1, W), F32)),
        grid=(nt,),
        in_specs=[pl.BlockSpec((tm, W), lambda i: (i, grp)), row, pl.BlockSpec((tm, W), lambda i: (i, zb)),
                  pl.BlockSpec((1, W), lambda i: (0, 0))],
        out_specs=(row, row, pl.BlockSpec((1, W), lambda i: (0, 0))),
        scratch_shapes=[pltpu.VMEM((8, W), F32)],
        compiler_params=_cparams("arbitrary"),
    )(dy, o, proj, g.reshape(1, W))


def _rope_fwd(q, kv, proj, cosp, sinp, *, H, kr_col0, name):
    T = q.shape[0]
    HB = H * V7X_LANES
    tm = ROPE_ROWS
    krb = kr_col0 // V7X_LANES

    def body(q_ref, kv_ref, kra_ref, krb_ref, cos_ref, sin_ref, qc_ref, kc_ref, v_ref):
        cos = cos_ref[...]
        sin = sin_ref[...]
        kr = (kra_ref[...] * cos + krb_ref[...] * sin).astype(BF16)
        for h in range(H):
            lo, hi = h * 128, (h + 1) * 128
            qc_ref[:, 2 * lo:2 * lo + 128] = q_ref[:, lo:hi].astype(BF16)
            qc_ref[:, 2 * lo + 128:2 * hi] = (q_ref[:, HB + lo:HB + hi] * cos
                                              + q_ref[:, 2 * HB + lo:2 * HB + hi] * sin).astype(BF16)
            kc_ref[:, 2 * lo:2 * lo + 128] = kv_ref[:, 2 * lo:2 * lo + 128].astype(BF16)
            kc_ref[:, 2 * lo + 128:2 * hi] = kr
            v_ref[:, lo:hi] = kv_ref[:, 2 * lo + 128:2 * hi].astype(BF16)

    tab = pl.BlockSpec((tm, 128), lambda i: (i, 0))
    return pl.pallas_call(
        body, name=name,
        out_shape=(jax.ShapeDtypeStruct((T, 2 * HB), BF16), jax.ShapeDtypeStruct((T, 2 * HB), BF16),
                   jax.ShapeDtypeStruct((T, HB), BF16)),
        grid=(T // tm,),
        in_specs=[pl.BlockSpec((tm, 3 * HB), lambda i: (i, 0)), pl.BlockSpec((tm, 2 * HB), lambda i: (i, 0)),
                  pl.BlockSpec((tm, 128), lambda i: (i, krb)), pl.BlockSpec((tm, 128), lambda i: (i, krb + 1)),
                  tab, tab],
        out_specs=(pl.BlockSpec((tm, 2 * HB), lambda i: (i, 0)), pl.BlockSpec((tm, 2 * HB), lambda i: (i, 0)),
                   pl.BlockSpec((tm, HB), lambda i: (i, 0))),
        compiler_params=_cparams("parallel"),
    )(q, kv, proj, proj, cosp, sinp)


def _rope_bwd(dqc, dkc, dv, cosp, sinp, *, H, name):
    T = dqc.shape[0]
    HB = H * V7X_LANES
    tm = ROPE_ROWS

    def body(dqc_ref, dkc_ref, dv_ref, cos_ref, sin_ref, dq_ref, dkv_ref, dkr_ref):
        cos = cos_ref[...]
        sin = sin_ref[...]
        gk = jnp.zeros((tm, 128), F32)
        for h in range(H):
            lo, hi = h * 128, (h + 1) * 128
            dq_ref[:, lo:hi] = dqc_ref[:, 2 * lo:2 * lo + 128].astype(BF16)
            gq = dqc_ref[:, 2 * lo + 128:2 * hi]
            dq_ref[:, HB + lo:HB + hi] = (gq * cos).astype(BF16)
            dq_ref[:, 2 * HB + lo:2 * HB + hi] = (gq * sin).astype(BF16)
            dkv_ref[:, 2 * lo:2 * lo + 128] = dkc_ref[:, 2 * lo:2 * lo + 128].astype(BF16)
            dkv_ref[:, 2 * lo + 128:2 * hi] = dv_ref[:, lo:hi].astype(BF16)
            gk = gk + dkc_ref[:, 2 * lo + 128:2 * hi]
        dkr_ref[:, 0:128] = (gk * cos).astype(BF16)
        dkr_ref[:, 128:256] = (gk * sin).astype(BF16)

    tab = pl.BlockSpec((tm, 128), lambda i: (i, 0))
    return pl.pallas_call(
        body, name=name,
        out_shape=(jax.ShapeDtypeStruct((T, 3 * HB), BF16), jax.ShapeDtypeStruct((T, 2 * HB), BF16),
                   jax.ShapeDtypeStruct((T, 256), BF16)),
        grid=(T // tm,),
        in_specs=[pl.BlockSpec((tm, 2 * HB), lambda i: (i, 0)), pl.BlockSpec((tm, 2 * HB), lambda i: (i, 0)),
                  pl.BlockSpec((tm, HB), lambda i: (i, 0)), tab, tab],
        out_specs=(pl.BlockSpec((tm, 3 * HB), lambda i: (i, 0)), pl.BlockSpec((tm, 2 * HB), lambda i: (i, 0)),
                   pl.BlockSpec((tm, 256), lambda i: (i, 0))),
        compiler_params=_cparams("parallel"),
    )(dqc, dkc, dv, cosp, sinp)


def _mla_fwd(qc, kc, v, *, H, scale, name, ride=None):
    T = qc.shape[0]
    tq = tk = ATT_BLK
    nq = T // tq

    def body(*refs):
        (q_ref, k_ref, v_ref, o_ref, lse_ref), cps = _ride_unpack(ride, refs, 3, 2)
        i = pl.program_id(1)
        _ride_start(cps, (pl.program_id(0) == 0) & (i == 0))
        q = q_ref[...]
        row = i * tq + lax.broadcasted_iota(jnp.int32, (tq, tk), 0)
        col = lax.broadcasted_iota(jnp.int32, (tq, tk), 1)

        def step(j, carry, masked):
            m, l, acc = carry
            off = pl.multiple_of(j * tk, tk)
            ks = k_ref[pl.ds(off, tk), :]
            vs = v_ref[pl.ds(off, tk), :]
            s = lax.dot_general(q, ks, NT_DIMS, preferred_element_type=F32) * scale
            if masked:
                s = jnp.where(col + j * tk <= row, s, NEG)
            m_new = jnp.maximum(m, jnp.max(s, axis=1, keepdims=True))
            alpha = jnp.exp(m - m_new)
            p = jnp.exp(s - m_new)
            l = alpha * l + jnp.sum(p, axis=1, keepdims=True)
            acc = alpha * acc + jnp.dot(p.astype(BF16), vs, preferred_element_type=F32)
            return m_new, l, acc

        def two_steps(t, carry):
            m, l, acc = carry
            off_a = pl.multiple_of(2 * t * tk, tk)
            off_b = pl.multiple_of((2 * t + 1) * tk, tk)
            s_a = lax.dot_general(q, k_ref[pl.ds(off_a, tk), :], NT_DIMS, preferred_element_type=F32) * scale
            s_b = lax.dot_general(q, k_ref[pl.ds(off_b, tk), :], NT_DIMS, preferred_element_type=F32) * scale
            m_new = jnp.maximum(m, jnp.maximum(jnp.max(s_a, axis=1, keepdims=True),
                                               jnp.max(s_b, axis=1, keepdims=True)))
            alpha = jnp.exp(m - m_new)
            p_a = jnp.exp(s_a - m_new)
            p_b = jnp.exp(s_b - m_new)
            l = alpha * l + (jnp.sum(p_a, axis=1, keepdims=True) + jnp.sum(p_b, axis=1, keepdims=True))
            acc = alpha * acc + (jnp.dot(p_a.astype(BF16), v_ref[pl.ds(off_a, tk), :], preferred_element_type=F32)
                                 + jnp.dot(p_b.astype(BF16), v_ref[pl.ds(off_b, tk), :], preferred_element_type=F32))
            return m_new, l, acc

        n_full = (i * tq) // tk
        n_tot = ((i + 1) * tq + tk - 1) // tk
        carry = (jnp.full((tq, 1), NEG, F32), jnp.zeros((tq, 1), F32), jnp.zeros((tq, 128), F32))
        carry = lax.fori_loop(0, n_full // 2, two_steps, carry)
        carry = lax.fori_loop(2 * (n_full // 2), n_full, functools.partial(step, masked=False), carry)
        m, l, acc = lax.fori_loop(n_full, n_tot, functools.partial(step, masked=True), carry)
        o_ref[...] = acc / l
        lse_ref[0] = m + jnp.log(l)
        _ride_wait(cps, (pl.program_id(0) == H - 1) & (i == nq - 1))

    x_in, x_out, x_shapes, x_scratch, x_args = _ride_parts(ride)
    res = pl.pallas_call(
        body, name=name,
        out_shape=(jax.ShapeDtypeStruct((T, H * 128), F32), jax.ShapeDtypeStruct((H, T, 1), F32), *x_shapes),
        grid=(H, nq),
        in_specs=[pl.BlockSpec((tq, 256), lambda h, i: (i, h)), pl.BlockSpec((T, 256), lambda h, i: (0, h)),
                  pl.BlockSpec((T, 128), lambda h, i: (0, h)), *x_in],
        out_specs=(pl.BlockSpec((tq, 128), lambda h, i: (i, h)), pl.BlockSpec((1, tq, 1), lambda h, i: (h, i, 0)),
                   *x_out),
        scratch_shapes=x_scratch,
        compiler_params=_cparams("arbitrary", "arbitrary"),
    )(qc, kc, v, *x_args)
    return res[0], res[1], res[2:]


def _mla_bwd(qc, kc, v, o, do, lse, *, H, scale, name, pieces=None):
    T = qc.shape[0]
    tq = tk = ATT_BLK
    nq, nk = T // tq, T // tk

    def body(*refs):
        if pieces is not None:
            (q_ref, k_ref, v_ref, o_ref, do_ref, lse_ref, p_ref, dq_ref, dk_ref, dv_ref, got_ref, delta_ref,
             send_sems, recv_sems) = refs
            cps = _scatter_copies(p_ref, got_ref, send_sems, recv_sems)

            @pl.when(pl.program_id(0) == 0)
            def _():
                for cp in cps:
                    cp.start()
        else:
            q_ref, k_ref, v_ref, o_ref, do_ref, lse_ref, dq_ref, dk_ref, dv_ref, delta_ref = refs
        dq_ref[...] = jnp.zeros_like(dq_ref)

        def fill_delta(i, c):
            off = pl.multiple_of(i * tq, tq)
            delta_ref[pl.ds(off, tq), :] = jnp.sum(do_ref[pl.ds(off, tq), :] * o_ref[pl.ds(off, tq), :],
                                                   axis=1, keepdims=True)
            return c

        lax.fori_loop(0, nq, fill_delta, 0)
        rowi = lax.broadcasted_iota(jnp.int32, (tq, tk), 0)
        coli = lax.broadcasted_iota(jnp.int32, (tq, tk), 1)

        def kblock(j, c):
            koff = pl.multiple_of(j * tk, tk)
            ks = k_ref[pl.ds(koff, tk), :]
            vs = v_ref[pl.ds(koff, tk), :]

            def qstep(i, carry, masked):
                dk, dv = carry
                qoff = pl.multiple_of(i * tq, tq)
                qs = q_ref[pl.ds(qoff, tq), :]
                dob = do_ref[pl.ds(qoff, tq), :].astype(BF16)
                s = lax.dot_general(qs, ks, NT_DIMS, preferred_element_type=F32) * scale
                if masked:
                    s = jnp.where(coli + j * tk <= rowi + i * tq, s, NEG)
                p = jnp.exp(s - lse_ref[0, pl.ds(qoff, tq), :])
                dv = dv + lax.dot_general(p.astype(BF16), dob, TN_DIMS, preferred_element_type=F32)
                dp = lax.dot_general(dob, vs, NT_DIMS, preferred_element_type=F32)
                ds = (p * (dp - delta_ref[pl.ds(qoff, tq), :]) * scale).astype(BF16)
                dk = dk + lax.dot_general(ds, qs, TN_DIMS, preferred_element_type=F32)
                dq_ref[pl.ds(qoff, tq), :] += jnp.dot(ds, ks, preferred_element_type=F32)
                return dk, dv

            i0 = (j * tk) // tq
            i1 = jnp.minimum(((j + 1) * tk + tq - 1) // tq, nq)
            carry = (jnp.zeros((tk, 256), F32), jnp.zeros((tk, 128), F32))
            carry = lax.fori_loop(i0, i1, functools.partial(qstep, masked=True), carry)
            pairs = (nq - i1) // 2
            carry = lax.fori_loop(0, pairs, lambda t, c: qstep(i1 + 2 * t + 1, qstep(i1 + 2 * t, c, False), False),
                                  carry)
            dk, dv = lax.fori_loop(i1 + 2 * pairs, nq, functools.partial(qstep, masked=False), carry)
            dk_ref[pl.ds(koff, tk), :] = dk
            dv_ref[pl.ds(koff, tk), :] = dv
            return c

        lax.fori_loop(0, nk, kblock, 0)
        if pieces is not None:
            @pl.when(pl.program_id(0) == H - 1)
            def _():
                for cp in cps:
                    cp.wait()

    wide = pl.BlockSpec((T, 256), lambda h: (0, h))
    narrow = pl.BlockSpec((T, 128), lambda h: (0, h))
    out_shape = [jax.ShapeDtypeStruct((T, H * 256), F32), jax.ShapeDtypeStruct((T, H * 256), F32),
                 jax.ShapeDtypeStruct((T, H * 128), F32)]
    in_specs = [wide, wide, narrow, narrow, narrow, pl.BlockSpec((1, T, 1), lambda h: (h, 0, 0))]
    out_specs = [wide, wide, narrow]
    scratch = [pltpu.VMEM((T, 1), F32)]
    args = [qc, kc, v, o, do, lse]
    if pieces is not None:
        out_shape.append(jax.ShapeDtypeStruct((3,) + pieces.shape[1:], pieces.dtype))
        in_specs.append(ANY)
        out_specs.append(ANY)
        scratch += [pltpu.SemaphoreType.DMA((3,)), pltpu.SemaphoreType.DMA((3,))]
        args.append(pieces)
    return pl.pallas_call(
        body, name=name,
        out_shape=tuple(out_shape),
        grid=(H,),
        in_specs=in_specs, out_specs=tuple(out_specs),
        scratch_shapes=scratch,
        compiler_params=_cparams("arbitrary"),
    )(*args)


def _log_sigmoid_pair(z):
    e = jnp.exp(-jnp.abs(z))
    lb = jnp.minimum(z, 0.0) - jnp.log(1.0 + e)
    inv = 1.0 / (1.0 + e)
    sg = jnp.where(z >= 0.0, inv, e * inv)
    return lb, lb - z, sg


def _tri_dot(x, tri):
    hi = x.astype(BF16)
    lo = (x - hi.astype(F32)).astype(BF16)
    return jnp.dot(hi, tri, preferred_element_type=F32) + jnp.dot(lo, tri, preferred_element_type=F32)


def _sb_fwd(proj, *, H, qcol0, kcol0, vcol0, scale, name, ride=None):
    T = proj.shape[0]
    tq = tk = ATT_BLK
    nq = T // tq
    qb, kb, vb = qcol0 // 128, kcol0 // 128, vcol0 // 128

    assert T // tk <= V7X_LANES

    def body(*refs):
        (q_ref, k_ref, v_ref, y_ref, rems_ref), cps = _ride_unpack(ride, refs, 3, 2)
        i = pl.program_id(1)
        _ride_start(cps, (pl.program_id(0) == 0) & (i == 0))
        q = q_ref[...].astype(BF16)
        row = i * tq + lax.broadcasted_iota(jnp.int32, (tq, tk), 0)
        col = lax.broadcasted_iota(jnp.int32, (tq, tk), 1)
        r_i = lax.broadcasted_iota(jnp.int32, (tk, tk), 0)
        c_i = lax.broadcasted_iota(jnp.int32, (tk, tk), 1)
        tri_after = (r_i > c_i).astype(BF16)
        lane = lax.broadcasted_iota(jnp.int32, (tq, V7X_LANES), 1)

        def step(j, carry, masked):
            rem, acc, tab = carry
            off = pl.multiple_of(j * tk, tk)
            ks = k_ref[pl.ds(off, tk), :].astype(BF16)
            vs = v_ref[pl.ds(off, tk), :].astype(BF16)
            z = lax.dot_general(q, ks, NT_DIMS, preferred_element_type=F32) * scale
            lb, lom, _ = _log_sigmoid_pair(z)
            if masked:
                valid = col + j * tk < row
                lom = jnp.where(valid, lom, 0.0)
            a = jnp.exp(lb + _tri_dot(lom, tri_after) + rem)
            if masked:
                a = jnp.where(valid, a, 0.0)
            acc = acc + jnp.dot(a.astype(BF16), vs, preferred_element_type=F32)
            rem = rem + jnp.sum(lom, axis=1, keepdims=True)
            return rem, acc, jnp.where(lane == j, rem, tab)

        n_full = (i * tq) // tk
        n_tot = ((i + 1) * tq + tk - 1) // tk
        carry = (jnp.zeros((tq, 1), F32), jnp.zeros((tq, 128), F32), jnp.full((tq, V7X_LANES), NEG, F32))
        carry = lax.fori_loop(0, n_tot - n_full, lambda idx, c: step(n_tot - 1 - idx, c, True), carry)

        def alive(rem):
            return (jnp.max(rem) >= SB_DEAD).astype(jnp.int32)

        def more(state):
            idx, live, _ = state
            return (idx < n_full) & (live > 0)

        def back(state):
            idx, _, c = state
            c = step(n_full - 1 - idx, c, False)
            return idx + 1, alive(c[0]), c

        _, _, (rem, acc, tab) = lax.while_loop(more, back, (jnp.int32(0), alive(carry[0]), carry))
        y_ref[...] = acc
        rems_ref[0] = tab
        _ride_wait(cps, (pl.program_id(0) == H - 1) & (i == nq - 1))

    x_in, x_out, x_shapes, x_scratch, x_args = _ride_parts(ride)
    res = pl.pallas_call(
        body, name=name,
        out_shape=(jax.ShapeDtypeStruct((T, H * 128), F32), jax.ShapeDtypeStruct((H, T, V7X_LANES), F32), *x_shapes),
        grid=(H, nq),
        in_specs=[pl.BlockSpec((tq, 128), lambda h, i: (i, qb + h)), pl.BlockSpec((T, 128), lambda h, i: (0, kb + h)),
                  pl.BlockSpec((T, 128), lambda h, i: (0, vb + h)), *x_in],
        out_specs=(pl.BlockSpec((tq, 128), lambda h, i: (i, h)),
                   pl.BlockSpec((1, tq, V7X_LANES), lambda h, i: (h, i, 0)), *x_out),
        scratch_shapes=x_scratch,
        compiler_params=_cparams("arbitrary", "arbitrary"),
    )(proj, proj, proj, *x_args)
    return res[0], res[1], res[2:]


def _sb_bwd(proj, dy, rems, *, H, qcol0, kcol0, vcol0, scale, name):
    T = proj.shape[0]
    tq = tk = ATT_BLK
    nq = T // tq
    qb, kb, vb = qcol0 // 128, kcol0 // 128, vcol0 // 128

    def body(q_ref, k_ref, v_ref, dy_ref, rems_ref, dq_ref, dk_ref, dv_ref, dk_acc, dv_acc):
        dk_acc[...] = jnp.zeros_like(dk_acc)
        dv_acc[...] = jnp.zeros_like(dv_acc)
        rowi = lax.broadcasted_iota(jnp.int32, (tq, tk), 0)
        coli = lax.broadcasted_iota(jnp.int32, (tq, tk), 1)
        r_i = lax.broadcasted_iota(jnp.int32, (tk, tk), 0)
        c_i = lax.broadcasted_iota(jnp.int32, (tk, tk), 1)
        tri_upto = (r_i <= c_i).astype(BF16)
        tri_before = (r_i < c_i).astype(BF16)
        lane = lax.broadcasted_iota(jnp.int32, (tq, V7X_LANES), 1)
        lane1 = lax.broadcasted_iota(jnp.int32, (1, V7X_LANES), 1)

        def qblock(i, c):
            qoff = pl.multiple_of(i * tq, tq)
            qs = q_ref[pl.ds(qoff, tq), :].astype(BF16)
            dyb = dy_ref[pl.ds(qoff, tq), :].astype(BF16)
            tab = rems_ref[0, pl.ds(qoff, tq), :]

            def step(j, carry, masked):
                pre, dq = carry
                rem = jnp.sum(jnp.where(lane == j, tab, 0.0), axis=1, keepdims=True)
                koff = pl.multiple_of(j * tk, tk)
                ks = k_ref[pl.ds(koff, tk), :].astype(BF16)
                vs = v_ref[pl.ds(koff, tk), :].astype(BF16)
                z = lax.dot_general(qs, ks, NT_DIMS, preferred_element_type=F32) * scale
                lb, lom, sg = _log_sigmoid_pair(z)
                if masked:
                    valid = coli + j * tk < rowi + i * tq
                    lom = jnp.where(valid, lom, 0.0)
                a = jnp.exp(lb + rem - _tri_dot(lom, tri_upto))
                if masked:
                    a = jnp.where(valid, a, 0.0)
                dv_acc[pl.ds(koff, tk), :] += lax.dot_general(a.astype(BF16), dyb, TN_DIMS,
                                                              preferred_element_type=F32)
                de = a * lax.dot_general(dyb, vs, NT_DIMS, preferred_element_type=F32)
                before = pre + _tri_dot(de, tri_before)
                dz = de * (1.0 - sg) - before * sg
                if masked:
                    dz = jnp.where(valid, dz, 0.0)
                dzb = (dz * scale).astype(BF16)
                dk_acc[pl.ds(koff, tk), :] += lax.dot_general(dzb, qs, TN_DIMS, preferred_element_type=F32)
                dq = dq + jnp.dot(dzb, ks, preferred_element_type=F32)
                return pre + jnp.sum(de, axis=1, keepdims=True), dq

            n_full = (i * tq) // tk
            n_tot = ((i + 1) * tq + tk - 1) // tk
            colmax = jnp.max(tab, axis=0, keepdims=True)
            dead = (lane1 >= 1) & (lane1 <= n_full) & (colmax < SB_DEAD)
            j0 = jnp.sum(dead.astype(jnp.int32))
            carry = (jnp.zeros((tq, 1), F32), jnp.zeros((tq, 128), F32))
            carry = lax.fori_loop(j0, n_full, functools.partial(step, masked=False), carry)
            _, dq = lax.fori_loop(n_full, n_tot, functools.partial(step, masked=True), carry)
            dq_ref[pl.ds(qoff, tq), :] = dq.astype(dq_ref.dtype)
            return c

        lax.fori_loop(0, nq, qblock, 0)
        dk_ref[...] = dk_acc[...].astype(dk_ref.dtype)
        dv_ref[...] = dv_acc[...].astype(dv_ref.dtype)

    def seg(b):
        return pl.BlockSpec((T, 128), lambda h: (0, b + h))

    out = pl.BlockSpec((T, 128), lambda h: (0, h))
    return pl.pallas_call(
        body, name=name,
        out_shape=tuple(jax.ShapeDtypeStruct((T, H * 128), BF16) for _ in range(3)),
        grid=(H,),
        in_specs=[seg(qb), seg(kb), seg(vb), out, pl.BlockSpec((1, T, V7X_LANES), lambda h: (h, 0, 0))],
        out_specs=(out, out, out),
        scratch_shapes=[pltpu.VMEM((T, 128), F32), pltpu.VMEM((T, 128), F32)],
        compiler_params=_cparams("parallel"),
    )(proj, proj, proj, dy, rems)


def _final_loss(h, g, target, *, row0, n_rows, name):
    T, D = h.shape
    tm = _row_tile(T, D, 6)
    nt = T // tm

    def body(h_ref, g_ref, t_ref, dh_ref, dg_ref, loss_ref, acc_ref, lacc_ref):
        i = pl.program_id(0)
        xv = h_ref[...]
        r = lax.rsqrt(jnp.mean(xv * xv, axis=-1, keepdims=True) + EPS)
        xh = xv * r
        gv = g_ref[...]
        rows = i * tm + lax.broadcasted_iota(jnp.int32, (tm, 1), 0)
        valid = (rows >= row0) & (rows < row0 + n_rows)
        err = jnp.where(valid, xh * gv - t_ref[...], 0.0)
        dout = err * (1.0 / D)
        dxh = dout * gv
        dh_ref[...] = r * (dxh - xh * jnp.mean(dxh * xh, axis=-1, keepdims=True))
        part = jnp.sum((dout * xh).reshape(tm // 8, 8, D), axis=0)
        lpart = jnp.sum((err * err).reshape(tm // 8, 8, D), axis=0)

        @pl.when(i == 0)
        def _():
            acc_ref[...] = part
            lacc_ref[...] = lpart

        @pl.when(i > 0)
        def _():
            acc_ref[...] += part
            lacc_ref[...] += lpart

        @pl.when(i == nt - 1)
        def _():
            dg_ref[...] = jnp.sum(acc_ref[...], axis=0, keepdims=True)
            loss_ref[...] = (0.5 / D) * jnp.sum(jnp.sum(lacc_ref[...], axis=0, keepdims=True), axis=1, keepdims=True)

    row = pl.BlockSpec((tm, D), lambda i: (i, 0))
    vec = pl.BlockSpec((1, D), lambda i: (0, 0))
    return pl.pallas_call(
        body, name=name,
        out_shape=(jax.ShapeDtypeStruct((T, D), F32), jax.ShapeDtypeStruct((1, D), F32),
                   jax.ShapeDtypeStruct((1, 1), F32)),
        grid=(nt,),
        in_specs=[row, vec, row],
        out_specs=(row, vec, pl.BlockSpec((1, 1), lambda i: (0, 0))),
        scratch_shapes=[pltpu.VMEM((8, D), F32), pltpu.VMEM((8, D), F32)],
        compiler_params=_cparams("arbitrary"),
    )(h, g.reshape(1, D), target)


def _elementwise(fn, args, out_dtypes, name):
    shape = args[0].shape
    C = shape[-1]
    R = math.prod(shape[:-1])
    n = len(args) + len(out_dtypes)
    cap = max(16, (VMEM_TILE_BUDGET // 2) // (2 * n * C * 4))
    tr = _divisor_tile(R, cap, 16)
    n_in = len(args)

    def body(*refs):
        outs = fn(*[r[...] for r in refs[:n_in]])
        for o_ref, val in zip(refs[n_in:], outs):
            o_ref[...] = val.astype(o_ref.dtype)

    spec = pl.BlockSpec((tr, C), lambda i: (i, 0))
    res = pl.pallas_call(
        body, name=name,
        out_shape=tuple(jax.ShapeDtypeStruct((R, C), dt) for dt in out_dtypes),
        grid=(R // tr,),
        in_specs=[spec] * n_in, out_specs=tuple([spec] * len(out_dtypes)),
        compiler_params=_cparams("parallel"),
    )(*[a.reshape(R, C) for a in args])
    return tuple(r.reshape(shape) for r in res)


def _adamw_math(w, g, m, v):
    m = ADAM_B1 * m + (1.0 - ADAM_B1) * g
    v = ADAM_B2 * v + (1.0 - ADAM_B2) * (g * g)
    m_hat = m / (1.0 - ADAM_B1 ** ADAM_STEP)
    v_hat = v / (1.0 - ADAM_B2 ** ADAM_STEP)
    delta = -ADAM_LR * (m_hat / (jnp.sqrt(v_hat) + ADAM_EPS) + ADAM_WD * w)
    return delta, m, v


def _adamw(w, g, m, v, name):
    return _elementwise(_adamw_math, [w, g, m, v], [F32, F32, F32], name)


ANY = pl.BlockSpec(memory_space=pl.ANY)


def _position():
    return lax.axis_index("x"), lax.axis_index("y"), lax.axis_index("c")


class _Ride:
    def __init__(self, ins, out_shapes, n, make):
        self.ins, self.out_shapes, self.n, self.make = list(ins), list(out_shapes), n, make


def _ride_parts(ride):
    if ride is None:
        return [], [], [], [], []
    sems = [pltpu.SemaphoreType.DMA((ride.n,)), pltpu.SemaphoreType.DMA((ride.n,))]
    return [ANY] * len(ride.ins), [ANY] * len(ride.out_shapes), ride.out_shapes, sems, ride.ins


def _ride_unpack(ride, refs, n_in, n_out):
    if ride is None:
        return refs, []
    a, b = len(ride.ins), len(ride.out_shapes)
    own = refs[:n_in] + refs[n_in + a:n_in + a + n_out]
    cps = ride.make(refs[n_in:n_in + a], refs[n_in + a + n_out:n_in + a + n_out + b], refs[-2], refs[-1])
    return own, cps


def _ride_start(cps, first):
    if cps:
        @pl.when(first)
        def _():
            for cp in cps:
                cp.start()


def _ride_wait(cps, last):
    if cps:
        @pl.when(last)
        def _():
            for cp in cps:
                cp.wait()


def _gather_ride_over_ici(halves):
    def make(ins, outs, send_sems, recv_sems):
        x, y, c = _position()
        chips = [(1 - x, y), (x, 1 - y), (1 - x, 1 - y)]
        return [pltpu.make_async_remote_copy(src_ref=a.at[c], dst_ref=o.at[2 * x + y],
                                             send_sem=send_sems.at[3 * t + k], recv_sem=recv_sems.at[3 * t + k],
                                             device_id=(qx, qy, c), device_id_type=MESH)
                for t, (a, o) in enumerate(zip(ins, outs)) for k, (qx, qy) in enumerate(chips)]

    shapes = [jax.ShapeDtypeStruct((4,) + a.shape[1:], a.dtype) for a in halves]
    return _Ride(halves, shapes, 3 * len(halves), make)


def _gather_ride_to_sibling(landed):
    def make(ins, outs, send_sems, recv_sems):
        x, y, c = _position()
        chips = [(1 - x, y), (x, 1 - y), (1 - x, 1 - y)]
        return [pltpu.make_async_remote_copy(src_ref=a.at[2 * qx + qy], dst_ref=o.at[2 * qx + qy],
                                             send_sem=send_sems.at[3 * t + k], recv_sem=recv_sems.at[3 * t + k],
                                             device_id=(x, y, 1 - c), device_id_type=MESH)
                for t, (a, o) in enumerate(zip(ins, outs)) for k, (qx, qy) in enumerate(chips)]

    shapes = [jax.ShapeDtypeStruct(a.shape, a.dtype) for a in landed]
    return _Ride(landed, shapes, 3 * len(landed), make)


def _assemble_shards(own, mine, other):
    x, y, c = _position()
    p = 2 * x + y
    out = []
    for q in range(4):
        full = jnp.where(c == 0, jnp.concatenate([mine[q], other[q]], axis=0),
                         jnp.concatenate([other[q], mine[q]], axis=0))
        out.append(jnp.where(p == q, own, full))
    return out


def _gather_shards(a, name):
    def body(a_ref, o_ref, send_sems, recv_sems):
        x, y, c = _position()
        p = 2 * x + y
        chips = [(1 - x, y), (x, 1 - y), (1 - x, 1 - y)]

        def copy(k, src, dst, to):
            return pltpu.make_async_remote_copy(src_ref=src, dst_ref=dst, send_sem=send_sems.at[k],
                                                recv_sem=recv_sems.at[k], device_id=to, device_id_type=MESH)

        first = [copy(k, a_ref.at[c], o_ref.at[p, c], (qx, qy, c)) for k, (qx, qy) in enumerate(chips)]
        for cp in first:
            cp.start()
        passed = []
        for k, (qx, qy) in enumerate(chips):
            land = o_ref.at[2 * qx + qy, c]
            copy(k, land, land, (x, y, c)).wait_recv()
            fwd = copy(3 + k, land, land, (x, y, 1 - c))
            fwd.start()
            passed.append(fwd)
        for k, (qx, qy) in enumerate(chips):
            land = o_ref.at[2 * qx + qy, 1 - c]
            copy(3 + k, land, land, (x, y, c)).wait_recv()
        for cp in first + passed:
            cp.wait_send()

    return pl.pallas_call(
        body, name=name,
        out_shape=jax.ShapeDtypeStruct((4,) + a.shape, a.dtype),
        in_specs=[ANY], out_specs=ANY,
        scratch_shapes=[pltpu.SemaphoreType.DMA((6,)), pltpu.SemaphoreType.DMA((6,))],
    )(a)


def _with_own(gathered, own):
    x, y, _ = _position()
    p = 2 * x + y
    return [jnp.where(p == q, own, gathered[q]) for q in range(4)]


def _swap_halves(g, name):
    def body(g_ref, o_ref, send_sem, recv_sem):
        x, y, c = _position()
        cp = pltpu.make_async_remote_copy(src_ref=g_ref.at[1 - c], dst_ref=o_ref, send_sem=send_sem,
                                          recv_sem=recv_sem, device_id=(x, y, 1 - c), device_id_type=MESH)
        cp.start()
        cp.wait()

    return pl.pallas_call(
        body, name=name,
        out_shape=jax.ShapeDtypeStruct(g.shape[1:], g.dtype),
        in_specs=[ANY], out_specs=ANY,
        scratch_shapes=[pltpu.SemaphoreType.DMA, pltpu.SemaphoreType.DMA],
    )(g)


def _scatter_copies(p_ref, o_ref, send_sems, recv_sems):
    x, y, c = _position()
    chips = [(1 - x, y), (x, 1 - y), (1 - x, 1 - y)]
    return [pltpu.make_async_remote_copy(src_ref=p_ref.at[2 * qx + qy], dst_ref=o_ref.at[k],
                                         send_sem=send_sems.at[k], recv_sem=recv_sems.at[k],
                                         device_id=(qx, qy, c), device_id_type=MESH)
            for k, (qx, qy) in enumerate(chips)]


def _scatter_to_chips(pb, name):
    def body(p_ref, o_ref, send_sems, recv_sems):
        cps = _scatter_copies(p_ref, o_ref, send_sems, recv_sems)
        for cp in cps:
            cp.start()
        for cp in cps:
            cp.wait()

    return pl.pallas_call(
        body, name=name,
        out_shape=jax.ShapeDtypeStruct((3,) + pb.shape[1:], pb.dtype),
        in_specs=[ANY], out_specs=ANY,
        scratch_shapes=[pltpu.SemaphoreType.DMA((3,)), pltpu.SemaphoreType.DMA((3,))],
    )(pb)


def _join_halves(r, name):
    def body(r_ref, o_ref, send_sem, recv_sem):
        x, y, c = _position()
        cp = pltpu.make_async_remote_copy(src_ref=r_ref, dst_ref=o_ref, send_sem=send_sem,
                                          recv_sem=recv_sem, device_id=(x, y, 1 - c), device_id_type=MESH)
        cp.start()
        cp.wait()

    other = pl.pallas_call(
        body, name=name,
        out_shape=jax.ShapeDtypeStruct(r.shape, r.dtype),
        in_specs=[ANY], out_specs=ANY,
        scratch_shapes=[pltpu.SemaphoreType.DMA, pltpu.SemaphoreType.DMA],
    )(r)
    c = lax.axis_index("c")
    return jnp.stack([jnp.where(c == 0, r, other), jnp.where(c == 0, other, r)], axis=0)


def _reduce_to_shard(gh, tag):
    x, y, c = _position()
    p = 2 * x + y
    sib = _swap_halves(gh, f"swap_{tag}")
    mine = lax.dynamic_index_in_dim(gh, c, 0, keepdims=False)
    psum, pb = _elementwise(lambda a, b: (a + b, a + b), [mine, sib], [F32, BF16], f"pairsum_{tag}")
    got = _scatter_to_chips(pb, f"scatter_{tag}")
    own = lax.dynamic_index_in_dim(psum, p, 0, keepdims=False)
    (red,) = _elementwise(lambda o, a, b, d: (((o + a.astype(F32)) + b.astype(F32)) + d.astype(F32),),
                          [own, got[0], got[1], got[2]], [F32], f"chipsum_{tag}")
    return _join_halves(red, f"join_{tag}")


def _swap_rows(g, name):
    K, N = g.shape
    Kh = K // 2

    def body(g_ref, o_ref, send_sem, recv_sem):
        x, y, c = _position()
        theirs = g_ref.at[pl.ds(pl.multiple_of((1 - c) * Kh, 8), Kh)]
        cp = pltpu.make_async_remote_copy(src_ref=theirs, dst_ref=o_ref, send_sem=send_sem,
                                          recv_sem=recv_sem, device_id=(x, y, 1 - c), device_id_type=MESH)
        cp.start()
        cp.wait()

    return pl.pallas_call(
        body, name=name,
        out_shape=jax.ShapeDtypeStruct((Kh, N), g.dtype),
        in_specs=[ANY], out_specs=ANY,
        scratch_shapes=[pltpu.SemaphoreType.DMA, pltpu.SemaphoreType.DMA],
    )(g)


def _pairsum_rows(g, sib, name):
    K, N = g.shape
    Kh = K // 2
    cap = max(16, (VMEM_TILE_BUDGET // 2) // (2 * 4 * N * 4))
    tr = _divisor_tile(Kh, cap, 16)
    nb = Kh // tr

    def body(c_ref, g_ref, s_ref, p_ref, pb_ref):
        s = g_ref[...] + s_ref[...]
        p_ref[...] = s
        pb_ref[...] = s.astype(BF16)

    mine = pl.BlockSpec((tr, N), lambda i, c_ref: (i + c_ref[0] * nb, 0))
    row = pl.BlockSpec((tr, N), lambda i, c_ref: (i, 0))
    return pl.pallas_call(
        body, name=name,
        out_shape=(jax.ShapeDtypeStruct((Kh, N), F32), jax.ShapeDtypeStruct((Kh, N), BF16)),
        grid_spec=pltpu.PrefetchScalarGridSpec(num_scalar_prefetch=1, grid=(nb,), in_specs=[mine, row],
                                               out_specs=(row, row)),
        compiler_params=_cparams("parallel"),
    )(lax.axis_index("c").astype(jnp.int32).reshape(1), g, sib)


def _reduce_rows_begin(g, tag):
    K, N = g.shape
    n = N // 4
    x, y, _ = _position()
    sib = _swap_rows(g, f"swap_{tag}")
    psum, pb = _pairsum_rows(g, sib, f"pairsum_{tag}")
    pieces = pb.reshape(K // 2, 4, n).transpose(1, 0, 2)
    own = lax.dynamic_slice_in_dim(psum, (2 * x + y) * n, n, axis=1)
    return pieces, own


def _reduce_rows_end(own, got, tag):
    (red,) = _elementwise(lambda o, a, b, d: (((o + a.astype(F32)) + b.astype(F32)) + d.astype(F32),),
                          [own, got[0], got[1], got[2]], [F32], f"chipsum_{tag}")
    return _join_halves(red, f"join_{tag}").reshape(2 * own.shape[0], own.shape[1])


def _all_reduce_small(vec, name):
    R = vec.shape[0]

    def body(v_ref, o_ref, land_ref, send_sems, recv_sems):
        x, y, c = _position()
        me = 4 * x + 2 * y + c
        land_ref[me] = v_ref[...]
        cps = []
        for r in range(1, 8):
            rx, ry, rc = (r >> 2) & 1, (r >> 1) & 1, r & 1
            to = (x ^ rx, y ^ ry, c ^ rc)
            cps.append(pltpu.make_async_remote_copy(src_ref=v_ref, dst_ref=land_ref.at[me],
                                                    send_sem=send_sems.at[r - 1], recv_sem=recv_sems.at[r - 1],
                                                    device_id=to, device_id_type=MESH))
        for cp in cps:
            cp.start()
        for cp in cps:
            cp.wait()
        total = land_ref[0]
        for d in range(1, 8):
            total = total + land_ref[d]
        o_ref[...] = total

    return pl.pallas_call(
        body, name=name,
        out_shape=jax.ShapeDtypeStruct((R, 128), F32),
        in_specs=[pl.BlockSpec(memory_space=pltpu.VMEM)], out_specs=pl.BlockSpec(memory_space=pltpu.VMEM),
        scratch_shapes=[pltpu.VMEM((8, R, 128), F32), pltpu.SemaphoreType.DMA((7,)), pltpu.SemaphoreType.DMA((7,))],
    )(vec)


def _rot(w):
    half = ROPE_DIM // 2
    return jnp.concatenate([-w[..., half:], w[..., :half]], axis=-1)


def _unrot(g):
    half = ROPE_DIM // 2
    return jnp.concatenate([g[..., half:], -g[..., :half]], axis=-1)


class _Layout:
    def __init__(self, D, QL, KVL):
        self.D, self.QL, self.KVL = D, QL, KVL
        self.H = D // 256
        self.WG = self.H * 128
        WG = self.WG
        self.z_mla, self.q_sb, self.k_sb, self.v_sb, self.z_sb = 0, WG, 2 * WG, 3 * WG, 4 * WG
        self.c_q = 5 * WG
        self.c_kv = self.c_q + QL
        self.k_r = self.c_kv + KVL
        self.width = -(-(self.k_r + 256) // 512) * 512
        self.orig = (QL, KVL, ROPE_DIM, WG, WG, WG, WG, WG)
        self.din = sum(self.orig)

    def pack_w_in(self, w):
        cuts = []
        o = 0
        for s in self.orig:
            cuts.append(w[:, o:o + s])
            o += s
        c_q, c_kv, k_r, z_mla, q_sb, k_sb, v_sb, z_sb = cuts
        z64 = jnp.zeros((w.shape[0], 128 - ROPE_DIM), w.dtype)
        pad = jnp.zeros((w.shape[0], self.width - self.k_r - 256), w.dtype)
        return jnp.concatenate([z_mla, q_sb, k_sb, v_sb, z_sb, c_q, c_kv, k_r, z64, _rot(k_r), z64, pad], axis=1)

    def pack_w_uq(self, w):
        H = self.H
        w3 = w.reshape(w.shape[0], H, 128 + ROPE_DIM)
        nope = w3[:, :, :128]
        r = w3[:, :, 128:]
        z = jnp.zeros(r.shape, w.dtype)
        a = jnp.concatenate([r, z], axis=-1)
        b = jnp.concatenate([_rot(r), z], axis=-1)
        return jnp.concatenate([nope.reshape(-1, H * 128), a.reshape(-1, H * 128), b.reshape(-1, H * 128)], axis=1)

    def unpack_dw_uq(self, g):
        H = self.H
        HB = H * 128
        nope = g[:, :HB].reshape(-1, H, 128)
        a = g[:, HB:2 * HB].reshape(-1, H, 128)[:, :, :ROPE_DIM]
        b = g[:, 2 * HB:].reshape(-1, H, 128)[:, :, :ROPE_DIM]
        return jnp.concatenate([nope, a + _unrot(b)], axis=-1).reshape(-1, H * (128 + ROPE_DIM))


def _rope_tables(T):
    inv_freq = ROPE_THETA ** (-jnp.arange(0, ROPE_DIM, 2, dtype=F32) / ROPE_DIM)
    ang = jnp.arange(T, dtype=jnp.int32).astype(F32)[:, None] * inv_freq[None, :]
    z = jnp.zeros((T, 128 - ROPE_DIM), F32)
    cos, sin = jnp.cos(ang), jnp.sin(ang)
    return jnp.concatenate([cos, cos, z], axis=1), jnp.concatenate([sin, sin, z], axis=1)


def _layer_fwd(h, wl, lay, tabs, nxt):
    g_norm, w_in, g_q, g_kv, w_uq, w_ukv, g_mla, g_sb, w_o = wl
    cosp, sinp = tabs
    H = lay.H
    u = _rms_fwd(h, g_norm, col0=0, out_dtype=BF16, name="rms_h")
    proj = _matmul(u, w_in, mode="nn", out_dtype=F32, name="mm_in")
    cqn = _rms_fwd(proj, g_q, col0=lay.c_q, out_dtype=BF16, name="rms_cq")
    ckvn = _rms_fwd(proj, g_kv, col0=lay.c_kv, out_dtype=BF16, name="rms_ckv")
    q = _matmul(cqn, w_uq, mode="nn", out_dtype=F32, name="mm_uq")
    kv = _matmul(ckvn, w_ukv, mode="nn", out_dtype=F32, name="mm_ukv")
    qc, kc, v = _rope_fwd(q, kv, proj, cosp, sinp, H=H, kr_col0=lay.k_r, name="rope_fwd")
    if nxt is None:
        o_mla, lse, _ = _mla_fwd(qc, kc, v, H=H, scale=1.0 / math.sqrt(128 + ROPE_DIM), name="mla_fwd")
        o_sb, rems, _ = _sb_fwd(proj, H=H, qcol0=lay.q_sb, kcol0=lay.k_sb, vcol0=lay.v_sb,
                                scale=1.0 / math.sqrt(128), name="sb_fwd")
        gathered = None
    else:
        o_mla, lse, mine = _mla_fwd(qc, kc, v, H=H, scale=1.0 / math.sqrt(128 + ROPE_DIM), name="mla_fwd_gather",
                                    ride=_gather_ride_over_ici(nxt))
        o_sb, rems, other = _sb_fwd(proj, H=H, qcol0=lay.q_sb, kcol0=lay.k_sb, vcol0=lay.v_sb,
                                    scale=1.0 / math.sqrt(128), name="sb_fwd_gather",
                                    ride=_gather_ride_to_sibling(mine))
        gathered = (mine, other)
    y_mla = _gate_fwd(o_mla, proj, g_mla, zcol0=lay.z_mla, name="gate_fwd_mla")
    y_sb = _gate_fwd(o_sb, proj, g_sb, zcol0=lay.z_sb, name="gate_fwd_sb")
    y = jnp.concatenate([y_mla, y_sb], axis=1)
    h_out = _matmul(y, w_o, mode="nn", out_dtype=F32, name="mm_o", residual=h)
    saved = (h, u, proj, cqn, ckvn, qc, kc, v, o_mla, lse, o_sb, rems, y)
    return h_out, saved, gathered


def _layer_bwd(dh, saved, wl, lay, tabs, carry):
    g_norm, w_in, g_q, g_kv, w_uq, w_ukv, g_mla, g_sb, w_o = wl
    h, u, proj, cqn, ckvn, qc, kc, v, o_mla, lse, o_sb, rems, y = saved
    cosp, sinp = tabs
    H = lay.H
    dy = _matmul(dh, w_o, mode="nt", out_dtype=F32, name="mm_o_dx")
    d_w_o = _matmul(y, dh, mode="tn", out_dtype=F32, name="mm_o_dw")
    do_mla, dz_mla, dg_mla = _gate_bwd(dy, o_mla, proj, g_mla, grp=0, zcol0=lay.z_mla, name="gate_bwd_mla")
    do_sb, dz_sb, dg_sb = _gate_bwd(dy, o_sb, proj, g_sb, grp=1, zcol0=lay.z_sb, name="gate_bwd_sb")
    dq_sb, dk_sb, dv_sb = _sb_bwd(proj, do_sb, rems, H=H, qcol0=lay.q_sb, kcol0=lay.k_sb, vcol0=lay.v_sb,
                                  scale=1.0 / math.sqrt(128), name="sb_bwd")
    if carry is None:
        dqc, dkc, dv = _mla_bwd(qc, kc, v, o_mla, do_mla, lse, H=H, scale=1.0 / math.sqrt(128 + ROPE_DIM),
                                name="mla_bwd")
        g_w_in_above = None
    else:
        dqc, dkc, dv, got = _mla_bwd(qc, kc, v, o_mla, do_mla, lse, H=H, scale=1.0 / math.sqrt(128 + ROPE_DIM),
                                     name="mla_bwd_scatter", pieces=carry[0])
        g_w_in_above = _reduce_rows_end(carry[1], got, "w_in")
    dq, dkv, dkr = _rope_bwd(dqc, dkc, dv, cosp, sinp, H=H, name="rope_bwd")
    d_w_uq = _matmul(cqn, dq, mode="tn", out_dtype=F32, name="mm_uq_dw")
    dcqn = _matmul(dq, w_uq, mode="nt", out_dtype=F32, name="mm_uq_dx")
    d_w_ukv = _matmul(ckvn, dkv, mode="tn", out_dtype=F32, name="mm_ukv_dw")
    dckvn = _matmul(dkv, w_ukv, mode="nt", out_dtype=F32, name="mm_ukv_dx")
    dcq, dg_q = _rms_bwd(proj, g_q, dcqn, col0=lay.c_q, out_dtype=BF16, name="rms_cq_bwd")
    dckv, dg_kv = _rms_bwd(proj, g_kv, dckvn, col0=lay.c_kv, out_dtype=BF16, name="rms_ckv_bwd")
    pad = jnp.zeros((dh.shape[0], lay.width - lay.k_r - 256), BF16)
    dproj = jnp.concatenate([dz_mla, dq_sb, dk_sb, dv_sb, dz_sb, dcq, dckv, dkr, pad], axis=1)
    du = _matmul(dproj, w_in, mode="nt", out_dtype=F32, name="mm_in_dx")
    dkr_o = (dkr[:, :ROPE_DIM].astype(F32) + _unrot(dkr[:, 128:128 + ROPE_DIM].astype(F32))).astype(BF16)
    pad_o = jnp.zeros((dh.shape[0], lay.width - lay.din), BF16)
    dproj_o = jnp.concatenate([dcq, dckv, dkr_o, dz_mla, dq_sb, dk_sb, dv_sb, dz_sb, pad_o], axis=1)
    d_w_in = _matmul(u, dproj_o, mode="tn", out_dtype=F32, name="mm_in_dw", out_cols=lay.din)
    dh_prev, dg_norm = _rms_bwd(h, g_norm, du, col0=0, out_dtype=F32, name="rms_h_bwd", residual=dh)
    grads = (dg_norm[0], None, dg_q[0], dg_kv[0], lay.unpack_dw_uq(d_w_uq), d_w_ukv, dg_mla[0], dg_sb[0], d_w_o)
    return dh_prev, grads, _reduce_rows_begin(d_w_in, "w_in"), g_w_in_above


def _halves(a):
    return a.reshape((2, a.shape[0] // 2) + a.shape[1:])


def _gather_cols(w, name):
    L, K, n = w.shape
    own = w.astype(BF16)
    g = _with_own(_gather_shards(_halves(own), name).reshape(4, L, K, n), own)
    return [jnp.concatenate([g[q][l] for q in range(4)], axis=1) for l in range(L)]


def _cut_cols(g):
    L, K, N = g.shape
    return g.reshape(2, L // 2, K, 4, N // 4).transpose(0, 3, 1, 2, 4)


def kernel(x, meta_tokens, g_norm, w_in, g_q, g_kv, w_uq, w_ukv, g_out_mla, g_out_sb, w_o, g_final, loss_target, m_meta_tokens, m_g_norm, m_w_in, m_g_q, m_g_kv, m_w_uq, m_w_ukv, m_g_out_mla, m_g_out_sb, m_w_o, m_g_final, v_meta_tokens, v_g_norm, v_w_in, v_g_q, v_g_kv, v_w_uq, v_w_ukv, v_g_out_mla, v_g_out_sb, v_w_o, v_g_final):
    _, S, D = x.shape
    NM = meta_tokens.shape[0]
    L = g_norm.shape[0]
    lay = _Layout(D, g_q.shape[1], g_kv.shape[1])
    TP = -(-(NM + S) // ROW_ALIGN) * ROW_ALIGN
    tabs = _rope_tables(TP)

    w_in_b, w_o_b = w_in.astype(BF16), w_o.astype(BF16)

    def row_halves(a):
        return a.reshape((2, a.shape[0] // 2) + a.shape[1:])

    def first_layer(w, name):
        g = _gather_shards(row_halves(w), name)
        return _with_own(g.reshape((4,) + w.shape), w)

    w_in_0 = lay.pack_w_in(jnp.concatenate(first_layer(w_in_b[0], "gather_w_in"), axis=1))
    w_o_0 = jnp.concatenate(first_layer(w_o_b[0], "gather_w_o"), axis=0)
    w_uq_full = [lay.pack_w_uq(w) for w in _gather_cols(w_uq, "gather_w_uq")]
    w_ukv_full = _gather_cols(w_ukv, "gather_w_ukv")
    meta_g = _with_own(_gather_shards(meta_tokens.reshape(2, NM // 2, -1), "gather_meta").reshape(4, NM, -1),
                       meta_tokens)
    meta_full = jnp.concatenate(meta_g, axis=1)

    h = jnp.concatenate([meta_full, x[0], jnp.zeros((TP - NM - S, D), F32)], axis=0)
    target = jnp.pad(loss_target[0], ((NM, TP - NM - S), (0, 0)))
    weights, saved = [], []
    w_in_l, w_o_l = w_in_0, w_o_0
    for l in range(L):
        weights.append((g_norm[l], w_in_l, g_q[l], g_kv[l], w_uq_full[l], w_ukv_full[l], g_out_mla[l],
                        g_out_sb[l], w_o_l))
        nxt = [row_halves(w_in_b[l + 1]), row_halves(w_o_b[l + 1])] if l + 1 < L else None
        h, s, gathered = _layer_fwd(h, weights[l], lay, tabs, nxt)
        saved.append(s)
        if gathered is not None:
            (in_mine, o_mine), (in_other, o_other) = gathered
            w_in_l = lay.pack_w_in(jnp.concatenate(_assemble_shards(w_in_b[l + 1], in_mine, in_other), axis=1))
            w_o_l = jnp.concatenate(_assemble_shards(w_o_b[l + 1], o_mine, o_other), axis=0)
    dh, dg_final, loss_part = _final_loss(h, g_final, target, row0=NM, n_rows=S, name="final_loss")

    layer_grads = [None] * L
    g_w_in_layers = [None] * L
    carry = None
    for l in reversed(range(L)):
        dh, layer_grads[l], carry, g_above = _layer_bwd(dh, saved[l], weights[l], lay, tabs, carry)
        if g_above is not None:
            g_w_in_layers[l + 1] = g_above
    g_w_in_layers[0] = _reduce_rows_end(carry[1], _scatter_to_chips(carry[0], "scatter_w_in"), "w_in")
    grad_x = dh[NM:NM + S][None]
    d_meta = dh[:NM]

    def stack(i):
        return jnp.stack([layer_grads[l][i] for l in range(L)], axis=0)

    small = [stack(0), stack(2), stack(3), stack(6), stack(7), dg_final[0]]
    flat = jnp.concatenate([s.reshape(-1) for s in small])
    n_flat = flat.shape[0]
    rows = -(-n_flat // (8 * 128)) * 8
    packed = jnp.pad(flat, (0, rows * 128 - n_flat)).reshape(rows, 128)
    summed = _all_reduce_small(packed, "allreduce_gains").reshape(-1)
    small_red = []
    o = 0
    for s in small:
        small_red.append(summed[o:o + s.size].reshape(s.shape))
        o += s.size
    g_g_norm, g_g_q, g_g_kv, g_g_mla, g_g_sb, g_g_final = small_red

    g_w_in = jnp.stack(g_w_in_layers, axis=0)
    g_w_uq = _reduce_to_shard(_cut_cols(stack(4)), "w_uq").reshape(w_uq.shape)
    g_w_ukv = _reduce_to_shard(_cut_cols(stack(5)), "w_ukv").reshape(w_ukv.shape)
    d_w_o = stack(8).reshape(2, L // 2, 4, w_o.shape[1], D).transpose(0, 2, 1, 3, 4)
    g_w_o = _reduce_to_shard(d_w_o, "w_o").reshape(w_o.shape)
    d_meta = d_meta.reshape(2, NM // 2, 4, D // 4).transpose(0, 2, 1, 3)
    g_meta = _reduce_to_shard(d_meta, "meta").reshape(meta_tokens.shape)

    loss = lax.psum(loss_part[0, 0], ("x", "y", "c"))

    names = ["meta", "g_norm", "w_in", "g_q", "g_kv", "w_uq", "w_ukv", "g_out_mla", "g_out_sb", "w_o", "g_final"]
    ws = [meta_tokens, g_norm, w_in, g_q, g_kv, w_uq, w_ukv, g_out_mla, g_out_sb, w_o, g_final]
    gs = [g_meta, g_g_norm, g_w_in, g_g_q, g_g_kv, g_w_uq, g_w_ukv, g_g_mla, g_g_sb, g_w_o, g_g_final]
    ms = [m_meta_tokens, m_g_norm, m_w_in, m_g_q, m_g_kv, m_w_uq, m_w_ukv, m_g_out_mla, m_g_out_sb, m_w_o, m_g_final]
    vs = [v_meta_tokens, v_g_norm, v_w_in, v_g_q, v_g_kv, v_w_uq, v_w_ukv, v_g_out_mla, v_g_out_sb, v_w_o, v_g_final]
    deltas, new_m, new_v = [], [], []
    for n, w, g, m, v in zip(names, ws, gs, ms, vs):
        shape = w.shape
        if w.ndim == 1:
            w, g, m, v = (a.reshape(1, -1) for a in (w, g, m, v))
        d, nm, nv = _adamw(w, g, m, v, f"adamw_{n}")
        deltas.append(d.reshape(shape))
        new_m.append(nm.reshape(shape))
        new_v.append(nv.reshape(shape))
    return (loss, grad_x, *gs, *deltas, *new_m, *new_v)
```

```python
import functools
import math

import jax
import jax.numpy as jnp
from jax import lax
from jax.experimental import pallas as pl
from jax.experimental.pallas import tpu as pltpu

F32 = jnp.float32
BF16 = jnp.bfloat16
MESH = pl.DeviceIdType.MESH

V7X_LANES = 128
VMEM_LIMIT = 56 * 1024 * 1024
VMEM_TILE_BUDGET = 40 * 1024 * 1024

ROPE_DIM = 64
ROPE_THETA = 10000.0
EPS = 1e-6
ROW_ALIGN = 384
ATT_BLK = 384
ROPE_ROWS = 128

ADAM_LR = 0.001
ADAM_B1 = 0.9
ADAM_B2 = 0.999
ADAM_EPS = 1e-08
ADAM_WD = 0.01
ADAM_STEP = 10

NEG = -1e30
SB_DEAD = -104.0
NT_DIMS = (((1,), (1,)), ((), ()))
TN_DIMS = (((0,), (0,)), ((), ()))
NN_DIMS = (((1,), (0,)), ((), ()))


def _cparams(*sem):
    return pltpu.CompilerParams(dimension_semantics=sem, vmem_limit_bytes=VMEM_LIMIT)


def _divisor_tile(n, cap, align):
    best = None
    t = align
    while t <= min(n, cap):
        if n % t == 0:
            best = t
        t += align
    return best if best is not None else n


def _mm_tiles(M, N, K, a_bytes, b_bytes, o_bytes, has_res):
    best = None
    for tm in sorted({_divisor_tile(M, c, 128) for c in (1408, 1024, 704, 512, 384, 256, 128)}, reverse=True):
        for tn in sorted({_divisor_tile(N, c, 128) for c in (1024, 512, 256, 128)}, reverse=True):
            for tk in sorted({_divisor_tile(K, c, 128) for c in (4096, 2048, 1408, 1024, 704, 512, 384, 256, 128)},
                             reverse=True):
                need = 2 * (tm * tk * a_bytes + tk * tn * b_bytes + tm * tn * o_bytes)
                need += 2 * tm * tn * 4 if has_res else 0
                need += tm * tn * 4 if tk != K else 0
                need += tm * tk * 2 if a_bytes != 2 else 0
                need += tk * tn * 2 if b_bytes != 2 else 0
                need += tm * tn * 4
                if need > VMEM_TILE_BUDGET:
                    continue
                score = (tm * tn / (tm + tn), tk)
                if best is None or score > best[0]:
                    best = (score, (tm, tn, tk))
    assert best is not None, (M, N, K)
    return best[1]


def _matmul(a, b, *, mode, out_dtype, name, residual=None, out_cols=None):
    if mode == "nn":
        (M, K), N = a.shape, b.shape[1]
    elif mode == "nt":
        (M, K), N = a.shape, b.shape[0]
    else:
        (K, M), N = a.shape, b.shape[1]
    tm, tn, tk = _mm_tiles(M, N, K, a.dtype.itemsize, b.dtype.itemsize, jnp.dtype(out_dtype).itemsize,
                           residual is not None)
    nk = K // tk
    assert out_cols is None or 0 <= N - out_cols < tn
    dims = {"nn": NN_DIMS, "nt": NT_DIMS, "tn": TN_DIMS}[mode]

    def body(*refs):
        if residual is not None:
            a_ref, b_ref, r_ref, o_ref = refs[:4]
        else:
            a_ref, b_ref, o_ref = refs[:3]
            r_ref = None
        part = lax.dot_general(a_ref[...].astype(BF16), b_ref[...].astype(BF16), dims,
                               preferred_element_type=F32)
        if nk == 1:
            if r_ref is not None:
                part = part + r_ref[...]
            o_ref[...] = part.astype(o_ref.dtype)
            return
        acc_ref = refs[-1]
        k = pl.program_id(2)

        @pl.when(k == 0)
        def _():
            acc_ref[...] = part

        @pl.when(k > 0)
        def _():
            acc_ref[...] += part

        @pl.when(k == nk - 1)
        def _():
            r = acc_ref[...]
            if r_ref is not None:
                r = r + r_ref[...]
            o_ref[...] = r.astype(o_ref.dtype)

    if mode == "tn":
        a_spec = pl.BlockSpec((tk, tm), lambda i, j, k: (k, i))
    else:
        a_spec = pl.BlockSpec((tm, tk), lambda i, j, k: (i, k))
    if mode == "nt":
        b_spec = pl.BlockSpec((tn, tk), lambda i, j, k: (j, k))
    else:
        b_spec = pl.BlockSpec((tk, tn), lambda i, j, k: (k, j))
    o_spec = pl.BlockSpec((tm, tn), lambda i, j, k: (i, j))
    in_specs = [a_spec, b_spec]
    args = [a, b]
    if residual is not None:
        in_specs.append(o_spec)
        args.append(residual)
    return pl.pallas_call(
        body, name=name,
        out_shape=jax.ShapeDtypeStruct((M, N if out_cols is None else out_cols), out_dtype),
        grid=(M // tm, N // tn, nk),
        in_specs=in_specs, out_specs=o_spec,
        scratch_shapes=[pltpu.VMEM((tm, tn), F32)] if nk > 1 else [],
        compiler_params=_cparams("parallel", "parallel", "arbitrary"),
    )(*args)


def _row_tile(rows, width, n_arrays):
    cap = max(16, VMEM_TILE_BUDGET // (2 * n_arrays * width * 4))
    return _divisor_tile(rows, min(cap, 384), 16)


def _rms_fwd(x, g, *, col0, out_dtype, name):
    T = x.shape[0]
    W = g.shape[-1]
    assert col0 % W == 0
    cb = col0 // W
    tm = _row_tile(T, W, 3)

    def body(x_ref, g_ref, o_ref):
        xv = x_ref[...].astype(F32)
        r = lax.rsqrt(jnp.mean(xv * xv, axis=-1, keepdims=True) + EPS)
        o_ref[...] = ((xv * r) * g_ref[...]).astype(o_ref.dtype)

    return pl.pallas_call(
        body, name=name,
        out_shape=jax.ShapeDtypeStruct((T, W), out_dtype),
        grid=(T // tm,),
        in_specs=[pl.BlockSpec((tm, W), lambda i: (i, cb)), pl.BlockSpec((1, W), lambda i: (0, 0))],
        out_specs=pl.BlockSpec((tm, W), lambda i: (i, 0)),
        compiler_params=_cparams("parallel"),
    )(x, g.reshape(1, W))


def _rms_bwd(x, g, dy, *, col0, out_dtype, name, residual=None):
    T = x.shape[0]
    W = g.shape[-1]
    assert col0 % W == 0
    cb = col0 // W
    tm = _row_tile(T, W, 6)
    nt = T // tm

    def body(*refs):
        if residual is not None:
            x_ref, g_ref, dy_ref, r_ref, dx_ref, dg_ref, acc_ref = refs
        else:
            x_ref, g_ref, dy_ref, dx_ref, dg_ref, acc_ref = refs
            r_ref = None
        i = pl.program_id(0)
        xv = x_ref[...].astype(F32)
        r = lax.rsqrt(jnp.mean(xv * xv, axis=-1, keepdims=True) + EPS)
        xh = xv * r
        dyv = dy_ref[...].astype(F32)
        dxh = dyv * g_ref[...]
        dx = r * (dxh - xh * jnp.mean(dxh * xh, axis=-1, keepdims=True))
        if r_ref is not None:
            dx = dx + r_ref[...]
        dx_ref[...] = dx.astype(dx_ref.dtype)
        part = jnp.sum((dyv * xh).reshape(tm // 8, 8, W), axis=0)

        @pl.when(i == 0)
        def _():
            acc_ref[...] = part

        @pl.when(i > 0)
        def _():
            acc_ref[...] += part

        @pl.when(i == nt - 1)
        def _():
            dg_ref[...] = jnp.sum(acc_ref[...], axis=0, keepdims=True)

    row = pl.BlockSpec((tm, W), lambda i: (i, 0))
    in_specs = [pl.BlockSpec((tm, W), lambda i: (i, cb)), pl.BlockSpec((1, W), lambda i: (0, 0)), row]
    args = [x, g.reshape(1, W), dy]
    if residual is not None:
        in_specs.append(row)
        args.append(residual)
    return pl.pallas_call(
        body, name=name,
        out_shape=(jax.ShapeDtypeStruct((T, W), out_dtype), jax.ShapeDtypeStruct((1, W), F32)),
        grid=(nt,),
        in_specs=in_specs,
        out_specs=(row, pl.BlockSpec((1, W), lambda i: (0, 0))),
        scratch_shapes=[pltpu.VMEM((8, W), F32)],
        compiler_params=_cparams("arbitrary"),
    )(*args)


def _gate_fwd(o, proj, g, *, zcol0, name):
    T, W = o.shape
    assert zcol0 % W == 0
    zb = zcol0 // W
    tm = _row_tile(T, W, 4)

    def body(o_ref, z_ref, g_ref, y_ref):
        ov = o_ref[...]
        r = lax.rsqrt(jnp.mean(ov * ov, axis=-1, keepdims=True) + EPS)
        z = z_ref[...]
        sg = 1.0 / (1.0 + jnp.exp(-z))
        y_ref[...] = (((ov * r) * g_ref[...]) * (z * sg)).astype(y_ref.dtype)

    return pl.pallas_call(
        body, name=name,
        out_shape=jax.ShapeDtypeStruct((T, W), BF16),
        grid=(T // tm,),
        in_specs=[pl.BlockSpec((tm, W), lambda i: (i, 0)), pl.BlockSpec((tm, W), lambda i: (i, zb)),
                  pl.BlockSpec((1, W), lambda i: (0, 0))],
        out_specs=pl.BlockSpec((tm, W), lambda i: (i, 0)),
        compiler_params=_cparams("parallel"),
    )(o, proj, g.reshape(1, W))


def _gate_bwd(dy, o, proj, g, *, grp, zcol0, name):
    T, W = o.shape
    assert zcol0 % W == 0
    zb = zcol0 // W
    tm = _row_tile(T, W, 8)
    nt = T // tm

    def body(dy_ref, o_ref, z_ref, g_ref, do_ref, dz_ref, dg_ref, acc_ref):
        i = pl.program_id(0)
        ov = o_ref[...]
        r = lax.rsqrt(jnp.mean(ov * ov, axis=-1, keepdims=True) + EPS)
        xh = ov * r
        gv = g_ref[...]
        z = z_ref[...]
        sg = 1.0 / (1.0 + jnp.exp(-z))
        dyv = dy_ref[...]
        dn = dyv * (z * sg)
        dz_ref[...] = (dyv * (xh * gv) * (sg * (1.0 + z * (1.0 - sg)))).astype(dz_ref.dtype)
        dxh = dn * gv
        do_ref[...] = r * (dxh - xh * jnp.mean(dxh * xh, axis=-1, keepdims=True))
        part = jnp.sum((dn * xh).reshape(tm // 8, 8, W), axis=0)

        @pl.when(i == 0)
        def _():
            acc_ref[...] = part

        @pl.when(i > 0)
        def _():
            acc_ref[...] += part

        @pl.when(i == nt - 1)
        def _():
            dg_ref[...] = jnp.sum(acc_ref[...], axis=0, keepdims=True)

    row = pl.BlockSpec((tm, W), lambda i: (i, 0))
    return pl.pallas_call(
        body, name=name,
        out_shape=(jax.ShapeDtypeStruct((T, W), F32), jax.ShapeDtypeStruct((T, W), BF16),
                   jax.ShapeDtypeStruct((1, W), F32)),
        grid=(nt,),
        in_specs=[pl.BlockSpec((tm, W), lambda i: (i, grp)), row, pl.BlockSpec((tm, W), lambda i: (i, zb)),
                  pl.BlockSpec((1, W), lambda i: (0, 0))],
        out_specs=(row, row, pl.BlockSpec((1, W), lambda i: (0, 0))),
        scratch_shapes=[pltpu.VMEM((8, W), F32)],
        compiler_params=_cparams("arbitrary"),
    )(dy, o, proj, g.reshape(1, W))


def _rope_fwd(q, kv, proj, cosp, sinp, *, H, kr_col0, name):
    T = q.shape[0]
    HB = H * V7X_LANES
    tm = ROPE_ROWS
    krb = kr_col0 // V7X_LANES

    def body(q_ref, kv_ref, kra_ref, krb_ref, cos_ref, sin_ref, qc_ref, kc_ref, v_ref):
        cos = cos_ref[...]
        sin = sin_ref[...]
        kr = (kra_ref[...] * cos + krb_ref[...] * sin).astype(BF16)
        for h in range(H):
            lo, hi = h * 128, (h + 1) * 128
            qc_ref[:, 2 * lo:2 * lo + 128] = q_ref[:, lo:hi].astype(BF16)
            qc_ref[:, 2 * lo + 128:2 * hi] = (q_ref[:, HB + lo:HB + hi] * cos
                                              + q_ref[:, 2 * HB + lo:2 * HB + hi] * sin).astype(BF16)
            kc_ref[:, 2 * lo:2 * lo + 128] = kv_ref[:, 2 * lo:2 * lo + 128].astype(BF16)
            kc_ref[:, 2 * lo + 128:2 * hi] = kr
            v_ref[:, lo:hi] = kv_ref[:, 2 * lo + 128:2 * hi].astype(BF16)

    tab = pl.BlockSpec((tm, 128), lambda i: (i, 0))
    return pl.pallas_call(
        body, name=name,
        out_shape=(jax.ShapeDtypeStruct((T, 2 * HB), BF16), jax.ShapeDtypeStruct((T, 2 * HB), BF16),
                   jax.ShapeDtypeStruct((T, HB), BF16)),
        grid=(T // tm,),
        in_specs=[pl.BlockSpec((tm, 3 * HB), lambda i: (i, 0)), pl.BlockSpec((tm, 2 * HB), lambda i: (i, 0)),
                  pl.BlockSpec((tm, 128), lambda i: (i, krb)), pl.BlockSpec((tm, 128), lambda i: (i, krb + 1)),
                  tab, tab],
        out_specs=(pl.BlockSpec((tm, 2 * HB), lambda i: (i, 0)), pl.BlockSpec((tm, 2 * HB), lambda i: (i, 0)),
                   pl.BlockSpec((tm, HB), lambda i: (i, 0))),
        compiler_params=_cparams("parallel"),
    )(q, kv, proj, proj, cosp, sinp)


def _rope_bwd(dqc, dkc, dv, cosp, sinp, *, H, name):
    T = dqc.shape[0]
    HB = H * V7X_LANES
    tm = ROPE_ROWS

    def body(dqc_ref, dkc_ref, dv_ref, cos_ref, sin_ref, dq_ref, dkv_ref, dkr_ref):
        cos = cos_ref[...]
        sin = sin_ref[...]
        gk = jnp.zeros((tm, 128), F32)
        for h in range(H):
            lo, hi = h * 128, (h + 1) * 128
            dq_ref[:, lo:hi] = dqc_ref[:, 2 * lo:2 * lo + 128].astype(BF16)
            gq = dqc_ref[:, 2 * lo + 128:2 * hi]
            dq_ref[:, HB + lo:HB + hi] = (gq * cos).astype(BF16)
            dq_ref[:, 2 * HB + lo:2 * HB + hi] = (gq * sin).astype(BF16)
            dkv_ref[:, 2 * lo:2 * lo + 128] = dkc_ref[:, 2 * lo:2 * lo + 128].astype(BF16)
            dkv_ref[:, 2 * lo + 128:2 * hi] = dv_ref[:, lo:hi].astype(BF16)
            gk = gk + dkc_ref[:, 2 * lo + 128:2 * hi]
        dkr_ref[:, 0:128] = (gk * cos).astype(BF16)
        dkr_ref[:, 128:256] = (gk * sin).astype(BF16)

    tab = pl.BlockSpec((tm, 128), lambda i: (i, 0))
    return pl.pallas_call(
        body, name=name,
        out_shape=(jax.ShapeDtypeStruct((T, 3 * HB), BF16), jax.ShapeDtypeStruct((T, 2 * HB), BF16),
                   jax.ShapeDtypeStruct((T, 256), BF16)),
        grid=(T // tm,),
        in_specs=[pl.BlockSpec((tm, 2 * HB), lambda i: (i, 0)), pl.BlockSpec((tm, 2 * HB), lambda i: (i, 0)),
                  pl.BlockSpec((tm, HB), lambda i: (i, 0)), tab, tab],
        out_specs=(pl.BlockSpec((tm, 3 * HB), lambda i: (i, 0)), pl.BlockSpec((tm, 2 * HB), lambda i: (i, 0)),
                   pl.BlockSpec((tm, 256), lambda i: (i, 0))),
        compiler_params=_cparams("parallel"),
    )(dqc, dkc, dv, cosp, sinp)


def _mla_fwd(qc, kc, v, *, H, scale, name, ride=None):
    T = qc.shape[0]
    tq = tk = ATT_BLK
    nq = T // tq

    def body(*refs):
        (q_ref, k_ref, v_ref, o_ref, lse_ref), cps = _ride_unpack(ride, refs, 3, 2)
        i = pl.program_id(1)
        _ride_start(cps, (pl.program_id(0) == 0) & (i == 0))
        q = q_ref[...]
        row = i * tq + lax.broadcasted_iota(jnp.int32, (tq, tk), 0)
        col = lax.broadcasted_iota(jnp.int32, (tq, tk), 1)

        def step(j, carry, masked):
            m, l, acc = carry
            off = pl.multiple_of(j * tk, tk)
            ks = k_ref[pl.ds(off, tk), :]
            vs = v_ref[pl.ds(off, tk), :]
            s = lax.dot_general(q, ks, NT_DIMS, preferred_element_type=F32) * scale
            if masked:
                s = jnp.where(col + j * tk <= row, s, NEG)
            m_new = jnp.maximum(m, jnp.max(s, axis=1, keepdims=True))
            alpha = jnp.exp(m - m_new)
            p = jnp.exp(s - m_new)
            l = alpha * l + jnp.sum(p, axis=1, keepdims=True)
            acc = alpha * acc + jnp.dot(p.astype(BF16), vs, preferred_element_type=F32)
            return m_new, l, acc

        def two_steps(t, carry):
            m, l, acc = carry
            off_a = pl.multiple_of(2 * t * tk, tk)
            off_b = pl.multiple_of((2 * t + 1) * tk, tk)
            s_a = lax.dot_general(q, k_ref[pl.ds(off_a, tk), :], NT_DIMS, preferred_element_type=F32) * scale
            s_b = lax.dot_general(q, k_ref[pl.ds(off_b, tk), :], NT_DIMS, preferred_element_type=F32) * scale
            m_new = jnp.maximum(m, jnp.maximum(jnp.max(s_a, axis=1, keepdims=True),
                                               jnp.max(s_b, axis=1, keepdims=True)))
            alpha = jnp.exp(m - m_new)
            p_a = jnp.exp(s_a - m_new)
            p_b = jnp.exp(s_b - m_new)
            l = alpha * l + (jnp.sum(p_a, axis=1, keepdims=True) + jnp.sum(p_b, axis=1, keepdims=True))
            acc = alpha * acc + (jnp.dot(p_a.astype(BF16), v_ref[pl.ds(off_a, tk), :], preferred_element_type=F32)
                                 + jnp.dot(p_b.astype(BF16), v_ref[pl.ds(off_b, tk), :], preferred_element_type=F32))
            return m_new, l, acc

        n_full = (i * tq) // tk
        n_tot = ((i + 1) * tq + tk - 1) // tk
        carry = (jnp.full((tq, 1), NEG, F32), jnp.zeros((tq, 1), F32), jnp.zeros((tq, 128), F32))
        carry = lax.fori_loop(0, n_full // 2, two_steps, carry)
        carry = lax.fori_loop(2 * (n_full // 2), n_full, functools.partial(step, masked=False), carry)
        m, l, acc = lax.fori_loop(n_full, n_tot, functools.partial(step, masked=True), carry)
        o_ref[...] = acc / l
        lse_ref[0] = m + jnp.log(l)
        _ride_wait(cps, (pl.program_id(0) == H - 1) & (i == nq - 1))

    x_in, x_out, x_shapes, x_scratch, x_args = _ride_parts(ride)
    res = pl.pallas_call(
        body, name=name,
        out_shape=(jax.ShapeDtypeStruct((T, H * 128), F32), jax.ShapeDtypeStruct((H, T, 1), F32), *x_shapes),
        grid=(H, nq),
        in_specs=[pl.BlockSpec((tq, 256), lambda h, i: (i, h)), pl.BlockSpec((T, 256), lambda h, i: (0, h)),
                  pl.BlockSpec((T, 128), lambda h, i: (0, h)), *x_in],
        out_specs=(pl.BlockSpec((tq, 128), lambda h, i: (i, h)), pl.BlockSpec((1, tq, 1), lambda h, i: (h, i, 0)),
                   *x_out),
        scratch_shapes=x_scratch,
        compiler_params=_cparams("arbitrary", "arbitrary"),
    )(qc, kc, v, *x_args)
    return res[0], res[1], res[2:]


def _mla_bwd(qc, kc, v, o, do, lse, *, H, scale, name, pieces=None):
    T = qc.shape[0]
    tq = tk = ATT_BLK
    nq, nk = T // tq, T // tk

    def body(*refs):
        if pieces is not None:
            (q_ref, k_ref, v_ref, o_ref, do_ref, lse_ref, p_ref, dq_ref, dk_ref, dv_ref, got_ref, delta_ref,
             send_sems, recv_sems) = refs
            cps = _scatter_copies(p_ref, got_ref, send_sems, recv_sems)

            @pl.when(pl.program_id(0) == 0)
            def _():
                for cp in cps:
                    cp.start()
        else:
            q_ref, k_ref, v_ref, o_ref, do_ref, lse_ref, dq_ref, dk_ref, dv_ref, delta_ref = refs
        dq_ref[...] = jnp.zeros_like(dq_ref)

        def fill_delta(i, c):
            off = pl.multiple_of(i * tq, tq)
            delta_ref[pl.ds(off, tq), :] = jnp.sum(do_ref[pl.ds(off, tq), :] * o_ref[pl.ds(off, tq), :],
                                                   axis=1, keepdims=True)
            return c

        lax.fori_loop(0, nq, fill_delta, 0)
        rowi = lax.broadcasted_iota(jnp.int32, (tq, tk), 0)
        coli = lax.broadcasted_iota(jnp.int32, (tq, tk), 1)

        def kblock(j, c):
            koff = pl.multiple_of(j * tk, tk)
            ks = k_ref[pl.ds(koff, tk), :]
            vs = v_ref[pl.ds(koff, tk), :]

            def qstep(i, carry, masked):
                dk, dv = carry
                qoff = pl.multiple_of(i * tq, tq)
                qs = q_ref[pl.ds(qoff, tq), :]
                dob = do_ref[pl.ds(qoff, tq), :].astype(BF16)
                s = lax.dot_general(qs, ks, NT_DIMS, preferred_element_type=F32) * scale
                if masked:
                    s = jnp.where(coli + j * tk <= rowi + i * tq, s, NEG)
                p = jnp.exp(s - lse_ref[0, pl.ds(qoff, tq), :])
                dv = dv + lax.dot_general(p.astype(BF16), dob, TN_DIMS, preferred_element_type=F32)
                dp = lax.dot_general(dob, vs, NT_DIMS, preferred_element_type=F32)
                ds = (p * (dp - delta_ref[pl.ds(qoff, tq), :]) * scale).astype(BF16)
                dk = dk + lax.dot_general(ds, qs, TN_DIMS, preferred_element_type=F32)
                dq_ref[pl.ds(qoff, tq), :] += jnp.dot(ds, ks, preferred_element_type=F32)
                return dk, dv

            i0 = (j * tk) // tq
            i1 = jnp.minimum(((j + 1) * tk + tq - 1) // tq, nq)
            carry = (jnp.zeros((tk, 256), F32), jnp.zeros((tk, 128), F32))
            carry = lax.fori_loop(i0, i1, functools.partial(qstep, masked=True), carry)
            pairs = (nq - i1) // 2
            carry = lax.fori_loop(0, pairs, lambda t, c: qstep(i1 + 2 * t + 1, qstep(i1 + 2 * t, c, False), False),
                                  carry)
            dk, dv = lax.fori_loop(i1 + 2 * pairs, nq, functools.partial(qstep, masked=False), carry)
            dk_ref[pl.ds(koff, tk), :] = dk
            dv_ref[pl.ds(koff, tk), :] = dv
            return c

        lax.fori_loop(0, nk, kblock, 0)
        if pieces is not None:
            @pl.when(pl.program_id(0) == H - 1)
            def _():
                for cp in cps:
                    cp.wait()

    wide = pl.BlockSpec((T, 256), lambda h: (0, h))
    narrow = pl.BlockSpec((T, 128), lambda h: (0, h))
    out_shape = [jax.ShapeDtypeStruct((T, H * 256), F32), jax.ShapeDtypeStruct((T, H * 256), F32),
                 jax.ShapeDtypeStruct((T, H * 128), F32)]
    in_specs = [wide, wide, narrow, narrow, narrow, pl.BlockSpec((1, T, 1), lambda h: (h, 0, 0))]
    out_specs = [wide, wide, narrow]
    scratch = [pltpu.VMEM((T, 1), F32)]
    args = [qc, kc, v, o, do, lse]
    if pieces is not None:
        out_shape.append(jax.ShapeDtypeStruct((3,) + pieces.shape[1:], pieces.dtype))
        in_specs.append(ANY)
        out_specs.append(ANY)
        scratch += [pltpu.SemaphoreType.DMA((3,)), pltpu.SemaphoreType.DMA((3,))]
        args.append(pieces)
    return pl.pallas_call(
        body, name=name,
        out_shape=tuple(out_shape),
        grid=(H,),
        in_specs=in_specs, out_specs=tuple(out_specs),
        scratch_shapes=scratch,
        compiler_params=_cparams("arbitrary"),
    )(*args)


def _log_sigmoid_pair(z):
    e = jnp.exp(-jnp.abs(z))
    lb = jnp.minimum(z, 0.0) - jnp.log(1.0 + e)
    inv = 1.0 / (1.0 + e)
    sg = jnp.where(z >= 0.0, inv, e * inv)
    return lb, lb - z, sg


def _tri_dot(x, tri):
    hi = x.astype(BF16)
    lo = (x - hi.astype(F32)).astype(BF16)
    return jnp.dot(hi, tri, preferred_element_type=F32) + jnp.dot(lo, tri, preferred_element_type=F32)


def _sb_fwd(proj, *, H, qcol0, kcol0, vcol0, scale, name, ride=None):
    T = proj.shape[0]
    tq = tk = ATT_BLK
    nq = T // tq
    qb, kb, vb = qcol0 // 128, kcol0 // 128, vcol0 // 128

    assert T // tk <= V7X_LANES

    def body(*refs):
        (q_ref, k_ref, v_ref, y_ref, rems_ref), cps = _ride_unpack(ride, refs, 3, 2)
        i = pl.program_id(1)
        _ride_start(cps, (pl.program_id(0) == 0) & (i == 0))
        q = q_ref[...].astype(BF16)
        row = i * tq + lax.broadcasted_iota(jnp.int32, (tq, tk), 0)
        col = lax.broadcasted_iota(jnp.int32, (tq, tk), 1)
        r_i = lax.broadcasted_iota(jnp.int32, (tk, tk), 0)
        c_i = lax.broadcasted_iota(jnp.int32, (tk, tk), 1)
        tri_after = (r_i > c_i).astype(BF16)
        lane = lax.broadcasted_iota(jnp.int32, (tq, V7X_LANES), 1)

        def step(j, carry, masked):
            rem, acc, tab = carry
            off = pl.multiple_of(j * tk, tk)
            ks = k_ref[pl.ds(off, tk), :].astype(BF16)
            vs = v_ref[pl.ds(off, tk), :].astype(BF16)
            z = lax.dot_general(q, ks, NT_DIMS, preferred_element_type=F32) * scale
            lb, lom, _ = _log_sigmoid_pair(z)
            if masked:
                valid = col + j * tk < row
                lom = jnp.where(valid, lom, 0.0)
            a = jnp.exp(lb + _tri_dot(lom, tri_after) + rem)
            if masked:
                a = jnp.where(valid, a, 0.0)
            acc = acc + jnp.dot(a.astype(BF16), vs, preferred_element_type=F32)
            rem = rem + jnp.sum(lom, axis=1, keepdims=True)
            return rem, acc, jnp.where(lane == j, rem, tab)

        n_full = (i * tq) // tk
        n_tot = ((i + 1) * tq + tk - 1) // tk
        carry = (jnp.zeros((tq, 1), F32), jnp.zeros((tq, 128), F32), jnp.full((tq, V7X_LANES), NEG, F32))
        carry = lax.fori_loop(0, n_tot - n_full, lambda idx, c: step(n_tot - 1 - idx, c, True), carry)

        def alive(rem):
            return (jnp.max(rem) >= SB_DEAD).astype(jnp.int32)

        def more(state):
            idx, live, _ = state
            return (idx < n_full) & (live > 0)

        def back(state):
            idx, _, c = state
            c = step(n_full - 1 - idx, c, False)
            return idx + 1, alive(c[0]), c

        _, _, (rem, acc, tab) = lax.while_loop(more, back, (jnp.int32(0), alive(carry[0]), carry))
        y_ref[...] = acc
        rems_ref[0] = tab
        _ride_wait(cps, (pl.program_id(0) == H - 1) & (i == nq - 1))

    x_in, x_out, x_shapes, x_scratch, x_args = _ride_parts(ride)
    res = pl.pallas_call(
        body, name=name,
        out_shape=(jax.ShapeDtypeStruct((T, H * 128), F32), jax.ShapeDtypeStruct((H, T, V7X_LANES), F32), *x_shapes),
        grid=(H, nq),
        in_specs=[pl.BlockSpec((tq, 128), lambda h, i: (i, qb + h)), pl.BlockSpec((T, 128), lambda h, i: (0, kb + h)),
                  pl.BlockSpec((T, 128), lambda h, i: (0, vb + h)), *x_in],
        out_specs=(pl.BlockSpec((tq, 128), lambda h, i: (i, h)),
                   pl.BlockSpec((1, tq, V7X_LANES), lambda h, i: (h, i, 0)), *x_out),
        scratch_shapes=x_scratch,
        compiler_params=_cparams("arbitrary", "arbitrary"),
    )(proj, proj, proj, *x_args)
    return res[0], res[1], res[2:]


def _sb_bwd(proj, dy, rems, *, H, qcol0, kcol0, vcol0, scale, name, ride=None):
    T = proj.shape[0]
    tq = tk = ATT_BLK
    nq = T // tq
    qb, kb, vb = qcol0 // 128, kcol0 // 128, vcol0 // 128

    def body(*refs):
        (q_ref, k_ref, v_ref, dy_ref, rems_ref, dq_ref, dk_ref, dv_ref, dk_acc, dv_acc), cps = _ride_unpack(
            ride, refs, 5, 3)
        _ride_start(cps, pl.program_id(0) == 0)
        dk_acc[...] = jnp.zeros_like(dk_acc)
        dv_acc[...] = jnp.zeros_like(dv_acc)
        rowi = lax.broadcasted_iota(jnp.int32, (tq, tk), 0)
        coli = lax.broadcasted_iota(jnp.int32, (tq, tk), 1)
        r_i = lax.broadcasted_iota(jnp.int32, (tk, tk), 0)
        c_i = lax.broadcasted_iota(jnp.int32, (tk, tk), 1)
        tri_upto = (r_i <= c_i).astype(BF16)
        tri_before = (r_i < c_i).astype(BF16)
        lane = lax.broadcasted_iota(jnp.int32, (tq, V7X_LANES), 1)
        lane1 = lax.broadcasted_iota(jnp.int32, (1, V7X_LANES), 1)

        def qblock(i, c):
            qoff = pl.multiple_of(i * tq, tq)
            qs = q_ref[pl.ds(qoff, tq), :].astype(BF16)
            dyb = dy_ref[pl.ds(qoff, tq), :].astype(BF16)
            tab = rems_ref[0, pl.ds(qoff, tq), :]

            def step(j, carry, masked):
                pre, dq = carry
                rem = jnp.sum(jnp.where(lane == j, tab, 0.0), axis=1, keepdims=True)
                koff = pl.multiple_of(j * tk, tk)
                ks = k_ref[pl.ds(koff, tk), :].astype(BF16)
                vs = v_ref[pl.ds(koff, tk), :].astype(BF16)
                z = lax.dot_general(qs, ks, NT_DIMS, preferred_element_type=F32) * scale
                lb, lom, sg = _log_sigmoid_pair(z)
                if masked:
                    valid = coli + j * tk < rowi + i * tq
                    lom = jnp.where(valid, lom, 0.0)
                a = jnp.exp(lb + rem - _tri_dot(lom, tri_upto))
                if masked:
                    a = jnp.where(valid, a, 0.0)
                dv_acc[pl.ds(koff, tk), :] += lax.dot_general(a.astype(BF16), dyb, TN_DIMS,
                                                              preferred_element_type=F32)
                de = a * lax.dot_general(dyb, vs, NT_DIMS, preferred_element_type=F32)
                before = pre + _tri_dot(de, tri_before)
                dz = de * (1.0 - sg) - before * sg
                if masked:
                    dz = jnp.where(valid, dz, 0.0)
                dzb = (dz * scale).astype(BF16)
                dk_acc[pl.ds(koff, tk), :] += lax.dot_general(dzb, qs, TN_DIMS, preferred_element_type=F32)
                dq = dq + jnp.dot(dzb, ks, preferred_element_type=F32)
                return pre + jnp.sum(de, axis=1, keepdims=True), dq

            n_full = (i * tq) // tk
            n_tot = ((i + 1) * tq + tk - 1) // tk
            colmax = jnp.max(tab, axis=0, keepdims=True)
            dead = (lane1 >= 1) & (lane1 <= n_full) & (colmax < SB_DEAD)
            j0 = jnp.sum(dead.astype(jnp.int32))
            carry = (jnp.zeros((tq, 1), F32), jnp.zeros((tq, 128), F32))
            carry = lax.fori_loop(j0, n_full, functools.partial(step, masked=False), carry)
            _, dq = lax.fori_loop(n_full, n_tot, functools.partial(step, masked=True), carry)
            dq_ref[pl.ds(qoff, tq), :] = dq.astype(dq_ref.dtype)
            return c

        lax.fori_loop(0, nq, qblock, 0)
        dk_ref[...] = dk_acc[...].astype(dk_ref.dtype)
        dv_ref[...] = dv_acc[...].astype(dv_ref.dtype)
        _ride_wait(cps, pl.program_id(0) == H - 1)

    def seg(b):
        return pl.BlockSpec((T, 128), lambda h: (0, b + h))

    out = pl.BlockSpec((T, 128), lambda h: (0, h))
    x_in, x_out, x_shapes, x_scratch, x_args = _ride_parts(ride)
    res = pl.pallas_call(
        body, name=name,
        out_shape=(*(jax.ShapeDtypeStruct((T, H * 128), BF16) for _ in range(3)), *x_shapes),
        grid=(H,),
        in_specs=[seg(qb), seg(kb), seg(vb), out, pl.BlockSpec((1, T, V7X_LANES), lambda h: (h, 0, 0)), *x_in],
        out_specs=(out, out, out, *x_out),
        scratch_shapes=[pltpu.VMEM((T, 128), F32), pltpu.VMEM((T, 128), F32), *x_scratch],
        compiler_params=_cparams("arbitrary"),
    )(proj, proj, proj, dy, rems, *x_args)
    return res[0], res[1], res[2], res[3:]


def _final_loss(h, g, target, *, row0, n_rows, name):
    T, D = h.shape
    tm = _row_tile(T, D, 6)
    nt = T // tm

    def body(h_ref, g_ref, t_ref, dh_ref, dg_ref, loss_ref, acc_ref, lacc_ref):
        i = pl.program_id(0)
        xv = h_ref[...]
        r = lax.rsqrt(jnp.mean(xv * xv, axis=-1, keepdims=True) + EPS)
        xh = xv * r
        gv = g_ref[...]
        rows = i * tm + lax.broadcasted_iota(jnp.int32, (tm, 1), 0)
        valid = (rows >= row0) & (rows < row0 + n_rows)
        err = jnp.where(valid, xh * gv - t_ref[...], 0.0)
        dout = err * (1.0 / D)
        dxh = dout * gv
        dh_ref[...] = r * (dxh - xh * jnp.mean(dxh * xh, axis=-1, keepdims=True))
        part = jnp.sum((dout * xh).reshape(tm // 8, 8, D), axis=0)
        lpart = jnp.sum((err * err).reshape(tm // 8, 8, D), axis=0)

        @pl.when(i == 0)
        def _():
            acc_ref[...] = part
            lacc_ref[...] = lpart

        @pl.when(i > 0)
        def _():
            acc_ref[...] += part
            lacc_ref[...] += lpart

        @pl.when(i == nt - 1)
        def _():
            dg_ref[...] = jnp.sum(acc_ref[...], axis=0, keepdims=True)
            loss_ref[...] = (0.5 / D) * jnp.sum(jnp.sum(lacc_ref[...], axis=0, keepdims=True), axis=1, keepdims=True)

    row = pl.BlockSpec((tm, D), lambda i: (i, 0))
    vec = pl.BlockSpec((1, D), lambda i: (0, 0))
    return pl.pallas_call(
        body, name=name,
        out_shape=(jax.ShapeDtypeStruct((T, D), F32), jax.ShapeDtypeStruct((1, D), F32),
                   jax.ShapeDtypeStruct((1, 1), F32)),
        grid=(nt,),
        in_specs=[row, vec, row],
        out_specs=(row, vec, pl.BlockSpec((1, 1), lambda i: (0, 0))),
        scratch_shapes=[pltpu.VMEM((8, D), F32), pltpu.VMEM((8, D), F32)],
        compiler_params=_cparams("arbitrary"),
    )(h, g.reshape(1, D), target)


def _elementwise(fn, args, out_dtypes, name):
    shape = args[0].shape
    C = shape[-1]
    R = math.prod(shape[:-1])
    n = len(args) + len(out_dtypes)
    cap = max(16, (VMEM_TILE_BUDGET // 2) // (2 * n * C * 4))
    tr = _divisor_tile(R, cap, 16)
    n_in = len(args)

    def body(*refs):
        outs = fn(*[r[...] for r in refs[:n_in]])
        for o_ref, val in zip(refs[n_in:], outs):
            o_ref[...] = val.astype(o_ref.dtype)

    spec = pl.BlockSpec((tr, C), lambda i: (i, 0))
    res = pl.pallas_call(
        body, name=name,
        out_shape=tuple(jax.ShapeDtypeStruct((R, C), dt) for dt in out_dtypes),
        grid=(R // tr,),
        in_specs=[spec] * n_in, out_specs=tuple([spec] * len(out_dtypes)),
        compiler_params=_cparams("parallel"),
    )(*[a.reshape(R, C) for a in args])
    return tuple(r.reshape(shape) for r in res)


def _adamw_math(w, g, m, v):
    m = ADAM_B1 * m + (1.0 - ADAM_B1) * g
    v = ADAM_B2 * v + (1.0 - ADAM_B2) * (g * g)
    m_hat = m / (1.0 - ADAM_B1 ** ADAM_STEP)
    v_hat = v / (1.0 - ADAM_B2 ** ADAM_STEP)
    delta = -ADAM_LR * (m_hat / (jnp.sqrt(v_hat) + ADAM_EPS) + ADAM_WD * w)
    return delta, m, v


def _adamw(w, g, m, v, name):
    return _elementwise(_adamw_math, [w, g, m, v], [F32, F32, F32], name)


ANY = pl.BlockSpec(memory_space=pl.ANY)


def _position():
    return lax.axis_index("x"), lax.axis_index("y"), lax.axis_index("c")


class _Ride:
    def __init__(self, ins, out_shapes, n, make):
        self.ins, self.out_shapes, self.n, self.make = list(ins), list(out_shapes), n, make


def _ride_parts(ride):
    if ride is None:
        return [], [], [], [], []
    sems = [pltpu.SemaphoreType.DMA((ride.n,)), pltpu.SemaphoreType.DMA((ride.n,))]
    return [ANY] * len(ride.ins), [ANY] * len(ride.out_shapes), ride.out_shapes, sems, ride.ins


def _ride_unpack(ride, refs, n_in, n_out):
    if ride is None:
        return refs, []
    a, b = len(ride.ins), len(ride.out_shapes)
    own = refs[:n_in] + refs[n_in + a:n_in + a + n_out] + refs[n_in + a + n_out + b:-2]
    cps = ride.make(refs[n_in:n_in + a], refs[n_in + a + n_out:n_in + a + n_out + b], refs[-2], refs[-1])
    return own, cps


def _ride_start(cps, first):
    if cps:
        @pl.when(first)
        def _():
            for cp in cps:
                cp.start()


def _ride_wait(cps, last):
    if cps:
        @pl.when(last)
        def _():
            for cp in cps:
                cp.wait()


def _gather_ride_over_ici(halves):
    def make(ins, outs, send_sems, recv_sems):
        x, y, c = _position()
        chips = [(1 - x, y), (x, 1 - y), (1 - x, 1 - y)]
        return [pltpu.make_async_remote_copy(src_ref=a.at[c], dst_ref=o.at[2 * x + y],
                                             send_sem=send_sems.at[3 * t + k], recv_sem=recv_sems.at[3 * t + k],
                                             device_id=(qx, qy, c), device_id_type=MESH)
                for t, (a, o) in enumerate(zip(ins, outs)) for k, (qx, qy) in enumerate(chips)]

    shapes = [jax.ShapeDtypeStruct((4,) + a.shape[1:], a.dtype) for a in halves]
    return _Ride(halves, shapes, 3 * len(halves), make)


def _gather_ride_to_sibling(landed):
    def make(ins, outs, send_sems, recv_sems):
        x, y, c = _position()
        chips = [(1 - x, y), (x, 1 - y), (1 - x, 1 - y)]
        return [pltpu.make_async_remote_copy(src_ref=a.at[2 * qx + qy], dst_ref=o.at[2 * qx + qy],
                                             send_sem=send_sems.at[3 * t + k], recv_sem=recv_sems.at[3 * t + k],
                                             device_id=(x, y, 1 - c), device_id_type=MESH)
                for t, (a, o) in enumerate(zip(ins, outs)) for k, (qx, qy) in enumerate(chips)]

    shapes = [jax.ShapeDtypeStruct(a.shape, a.dtype) for a in landed]
    return _Ride(landed, shapes, 3 * len(landed), make)


def _assemble_shards(own, mine, other):
    x, y, c = _position()
    p = 2 * x + y
    out = []
    for q in range(4):
        full = jnp.where(c == 0, jnp.concatenate([mine[q], other[q]], axis=0),
                         jnp.concatenate([other[q], mine[q]], axis=0))
        out.append(jnp.where(p == q, own, full))
    return out


def _gather_shards(a, name):
    def body(a_ref, o_ref, send_sems, recv_sems):
        x, y, c = _position()
        p = 2 * x + y
        chips = [(1 - x, y), (x, 1 - y), (1 - x, 1 - y)]

        def copy(k, src, dst, to):
            return pltpu.make_async_remote_copy(src_ref=src, dst_ref=dst, send_sem=send_sems.at[k],
                                                recv_sem=recv_sems.at[k], device_id=to, device_id_type=MESH)

        first = [copy(k, a_ref.at[c], o_ref.at[p, c], (qx, qy, c)) for k, (qx, qy) in enumerate(chips)]
        for cp in first:
            cp.start()
        passed = []
        for k, (qx, qy) in enumerate(chips):
            land = o_ref.at[2 * qx + qy, c]
            copy(k, land, land, (x, y, c)).wait_recv()
            fwd = copy(3 + k, land, land, (x, y, 1 - c))
            fwd.start()
            passed.append(fwd)
        for k, (qx, qy) in enumerate(chips):
            land = o_ref.at[2 * qx + qy, 1 - c]
            copy(3 + k, land, land, (x, y, c)).wait_recv()
        for cp in first + passed:
            cp.wait_send()

    return pl.pallas_call(
        body, name=name,
        out_shape=jax.ShapeDtypeStruct((4,) + a.shape, a.dtype),
        in_specs=[ANY], out_specs=ANY,
        scratch_shapes=[pltpu.SemaphoreType.DMA((6,)), pltpu.SemaphoreType.DMA((6,))],
    )(a)


def _with_own(gathered, own):
    x, y, _ = _position()
    p = 2 * x + y
    return [jnp.where(p == q, own, gathered[q]) for q in range(4)]


def _swap_halves(g, name):
    def body(g_ref, o_ref, send_sem, recv_sem):
        x, y, c = _position()
        cp = pltpu.make_async_remote_copy(src_ref=g_ref.at[1 - c], dst_ref=o_ref, send_sem=send_sem,
                                          recv_sem=recv_sem, device_id=(x, y, 1 - c), device_id_type=MESH)
        cp.start()
        cp.wait()

    return pl.pallas_call(
        body, name=name,
        out_shape=jax.ShapeDtypeStruct(g.shape[1:], g.dtype),
        in_specs=[ANY], out_specs=ANY,
        scratch_shapes=[pltpu.SemaphoreType.DMA, pltpu.SemaphoreType.DMA],
    )(g)


def _scatter_copies(p_ref, o_ref, send_sems, recv_sems):
    x, y, c = _position()
    chips = [(1 - x, y), (x, 1 - y), (1 - x, 1 - y)]
    return [pltpu.make_async_remote_copy(src_ref=p_ref.at[2 * qx + qy], dst_ref=o_ref.at[k],
                                         send_sem=send_sems.at[k], recv_sem=recv_sems.at[k],
                                         device_id=(qx, qy, c), device_id_type=MESH)
            for k, (qx, qy) in enumerate(chips)]


def _scatter_to_chips(pb, name):
    def body(p_ref, o_ref, send_sems, recv_sems):
        cps = _scatter_copies(p_ref, o_ref, send_sems, recv_sems)
        for cp in cps:
            cp.start()
        for cp in cps:
            cp.wait()

    return pl.pallas_call(
        body, name=name,
        out_shape=jax.ShapeDtypeStruct((3,) + pb.shape[1:], pb.dtype),
        in_specs=[ANY], out_specs=ANY,
        scratch_shapes=[pltpu.SemaphoreType.DMA((3,)), pltpu.SemaphoreType.DMA((3,))],
    )(pb)


def _join_halves(r, name):
    def body(r_ref, o_ref, send_sem, recv_sem):
        x, y, c = _position()
        cp = pltpu.make_async_remote_copy(src_ref=r_ref, dst_ref=o_ref, send_sem=send_sem,
                                          recv_sem=recv_sem, device_id=(x, y, 1 - c), device_id_type=MESH)
        cp.start()
        cp.wait()

    other = pl.pallas_call(
        body, name=name,
        out_shape=jax.ShapeDtypeStruct(r.shape, r.dtype),
        in_specs=[ANY], out_specs=ANY,
        scratch_shapes=[pltpu.SemaphoreType.DMA, pltpu.SemaphoreType.DMA],
    )(r)
    c = lax.axis_index("c")
    return jnp.stack([jnp.where(c == 0, r, other), jnp.where(c == 0, other, r)], axis=0)


def _reduce_to_shard(gh, tag):
    x, y, c = _position()
    p = 2 * x + y
    sib = _swap_halves(gh, f"swap_{tag}")
    mine = lax.dynamic_index_in_dim(gh, c, 0, keepdims=False)
    psum, pb = _elementwise(lambda a, b: (a + b, a + b), [mine, sib], [F32, BF16], f"pairsum_{tag}")
    got = _scatter_to_chips(pb, f"scatter_{tag}")
    own = lax.dynamic_index_in_dim(psum, p, 0, keepdims=False)
    (red,) = _elementwise(lambda o, a, b, d: (((o + a.astype(F32)) + b.astype(F32)) + d.astype(F32),),
                          [own, got[0], got[1], got[2]], [F32], f"chipsum_{tag}")
    return _join_halves(red, f"join_{tag}")


def _swap_rows(g, name):
    K, N = g.shape
    Kh = K // 2

    def body(g_ref, o_ref, send_sem, recv_sem):
        x, y, c = _position()
        theirs = g_ref.at[pl.ds(pl.multiple_of((1 - c) * Kh, 8), Kh)]
        cp = pltpu.make_async_remote_copy(src_ref=theirs, dst_ref=o_ref, send_sem=send_sem,
                                          recv_sem=recv_sem, device_id=(x, y, 1 - c), device_id_type=MESH)
        cp.start()
        cp.wait()

    return pl.pallas_call(
        body, name=name,
        out_shape=jax.ShapeDtypeStruct((Kh, N), g.dtype),
        in_specs=[ANY], out_specs=ANY,
        scratch_shapes=[pltpu.SemaphoreType.DMA, pltpu.SemaphoreType.DMA],
    )(g)


def _swap_rows_ride(g):
    K, N = g.shape
    Kh = K // 2

    def make(ins, outs, send_sems, recv_sems):
        x, y, c = _position()
        theirs = ins[0].at[pl.ds(pl.multiple_of((1 - c) * Kh, 8), Kh)]
        return [pltpu.make_async_remote_copy(src_ref=theirs, dst_ref=outs[0], send_sem=send_sems.at[0],
                                             recv_sem=recv_sems.at[0], device_id=(x, y, 1 - c),
                                             device_id_type=MESH)]

    return _Ride([g], [jax.ShapeDtypeStruct((Kh, N), g.dtype)], 1, make)


def _pairsum_rows(g, sib, name):
    K, N = g.shape
    Kh = K // 2
    cap = max(16, (VMEM_TILE_BUDGET // 2) // (2 * 4 * N * 4))
    tr = _divisor_tile(Kh, cap, 16)
    nb = Kh // tr

    def body(c_ref, g_ref, s_ref, p_ref, pb_ref):
        s = g_ref[...] + s_ref[...]
        p_ref[...] = s
        pb_ref[...] = s.astype(BF16)

    mine = pl.BlockSpec((tr, N), lambda i, c_ref: (i + c_ref[0] * nb, 0))
    row = pl.BlockSpec((tr, N), lambda i, c_ref: (i, 0))
    return pl.pallas_call(
        body, name=name,
        out_shape=(jax.ShapeDtypeStruct((Kh, N), F32), jax.ShapeDtypeStruct((Kh, N), BF16)),
        grid_spec=pltpu.PrefetchScalarGridSpec(num_scalar_prefetch=1, grid=(nb,), in_specs=[mine, row],
                                               out_specs=(row, row)),
        compiler_params=_cparams("parallel"),
    )(lax.axis_index("c").astype(jnp.int32).reshape(1), g, sib)


def _reduce_rows_begin(g, sib, tag):
    K, N = g.shape
    n = N // 4
    x, y, _ = _position()
    psum, pb = _pairsum_rows(g, sib, f"pairsum_{tag}")
    pieces = pb.reshape(K // 2, 4, n).transpose(1, 0, 2)
    own = lax.dynamic_slice_in_dim(psum, (2 * x + y) * n, n, axis=1)
    return pieces, own


def _reduce_rows_end(own, got, tag):
    (red,) = _elementwise(lambda o, a, b, d: (((o + a.astype(F32)) + b.astype(F32)) + d.astype(F32),),
                          [own, got[0], got[1], got[2]], [F32], f"chipsum_{tag}")
    return _join_halves(red, f"join_{tag}").reshape(2 * own.shape[0], own.shape[1])


def _all_reduce_small(vec, name):
    R = vec.shape[0]

    def body(v_ref, o_ref, land_ref, send_sems, recv_sems):
        x, y, c = _position()
        me = 4 * x + 2 * y + c
        land_ref[me] = v_ref[...]
        cps = []
        for r in range(1, 8):
            rx, ry, rc = (r >> 2) & 1, (r >> 1) & 1, r & 1
            to = (x ^ rx, y ^ ry, c ^ rc)
            cps.append(pltpu.make_async_remote_copy(src_ref=v_ref, dst_ref=land_ref.at[me],
                                                    send_sem=send_sems.at[r - 1], recv_sem=recv_sems.at[r - 1],
                                                    device_id=to, device_id_type=MESH))
        for cp in cps:
            cp.start()
        for cp in cps:
            cp.wait()
        total = land_ref[0]
        for d in range(1, 8):
            total = total + land_ref[d]
        o_ref[...] = total

    return pl.pallas_call(
        body, name=name,
        out_shape=jax.ShapeDtypeStruct((R, 128), F32),
        in_specs=[pl.BlockSpec(memory_space=pltpu.VMEM)], out_specs=pl.BlockSpec(memory_space=pltpu.VMEM),
        scratch_shapes=[pltpu.VMEM((8, R, 128), F32), pltpu.SemaphoreType.DMA((7,)), pltpu.SemaphoreType.DMA((7,))],
    )(vec)


def _rot(w):
    half = ROPE_DIM // 2
    return jnp.concatenate([-w[..., half:], w[..., :half]], axis=-1)


def _unrot(g):
    half = ROPE_DIM // 2
    return jnp.concatenate([g[..., half:], -g[..., :half]], axis=-1)


class _Layout:
    def __init__(self, D, QL, KVL):
        self.D, self.QL, self.KVL = D, QL, KVL
        self.H = D // 256
        self.WG = self.H * 128
        WG = self.WG
        self.z_mla, self.q_sb, self.k_sb, self.v_sb, self.z_sb = 0, WG, 2 * WG, 3 * WG, 4 * WG
        self.c_q = 5 * WG
        self.c_kv = self.c_q + QL
        self.k_r = self.c_kv + KVL
        self.width = -(-(self.k_r + 256) // 512) * 512
        self.orig = (QL, KVL, ROPE_DIM, WG, WG, WG, WG, WG)
        self.din = sum(self.orig)

    def pack_w_in(self, w):
        cuts = []
        o = 0
        for s in self.orig:
            cuts.append(w[:, o:o + s])
            o += s
        c_q, c_kv, k_r, z_mla, q_sb, k_sb, v_sb, z_sb = cuts
        z64 = jnp.zeros((w.shape[0], 128 - ROPE_DIM), w.dtype)
        pad = jnp.zeros((w.shape[0], self.width - self.k_r - 256), w.dtype)
        return jnp.concatenate([z_mla, q_sb, k_sb, v_sb, z_sb, c_q, c_kv, k_r, z64, _rot(k_r), z64, pad], axis=1)

    def pack_w_uq(self, w):
        H = self.H
        w3 = w.reshape(w.shape[0], H, 128 + ROPE_DIM)
        nope = w3[:, :, :128]
        r = w3[:, :, 128:]
        z = jnp.zeros(r.shape, w.dtype)
        a = jnp.concatenate([r, z], axis=-1)
        b = jnp.concatenate([_rot(r), z], axis=-1)
        return jnp.concatenate([nope.reshape(-1, H * 128), a.reshape(-1, H * 128), b.reshape(-1, H * 128)], axis=1)

    def unpack_dw_uq(self, g):
        H = self.H
        HB = H * 128
        nope = g[:, :HB].reshape(-1, H, 128)
        a = g[:, HB:2 * HB].reshape(-1, H, 128)[:, :, :ROPE_DIM]
        b = g[:, 2 * HB:].reshape(-1, H, 128)[:, :, :ROPE_DIM]
        return jnp.concatenate([nope, a + _unrot(b)], axis=-1).reshape(-1, H * (128 + ROPE_DIM))


def _rope_tables(T):
    inv_freq = ROPE_THETA ** (-jnp.arange(0, ROPE_DIM, 2, dtype=F32) / ROPE_DIM)
    ang = jnp.arange(T, dtype=jnp.int32).astype(F32)[:, None] * inv_freq[None, :]
    z = jnp.zeros((T, 128 - ROPE_DIM), F32)
    cos, sin = jnp.cos(ang), jnp.sin(ang)
    return jnp.concatenate([cos, cos, z], axis=1), jnp.concatenate([sin, sin, z], axis=1)


def _layer_fwd(h, wl, lay, tabs, nxt):
    g_norm, w_in, g_q, g_kv, w_uq, w_ukv, g_mla, g_sb, w_o = wl
    cosp, sinp = tabs
    H = lay.H
    u = _rms_fwd(h, g_norm, col0=0, out_dtype=BF16, name="rms_h")
    proj = _matmul(u, w_in, mode="nn", out_dtype=F32, name="mm_in")
    cqn = _rms_fwd(proj, g_q, col0=lay.c_q, out_dtype=BF16, name="rms_cq")
    ckvn = _rms_fwd(proj, g_kv, col0=lay.c_kv, out_dtype=BF16, name="rms_ckv")
    q = _matmul(cqn, w_uq, mode="nn", out_dtype=F32, name="mm_uq")
    kv = _matmul(ckvn, w_ukv, mode="nn", out_dtype=F32, name="mm_ukv")
    qc, kc, v = _rope_fwd(q, kv, proj, cosp, sinp, H=H, kr_col0=lay.k_r, name="rope_fwd")
    if nxt is None:
        o_mla, lse, _ = _mla_fwd(qc, kc, v, H=H, scale=1.0 / math.sqrt(128 + ROPE_DIM), name="mla_fwd")
        o_sb, rems, _ = _sb_fwd(proj, H=H, qcol0=lay.q_sb, kcol0=lay.k_sb, vcol0=lay.v_sb,
                                scale=1.0 / math.sqrt(128), name="sb_fwd")
        gathered = None
    else:
        o_mla, lse, mine = _mla_fwd(qc, kc, v, H=H, scale=1.0 / math.sqrt(128 + ROPE_DIM), name="mla_fwd_gather",
                                    ride=_gather_ride_over_ici(nxt))
        o_sb, rems, other = _sb_fwd(proj, H=H, qcol0=lay.q_sb, kcol0=lay.k_sb, vcol0=lay.v_sb,
                                    scale=1.0 / math.sqrt(128), name="sb_fwd_gather",
                                    ride=_gather_ride_to_sibling(mine))
        gathered = (mine, other)
    y_mla = _gate_fwd(o_mla, proj, g_mla, zcol0=lay.z_mla, name="gate_fwd_mla")
    y_sb = _gate_fwd(o_sb, proj, g_sb, zcol0=lay.z_sb, name="gate_fwd_sb")
    y = jnp.concatenate([y_mla, y_sb], axis=1)
    h_out = _matmul(y, w_o, mode="nn", out_dtype=F32, name="mm_o", residual=h)
    saved = (h, u, proj, cqn, ckvn, qc, kc, v, o_mla, lse, o_sb, rems, y)
    return h_out, saved, gathered


def _layer_bwd(dh, saved, wl, lay, tabs, carry):
    g_norm, w_in, g_q, g_kv, w_uq, w_ukv, g_mla, g_sb, w_o = wl
    h, u, proj, cqn, ckvn, qc, kc, v, o_mla, lse, o_sb, rems, y = saved
    cosp, sinp = tabs
    H = lay.H
    dy = _matmul(dh, w_o, mode="nt", out_dtype=F32, name="mm_o_dx")
    d_w_o = _matmul(y, dh, mode="tn", out_dtype=F32, name="mm_o_dw")
    do_mla, dz_mla, dg_mla = _gate_bwd(dy, o_mla, proj, g_mla, grp=0, zcol0=lay.z_mla, name="gate_bwd_mla")
    do_sb, dz_sb, dg_sb = _gate_bwd(dy, o_sb, proj, g_sb, grp=1, zcol0=lay.z_sb, name="gate_bwd_sb")
    if carry is None:
        dq_sb, dk_sb, dv_sb, _ = _sb_bwd(proj, do_sb, rems, H=H, qcol0=lay.q_sb, kcol0=lay.k_sb, vcol0=lay.v_sb,
                                         scale=1.0 / math.sqrt(128), name="sb_bwd")
        dqc, dkc, dv = _mla_bwd(qc, kc, v, o_mla, do_mla, lse, H=H, scale=1.0 / math.sqrt(128 + ROPE_DIM),
                                name="mla_bwd")
        g_w_in_above = None
    else:
        dq_sb, dk_sb, dv_sb, (sib,) = _sb_bwd(proj, do_sb, rems, H=H, qcol0=lay.q_sb, kcol0=lay.k_sb,
                                              vcol0=lay.v_sb, scale=1.0 / math.sqrt(128), name="sb_bwd_swap",
                                              ride=_swap_rows_ride(carry))
        pieces, own = _reduce_rows_begin(carry, sib, "w_in")
        dqc, dkc, dv, got = _mla_bwd(qc, kc, v, o_mla, do_mla, lse, H=H, scale=1.0 / math.sqrt(128 + ROPE_DIM),
                                     name="mla_bwd_scatter", pieces=pieces)
        g_w_in_above = _reduce_rows_end(own, got, "w_in")
    dq, dkv, dkr = _rope_bwd(dqc, dkc, dv, cosp, sinp, H=H, name="rope_bwd")
    d_w_uq = _matmul(cqn, dq, mode="tn", out_dtype=F32, name="mm_uq_dw")
    dcqn = _matmul(dq, w_uq, mode="nt", out_dtype=F32, name="mm_uq_dx")
    d_w_ukv = _matmul(ckvn, dkv, mode="tn", out_dtype=F32, name="mm_ukv_dw")
    dckvn = _matmul(dkv, w_ukv, mode="nt", out_dtype=F32, name="mm_ukv_dx")
    dcq, dg_q = _rms_bwd(proj, g_q, dcqn, col0=lay.c_q, out_dtype=BF16, name="rms_cq_bwd")
    dckv, dg_kv = _rms_bwd(proj, g_kv, dckvn, col0=lay.c_kv, out_dtype=BF16, name="rms_ckv_bwd")
    pad = jnp.zeros((dh.shape[0], lay.width - lay.k_r - 256), BF16)
    dproj = jnp.concatenate([dz_mla, dq_sb, dk_sb, dv_sb, dz_sb, dcq, dckv, dkr, pad], axis=1)
    du = _matmul(dproj, w_in, mode="nt", out_dtype=F32, name="mm_in_dx")
    dkr_o = (dkr[:, :ROPE_DIM].astype(F32) + _unrot(dkr[:, 128:128 + ROPE_DIM].astype(F32))).astype(BF16)
    pad_o = jnp.zeros((dh.shape[0], lay.width - lay.din), BF16)
    dproj_o = jnp.concatenate([dcq, dckv, dkr_o, dz_mla, dq_sb, dk_sb, dv_sb, dz_sb, pad_o], axis=1)
    d_w_in = _matmul(u, dproj_o, mode="tn", out_dtype=F32, name="mm_in_dw", out_cols=lay.din)
    dh_prev, dg_norm = _rms_bwd(h, g_norm, du, col0=0, out_dtype=F32, name="rms_h_bwd", residual=dh)
    grads = (dg_norm[0], None, dg_q[0], dg_kv[0], lay.unpack_dw_uq(d_w_uq), d_w_ukv, dg_mla[0], dg_sb[0], d_w_o)
    return dh_prev, grads, d_w_in, g_w_in_above


def _halves(a):
    return a.reshape((2, a.shape[0] // 2) + a.shape[1:])


def _gather_cols(w, name):
    L, K, n = w.shape
    own = w.astype(BF16)
    g = _with_own(_gather_shards(_halves(own), name).reshape(4, L, K, n), own)
    return [jnp.concatenate([g[q][l] for q in range(4)], axis=1) for l in range(L)]


def _cut_cols(g):
    L, K, N = g.shape
    return g.reshape(2, L // 2, K, 4, N // 4).transpose(0, 3, 1, 2, 4)


def kernel(x, meta_tokens, g_norm, w_in, g_q, g_kv, w_uq, w_ukv, g_out_mla, g_out_sb, w_o, g_final, loss_target, m_meta_tokens, m_g_norm, m_w_in, m_g_q, m_g_kv, m_w_uq, m_w_ukv, m_g_out_mla, m_g_out_sb, m_w_o, m_g_final, v_meta_tokens, v_g_norm, v_w_in, v_g_q, v_g_kv, v_w_uq, v_w_ukv, v_g_out_mla, v_g_out_sb, v_w_o, v_g_final):
    _, S, D = x.shape
    NM = meta_tokens.shape[0]
    L = g_norm.shape[0]
    lay = _Layout(D, g_q.shape[1], g_kv.shape[1])
    TP = -(-(NM + S) // ROW_ALIGN) * ROW_ALIGN
    tabs = _rope_tables(TP)

    w_in_b, w_o_b = w_in.astype(BF16), w_o.astype(BF16)

    def row_halves(a):
        return a.reshape((2, a.shape[0] // 2) + a.shape[1:])

    def first_layer(w, name):
        g = _gather_shards(row_halves(w), name)
        return _with_own(g.reshape((4,) + w.shape), w)

    w_in_0 = lay.pack_w_in(jnp.concatenate(first_layer(w_in_b[0], "gather_w_in"), axis=1))
    w_o_0 = jnp.concatenate(first_layer(w_o_b[0], "gather_w_o"), axis=0)
    w_uq_full = [lay.pack_w_uq(w) for w in _gather_cols(w_uq, "gather_w_uq")]
    w_ukv_full = _gather_cols(w_ukv, "gather_w_ukv")
    meta_g = _with_own(_gather_shards(meta_tokens.reshape(2, NM // 2, -1), "gather_meta").reshape(4, NM, -1),
                       meta_tokens)
    meta_full = jnp.concatenate(meta_g, axis=1)

    h = jnp.concatenate([meta_full, x[0], jnp.zeros((TP - NM - S, D), F32)], axis=0)
    target = jnp.pad(loss_target[0], ((NM, TP - NM - S), (0, 0)))
    weights, saved = [], []
    w_in_l, w_o_l = w_in_0, w_o_0
    for l in range(L):
        weights.append((g_norm[l], w_in_l, g_q[l], g_kv[l], w_uq_full[l], w_ukv_full[l], g_out_mla[l],
                        g_out_sb[l], w_o_l))
        nxt = [row_halves(w_in_b[l + 1]), row_halves(w_o_b[l + 1])] if l + 1 < L else None
        h, s, gathered = _layer_fwd(h, weights[l], lay, tabs, nxt)
        saved.append(s)
        if gathered is not None:
            (in_mine, o_mine), (in_other, o_other) = gathered
            w_in_l = lay.pack_w_in(jnp.concatenate(_assemble_shards(w_in_b[l + 1], in_mine, in_other), axis=1))
            w_o_l = jnp.concatenate(_assemble_shards(w_o_b[l + 1], o_mine, o_other), axis=0)
    dh, dg_final, loss_part = _final_loss(h, g_final, target, row0=NM, n_rows=S, name="final_loss")

    layer_grads = [None] * L
    g_w_in_layers = [None] * L
    carry = None
    for l in reversed(range(L)):
        dh, layer_grads[l], carry, g_above = _layer_bwd(dh, saved[l], weights[l], lay, tabs, carry)
        if g_above is not None:
            g_w_in_layers[l + 1] = g_above
    pieces, own = _reduce_rows_begin(carry, _swap_rows(carry, "swap_w_in"), "w_in")
    g_w_in_layers[0] = _reduce_rows_end(own, _scatter_to_chips(pieces, "scatter_w_in"), "w_in")
    grad_x = dh[NM:NM + S][None]
    d_meta = dh[:NM]

    def stack(i):
        return jnp.stack([layer_grads[l][i] for l in range(L)], axis=0)

    small = [stack(0), stack(2), stack(3), stack(6), stack(7), dg_final[0]]
    flat = jnp.concatenate([s.reshape(-1) for s in small])
    n_flat = flat.shape[0]
    rows = -(-n_flat // (8 * 128)) * 8
    packed = jnp.pad(flat, (0, rows * 128 - n_flat)).reshape(rows, 128)
    summed = _all_reduce_small(packed, "allreduce_gains").reshape(-1)
    small_red = []
    o = 0
    for s in small:
        small_red.append(summed[o:o + s.size].reshape(s.shape))
        o += s.size
    g_g_norm, g_g_q, g_g_kv, g_g_mla, g_g_sb, g_g_final = small_red

    g_w_in = jnp.stack(g_w_in_layers, axis=0)
    g_w_uq = _reduce_to_shard(_cut_cols(stack(4)), "w_uq").reshape(w_uq.shape)
    g_w_ukv = _reduce_to_shard(_cut_cols(stack(5)), "w_ukv").reshape(w_ukv.shape)
    d_w_o = stack(8).reshape(2, L // 2, 4, w_o.shape[1], D).transpose(0, 2, 1, 3, 4)
    g_w_o = _reduce_to_shard(d_w_o, "w_o").reshape(w_o.shape)
    d_meta = d_meta.reshape(2, NM // 2, 4, D // 4).transpose(0, 2, 1, 3)
    g_meta = _reduce_to_shard(d_meta, "meta").reshape(meta_tokens.shape)

    loss = lax.psum(loss_part[0, 0], ("x", "y", "c"))

    names = ["meta", "g_norm", "w_in", "g_q", "g_kv", "w_uq", "w_ukv", "g_out_mla", "g_out_sb", "w_o", "g_final"]
    ws = [meta_tokens, g_norm, w_in, g_q, g_kv, w_uq, w_ukv, g_out_mla, g_out_sb, w_o, g_final]
    gs = [g_meta, g_g_norm, g_w_in, g_g_q, g_g_kv, g_w_uq, g_w_ukv, g_g_mla, g_g_sb, g_w_o, g_g_final]
    ms = [m_meta_tokens, m_g_norm, m_w_in, m_g_q, m_g_kv, m_w_uq, m_w_ukv, m_g_out_mla, m_g_out_sb, m_w_o, m_g_final]
    vs = [v_meta_tokens, v_g_norm, v_w_in, v_g_q, v_g_kv, v_w_uq, v_w_ukv, v_g_out_mla, v_g_out_sb, v_w_o, v_g_final]
    deltas, new_m, new_v = [], [], []
    for n, w, g, m, v in zip(names, ws, gs, ms, vs):
        shape = w.shape
        if w.ndim == 1:
            w, g, m, v = (a.reshape(1, -1) for a in (w, g, m, v))
        flip = w.ndim == 3 and w.shape[2] % V7X_LANES != 0
        if flip:
            w, g, m, v = (jnp.swapaxes(a, 1, 2) for a in (w, g, m, v))
        d, nm, nv = _adamw(w, g, m, v, f"adamw_{n}")
        if flip:
            d, nm, nv = (jnp.swapaxes(a, 1, 2) for a in (d, nm, nv))
        deltas.append(d.reshape(shape))
        new_m.append(nm.reshape(shape))
        new_v.append(nv.reshape(shape))
    return (loss, grad_x, *gs, *deltas, *new_m, *new_v)
```

```python
import functools
import math

import jax
import jax.numpy as jnp
from jax import lax
from jax.experimental import pallas as pl
from jax.experimental.pallas import tpu as pltpu

F32 = jnp.float32
BF16 = jnp.bfloat16
MESH = pl.DeviceIdType.MESH

V7X_LANES = 128
VMEM_LIMIT = 56 * 1024 * 1024
VMEM_TILE_BUDGET = 40 * 1024 * 1024

ROPE_DIM = 64
ROPE_THETA = 10000.0
EPS = 1e-6
ROW_ALIGN = 384
ATT_BLK = 384
ROPE_ROWS = 128

ADAM_LR = 0.001
ADAM_B1 = 0.9
ADAM_B2 = 0.999
ADAM_EPS = 1e-08
ADAM_WD = 0.01
ADAM_STEP = 10

NEG = -1e30
SB_DEAD = -104.0
NT_DIMS = (((1,), (1,)), ((), ()))
TN_DIMS = (((0,), (0,)), ((), ()))
NN_DIMS = (((1,), (0,)), ((), ()))


def _cparams(*sem):
    return pltpu.CompilerParams(dimension_semantics=sem, vmem_limit_bytes=VMEM_LIMIT)


def _divisor_tile(n, cap, align):
    best = None
    t = align
    while t <= min(n, cap):
        if n % t == 0:
            best = t
        t += align
    return best if best is not None else n


def _mm_tiles(M, N, K, a_bytes, b_bytes, o_bytes, has_res):
    best = None
    for tm in sorted({_divisor_tile(M, c, 128) for c in (1408, 1024, 704, 512, 384, 256, 128)}, reverse=True):
        for tn in sorted({_divisor_tile(N, c, 128) for c in (1024, 512, 256, 128)}, reverse=True):
            for tk in sorted({_divisor_tile(K, c, 128) for c in (4096, 2048, 1408, 1024, 704, 512, 384, 256, 128)},
                             reverse=True):
                need = 2 * (tm * tk * a_bytes + tk * tn * b_bytes + tm * tn * o_bytes)
                need += 2 * tm * tn * 4 if has_res else 0
                need += tm * tn * 4 if tk != K else 0
                need += tm * tk * 2 if a_bytes != 2 else 0
                need += tk * tn * 2 if b_bytes != 2 else 0
                need += tm * tn * 4
                if need > VMEM_TILE_BUDGET:
                    continue
                score = (tm * tn / (tm + tn), tk)
                if best is None or score > best[0]:
                    best = (score, (tm, tn, tk))
    assert best is not None, (M, N, K)
    return best[1]


def _matmul(a, b, *, mode, out_dtype, name, residual=None, out_cols=None):
    if mode == "nn":
        (M, K), N = a.shape, b.shape[1]
    elif mode == "nt":
        (M, K), N = a.shape, b.shape[0]
    else:
        (K, M), N = a.shape, b.shape[1]
    tm, tn, tk = _mm_tiles(M, N, K, a.dtype.itemsize, b.dtype.itemsize, jnp.dtype(out_dtype).itemsize,
                           residual is not None)
    nk = K // tk
    assert out_cols is None or 0 <= N - out_cols < tn
    dims = {"nn": NN_DIMS, "nt": NT_DIMS, "tn": TN_DIMS}[mode]

    def body(*refs):
        if residual is not None:
            a_ref, b_ref, r_ref, o_ref = refs[:4]
        else:
            a_ref, b_ref, o_ref = refs[:3]
            r_ref = None
        part = lax.dot_general(a_ref[...].astype(BF16), b_ref[...].astype(BF16), dims,
                               preferred_element_type=F32)
        if nk == 1:
            if r_ref is not None:
                part = part + r_ref[...]
            o_ref[...] = part.astype(o_ref.dtype)
            return
        acc_ref = refs[-1]
        k = pl.program_id(2)

        @pl.when(k == 0)
        def _():
            acc_ref[...] = part

        @pl.when(k > 0)
        def _():
            acc_ref[...] += part

        @pl.when(k == nk - 1)
        def _():
            r = acc_ref[...]
            if r_ref is not None:
                r = r + r_ref[...]
            o_ref[...] = r.astype(o_ref.dtype)

    if mode == "tn":
        a_spec = pl.BlockSpec((tk, tm), lambda i, j, k: (k, i))
    else:
        a_spec = pl.BlockSpec((tm, tk), lambda i, j, k: (i, k))
    if mode == "nt":
        b_spec = pl.BlockSpec((tn, tk), lambda i, j, k: (j, k))
    else:
        b_spec = pl.BlockSpec((tk, tn), lambda i, j, k: (k, j))
    o_spec = pl.BlockSpec((tm, tn), lambda i, j, k: (i, j))
    in_specs = [a_spec, b_spec]
    args = [a, b]
    if residual is not None:
        in_specs.append(o_spec)
        args.append(residual)
    return pl.pallas_call(
        body, name=name,
        out_shape=jax.ShapeDtypeStruct((M, N if out_cols is None else out_cols), out_dtype),
        grid=(M // tm, N // tn, nk),
        in_specs=in_specs, out_specs=o_spec,
        scratch_shapes=[pltpu.VMEM((tm, tn), F32)] if nk > 1 else [],
        compiler_params=_cparams("parallel", "parallel", "arbitrary"),
    )(*args)


def _row_tile(rows, width, n_arrays):
    cap = max(16, VMEM_TILE_BUDGET // (2 * n_arrays * width * 4))
    return _divisor_tile(rows, min(cap, 384), 16)


def _rms_fwd(x, g, *, col0, out_dtype, name):
    T = x.shape[0]
    W = g.shape[-1]
    assert col0 % W == 0
    cb = col0 // W
    tm = _row_tile(T, W, 3)

    def body(x_ref, g_ref, o_ref):
        xv = x_ref[...].astype(F32)
        r = lax.rsqrt(jnp.mean(xv * xv, axis=-1, keepdims=True) + EPS)
        o_ref[...] = ((xv * r) * g_ref[...]).astype(o_ref.dtype)

    return pl.pallas_call(
        body, name=name,
        out_shape=jax.ShapeDtypeStruct((T, W), out_dtype),
        grid=(T // tm,),
        in_specs=[pl.BlockSpec((tm, W), lambda i: (i, cb)), pl.BlockSpec((1, W), lambda i: (0, 0))],
        out_specs=pl.BlockSpec((tm, W), lambda i: (i, 0)),
        compiler_params=_cparams("parallel"),
    )(x, g.reshape(1, W))


def _rms_bwd(x, g, dy, *, col0, out_dtype, name, residual=None):
    T = x.shape[0]
    W = g.shape[-1]
    assert col0 % W == 0
    cb = col0 // W
    tm = _row_tile(T, W, 6)
    nt = T // tm

    def body(*refs):
        if residual is not None:
            x_ref, g_ref, dy_ref, r_ref, dx_ref, dg_ref, acc_ref = refs
        else:
            x_ref, g_ref, dy_ref, dx_ref, dg_ref, acc_ref = refs
            r_ref = None
        i = pl.program_id(0)
        xv = x_ref[...].astype(F32)
        r = lax.rsqrt(jnp.mean(xv * xv, axis=-1, keepdims=True) + EPS)
        xh = xv * r
        dyv = dy_ref[...].astype(F32)
        dxh = dyv * g_ref[...]
        dx = r * (dxh - xh * jnp.mean(dxh * xh, axis=-1, keepdims=True))
        if r_ref is not None:
            dx = dx + r_ref[...]
        dx_ref[...] = dx.astype(dx_ref.dtype)
        part = jnp.sum((dyv * xh).reshape(tm // 8, 8, W), axis=0)

        @pl.when(i == 0)
        def _():
            acc_ref[...] = part

        @pl.when(i > 0)
        def _():
            acc_ref[...] += part

        @pl.when(i == nt - 1)
        def _():
            dg_ref[...] = jnp.sum(acc_ref[...], axis=0, keepdims=True)

    row = pl.BlockSpec((tm, W), lambda i: (i, 0))
    in_specs = [pl.BlockSpec((tm, W), lambda i: (i, cb)), pl.BlockSpec((1, W), lambda i: (0, 0)), row]
    args = [x, g.reshape(1, W), dy]
    if residual is not None:
        in_specs.append(row)
        args.append(residual)
    return pl.pallas_call(
        body, name=name,
        out_shape=(jax.ShapeDtypeStruct((T, W), out_dtype), jax.ShapeDtypeStruct((1, W), F32)),
        grid=(nt,),
        in_specs=in_specs,
        out_specs=(row, pl.BlockSpec((1, W), lambda i: (0, 0))),
        scratch_shapes=[pltpu.VMEM((8, W), F32)],
        compiler_params=_cparams("arbitrary"),
    )(*args)


def _gate_fwd(o, proj, g, *, zcol0, name):
    T, W = o.shape
    assert zcol0 % W == 0
    zb = zcol0 // W
    tm = _row_tile(T, W, 4)

    def body(o_ref, z_ref, g_ref, y_ref):
        ov = o_ref[...]
        r = lax.rsqrt(jnp.mean(ov * ov, axis=-1, keepdims=True) + EPS)
        z = z_ref[...]
        sg = 1.0 / (1.0 + jnp.exp(-z))
        y_ref[...] = (((ov * r) * g_ref[...]) * (z * sg)).astype(y_ref.dtype)

    return pl.pallas_call(
        body, name=name,
        out_shape=jax.ShapeDtypeStruct((T, W), BF16),
        grid=(T // tm,),
        in_specs=[pl.BlockSpec((tm, W), lambda i: (i, 0)), pl.BlockSpec((tm, W), lambda i: (i, zb)),
                  pl.BlockSpec((1, W), lambda i: (0, 0))],
        out_specs=pl.BlockSpec((tm, W), lambda i: (i, 0)),
        compiler_params=_cparams("parallel"),
    )(o, proj, g.reshape(1, W))


def _gate_bwd(dy, o, proj, g, *, grp, zcol0, name):
    T, W = o.shape
    assert zcol0 % W == 0
    zb = zcol0 // W
    tm = _row_tile(T, W, 8)
    nt = T // tm

    def body(dy_ref, o_ref, z_ref, g_ref, do_ref, dz_ref, dg_ref, acc_ref):
        i = pl.program_id(0)
        ov = o_ref[...]
        r = lax.rsqrt(jnp.mean(ov * ov, axis=-1, keepdims=True) + EPS)
        xh = ov * r
        gv = g_ref[...]
        z = z_ref[...]
        sg = 1.0 / (1.0 + jnp.exp(-z))
        dyv = dy_ref[...]
        dn = dyv * (z * sg)
        dz_ref[...] = (dyv * (xh * gv) * (sg * (1.0 + z * (1.0 - sg)))).astype(dz_ref.dtype)
        dxh = dn * gv
        do_ref[...] = r * (dxh - xh * jnp.mean(dxh * xh, axis=-1, keepdims=True))
        part = jnp.sum((dn * xh).reshape(tm // 8, 8, W), axis=0)

        @pl.when(i == 0)
        def _():
            acc_ref[...] = part

        @pl.when(i > 0)
        def _():
            acc_ref[...] += part

        @pl.when(i == nt - 1)
        def _():
            dg_ref[...] = jnp.sum(acc_ref[...], axis=0, keepdims=True)

    row = pl.BlockSpec((tm, W), lambda i: (i, 0))
    return pl.pallas_call(
        body, name=name,
        out_shape=(jax.ShapeDtypeStruct((T, W), F32), jax.ShapeDtypeStruct((T, W), BF16),
                   jax.ShapeDtypeStruct((1, W), F32)),
        grid=(nt,),
        in_specs=[pl.BlockSpec((tm, W), lambda i: (i, grp)), row, pl.BlockSpec((tm, W), lambda i: (i, zb)),
                  pl.BlockSpec((1, W), lambda i: (0, 0))],
        out_specs=(row, row, pl.BlockSpec((1, W), lambda i: (0, 0))),
        scratch_shapes=[pltpu.VMEM((8, W), F32)],
        compiler_params=_cparams("arbitrary"),
    )(dy, o, proj, g.reshape(1, W))


def _rope_fwd(q, kv, proj, cosp, sinp, *, H, kr_col0, name):
    T = q.shape[0]
    HB = H * V7X_LANES
    tm = ROPE_ROWS
    krb = kr_col0 // V7X_LANES

    def body(q_ref, kv_ref, kra_ref, krb_ref, cos_ref, sin_ref, qc_ref, kc_ref, v_ref):
        cos = cos_ref[...]
        sin = sin_ref[...]
        kr = (kra_ref[...] * cos + krb_ref[...] * sin).astype(BF16)
        for h in range(H):
            lo, hi = h * 128, (h + 1) * 128
            qc_ref[:, 2 * lo:2 * lo + 128] = q_ref[:, lo:hi].astype(BF16)
            qc_ref[:, 2 * lo + 128:2 * hi] = (q_ref[:, HB + lo:HB + hi] * cos
                                              + q_ref[:, 2 * HB + lo:2 * HB + hi] * sin).astype(BF16)
            kc_ref[:, 2 * lo:2 * lo + 128] = kv_ref[:, 2 * lo:2 * lo + 128].astype(BF16)
            kc_ref[:, 2 * lo + 128:2 * hi] = kr
            v_ref[:, lo:hi] = kv_ref[:, 2 * lo + 128:2 * hi].astype(BF16)

    tab = pl.BlockSpec((tm, 128), lambda i: (i, 0))
    return pl.pallas_call(
        body, name=name,
        out_shape=(jax.ShapeDtypeStruct((T, 2 * HB), BF16), jax.ShapeDtypeStruct((T, 2 * HB), BF16),
                   jax.ShapeDtypeStruct((T, HB), BF16)),
        grid=(T // tm,),
        in_specs=[pl.BlockSpec((tm, 3 * HB), lambda i: (i, 0)), pl.BlockSpec((tm, 2 * HB), lambda i: (i, 0)),
                  pl.BlockSpec((tm, 128), lambda i: (i, krb)), pl.BlockSpec((tm, 128), lambda i: (i, krb + 1)),
                  tab, tab],
        out_specs=(pl.BlockSpec((tm, 2 * HB), lambda i: (i, 0)), pl.BlockSpec((tm, 2 * HB), lambda i: (i, 0)),
                   pl.BlockSpec((tm, HB), lambda i: (i, 0))),
        compiler_params=_cparams("parallel"),
    )(q, kv, proj, proj, cosp, sinp)


def _rope_bwd(dqc, dkc, dv, cosp, sinp, *, H, name):
    T = dqc.shape[0]
    HB = H * V7X_LANES
    tm = ROPE_ROWS

    def body(dqc_ref, dkc_ref, dv_ref, cos_ref, sin_ref, dq_ref, dkv_ref, dkr_ref):
        cos = cos_ref[...]
        sin = sin_ref[...]
        gk = jnp.zeros((tm, 128), F32)
        for h in range(H):
            lo, hi = h * 128, (h + 1) * 128
            dq_ref[:, lo:hi] = dqc_ref[:, 2 * lo:2 * lo + 128].astype(BF16)
            gq = dqc_ref[:, 2 * lo + 128:2 * hi]
            dq_ref[:, HB + lo:HB + hi] = (gq * cos).astype(BF16)
            dq_ref[:, 2 * HB + lo:2 * HB + hi] = (gq * sin).astype(BF16)
            dkv_ref[:, 2 * lo:2 * lo + 128] = dkc_ref[:, 2 * lo:2 * lo + 128].astype(BF16)
            dkv_ref[:, 2 * lo + 128:2 * hi] = dv_ref[:, lo:hi].astype(BF16)
            gk = gk + dkc_ref[:, 2 * lo + 128:2 * hi]
        dkr_ref[:, 0:128] = (gk * cos).astype(BF16)
        dkr_ref[:, 128:256] = (gk * sin).astype(BF16)

    tab = pl.BlockSpec((tm, 128), lambda i: (i, 0))
    return pl.pallas_call(
        body, name=name,
        out_shape=(jax.ShapeDtypeStruct((T, 3 * HB), BF16), jax.ShapeDtypeStruct((T, 2 * HB), BF16),
                   jax.ShapeDtypeStruct((T, 256), BF16)),
        grid=(T // tm,),
        in_specs=[pl.BlockSpec((tm, 2 * HB), lambda i: (i, 0)), pl.BlockSpec((tm, 2 * HB), lambda i: (i, 0)),
                  pl.BlockSpec((tm, HB), lambda i: (i, 0)), tab, tab],
        out_specs=(pl.BlockSpec((tm, 3 * HB), lambda i: (i, 0)), pl.BlockSpec((tm, 2 * HB), lambda i: (i, 0)),
                   pl.BlockSpec((tm, 256), lambda i: (i, 0))),
        compiler_params=_cparams("parallel"),
    )(dqc, dkc, dv, cosp, sinp)


def _mla_fwd(qc, kc, v, *, H, scale, name, ride=None):
    T = qc.shape[0]
    tq = tk = ATT_BLK
    nq = T // tq

    def body(*refs):
        (q_ref, k_ref, v_ref, o_ref, lse_ref), cps = _ride_unpack(ride, refs, 3, 2)
        i = pl.program_id(1)
        _ride_start(cps, (pl.program_id(0) == 0) & (i == 0))
        q = q_ref[...]
        row = i * tq + lax.broadcasted_iota(jnp.int32, (tq, tk), 0)
        col = lax.broadcasted_iota(jnp.int32, (tq, tk), 1)

        def step(j, carry, masked):
            m, l, acc = carry
            off = pl.multiple_of(j * tk, tk)
            ks = k_ref[pl.ds(off, tk), :]
            vs = v_ref[pl.ds(off, tk), :]
            s = lax.dot_general(q, ks, NT_DIMS, preferred_element_type=F32) * scale
            if masked:
                s = jnp.where(col + j * tk <= row, s, NEG)
            m_new = jnp.maximum(m, jnp.max(s, axis=1, keepdims=True))
            alpha = jnp.exp(m - m_new)
            p = jnp.exp(s - m_new)
            l = alpha * l + jnp.sum(p, axis=1, keepdims=True)
            acc = alpha * acc + jnp.dot(p.astype(BF16), vs, preferred_element_type=F32)
            return m_new, l, acc

        def two_steps(t, carry):
            m, l, acc = carry
            off_a = pl.multiple_of(2 * t * tk, tk)
            off_b = pl.multiple_of((2 * t + 1) * tk, tk)
            s_a = lax.dot_general(q, k_ref[pl.ds(off_a, tk), :], NT_DIMS, preferred_element_type=F32) * scale
            s_b = lax.dot_general(q, k_ref[pl.ds(off_b, tk), :], NT_DIMS, preferred_element_type=F32) * scale
            m_new = jnp.maximum(m, jnp.maximum(jnp.max(s_a, axis=1, keepdims=True),
                                               jnp.max(s_b, axis=1, keepdims=True)))
            alpha = jnp.exp(m - m_new)
            p_a = jnp.exp(s_a - m_new)
            p_b = jnp.exp(s_b - m_new)
            l = alpha * l + (jnp.sum(p_a, axis=1, keepdims=True) + jnp.sum(p_b, axis=1, keepdims=True))
            acc = alpha * acc + (jnp.dot(p_a.astype(BF16), v_ref[pl.ds(off_a, tk), :], preferred_element_type=F32)
                                 + jnp.dot(p_b.astype(BF16), v_ref[pl.ds(off_b, tk), :], preferred_element_type=F32))
            return m_new, l, acc

        n_full = (i * tq) // tk
        n_tot = ((i + 1) * tq + tk - 1) // tk
        carry = (jnp.full((tq, 1), NEG, F32), jnp.zeros((tq, 1), F32), jnp.zeros((tq, 128), F32))
        carry = lax.fori_loop(0, n_full // 2, two_steps, carry)
        carry = lax.fori_loop(2 * (n_full // 2), n_full, functools.partial(step, masked=False), carry)
        m, l, acc = lax.fori_loop(n_full, n_tot, functools.partial(step, masked=True), carry)
        o_ref[...] = acc / l
        lse_ref[0] = m + jnp.log(l)
        _ride_wait(cps, (pl.program_id(0) == H - 1) & (i == nq - 1))

    x_in, x_out, x_shapes, x_scratch, x_args = _ride_parts(ride)
    res = pl.pallas_call(
        body, name=name,
        out_shape=(jax.ShapeDtypeStruct((T, H * 128), F32), jax.ShapeDtypeStruct((H, T, 1), F32), *x_shapes),
        grid=(H, nq),
        in_specs=[pl.BlockSpec((tq, 256), lambda h, i: (i, h)), pl.BlockSpec((T, 256), lambda h, i: (0, h)),
                  pl.BlockSpec((T, 128), lambda h, i: (0, h)), *x_in],
        out_specs=(pl.BlockSpec((tq, 128), lambda h, i: (i, h)), pl.BlockSpec((1, tq, 1), lambda h, i: (h, i, 0)),
                   *x_out),
        scratch_shapes=x_scratch,
        compiler_params=_cparams("arbitrary", "arbitrary"),
    )(qc, kc, v, *x_args)
    return res[0], res[1], res[2:]


def _mla_bwd(qc, kc, v, o, do, lse, *, H, scale, name, pieces=None):
    T = qc.shape[0]
    tq = tk = ATT_BLK
    nq, nk = T // tq, T // tk

    def body(*refs):
        if pieces is not None:
            (q_ref, k_ref, v_ref, o_ref, do_ref, lse_ref, p_ref, dq_ref, dk_ref, dv_ref, got_ref, delta_ref,
             send_sems, recv_sems) = refs
            cps = _scatter_copies(p_ref, got_ref, send_sems, recv_sems)

            @pl.when(pl.program_id(0) == 0)
            def _():
                for cp in cps:
                    cp.start()
        else:
            q_ref, k_ref, v_ref, o_ref, do_ref, lse_ref, dq_ref, dk_ref, dv_ref, delta_ref = refs
        dq_ref[...] = jnp.zeros_like(dq_ref)

        def fill_delta(i, c):
            off = pl.multiple_of(i * tq, tq)
            delta_ref[pl.ds(off, tq), :] = jnp.sum(do_ref[pl.ds(off, tq), :] * o_ref[pl.ds(off, tq), :],
                                                   axis=1, keepdims=True)
            return c

        lax.fori_loop(0, nq, fill_delta, 0)
        rowi = lax.broadcasted_iota(jnp.int32, (tq, tk), 0)
        coli = lax.broadcasted_iota(jnp.int32, (tq, tk), 1)

        def kblock(j, c):
            koff = pl.multiple_of(j * tk, tk)
            ks = k_ref[pl.ds(koff, tk), :]
            vs = v_ref[pl.ds(koff, tk), :]

            def qstep(i, carry, masked):
                dk, dv = carry
                qoff = pl.multiple_of(i * tq, tq)
                qs = q_ref[pl.ds(qoff, tq), :]
                dob = do_ref[pl.ds(qoff, tq), :].astype(BF16)
                s = lax.dot_general(qs, ks, NT_DIMS, preferred_element_type=F32) * scale
                if masked:
                    s = jnp.where(coli + j * tk <= rowi + i * tq, s, NEG)
                p = jnp.exp(s - lse_ref[0, pl.ds(qoff, tq), :])
                dv = dv + lax.dot_general(p.astype(BF16), dob, TN_DIMS, preferred_element_type=F32)
                dp = lax.dot_general(dob, vs, NT_DIMS, preferred_element_type=F32)
                ds = (p * (dp - delta_ref[pl.ds(qoff, tq), :]) * scale).astype(BF16)
                dk = dk + lax.dot_general(ds, qs, TN_DIMS, preferred_element_type=F32)
                dq_ref[pl.ds(qoff, tq), :] += jnp.dot(ds, ks, preferred_element_type=F32)
                return dk, dv

            i0 = (j * tk) // tq
            i1 = jnp.minimum(((j + 1) * tk + tq - 1) // tq, nq)
            carry = (jnp.zeros((tk, 256), F32), jnp.zeros((tk, 128), F32))
            carry = lax.fori_loop(i0, i1, functools.partial(qstep, masked=True), carry)
            pairs = (nq - i1) // 2
            carry = lax.fori_loop(0, pairs, lambda t, c: qstep(i1 + 2 * t + 1, qstep(i1 + 2 * t, c, False), False),
                                  carry)
            dk, dv = lax.fori_loop(i1 + 2 * pairs, nq, functools.partial(qstep, masked=False), carry)
            dk_ref[pl.ds(koff, tk), :] = dk
            dv_ref[pl.ds(koff, tk), :] = dv
            return c

        lax.fori_loop(0, nk, kblock, 0)
        if pieces is not None:
            @pl.when(pl.program_id(0) == H - 1)
            def _():
                for cp in cps:
                    cp.wait()

    wide = pl.BlockSpec((T, 256), lambda h: (0, h))
    narrow = pl.BlockSpec((T, 128), lambda h: (0, h))
    out_shape = [jax.ShapeDtypeStruct((T, H * 256), F32), jax.ShapeDtypeStruct((T, H * 256), F32),
                 jax.ShapeDtypeStruct((T, H * 128), F32)]
    in_specs = [wide, wide, narrow, narrow, narrow, pl.BlockSpec((1, T, 1), lambda h: (h, 0, 0))]
    out_specs = [wide, wide, narrow]
    scratch = [pltpu.VMEM((T, 1), F32)]
    args = [qc, kc, v, o, do, lse]
    if pieces is not None:
        out_shape.append(jax.ShapeDtypeStruct((3,) + pieces.shape[1:], pieces.dtype))
        in_specs.append(ANY)
        out_specs.append(ANY)
        scratch += [pltpu.SemaphoreType.DMA((3,)), pltpu.SemaphoreType.DMA((3,))]
        args.append(pieces)
    return pl.pallas_call(
        body, name=name,
        out_shape=tuple(out_shape),
        grid=(H,),
        in_specs=in_specs, out_specs=tuple(out_specs),
        scratch_shapes=scratch,
        compiler_params=_cparams("arbitrary"),
    )(*args)


def _log_sigmoid_pair(z):
    e = jnp.exp(-jnp.abs(z))
    lb = jnp.minimum(z, 0.0) - jnp.log(1.0 + e)
    inv = 1.0 / (1.0 + e)
    sg = jnp.where(z >= 0.0, inv, e * inv)
    return lb, lb - z, sg


def _tri_dot(x, tri):
    hi = x.astype(BF16)
    lo = (x - hi.astype(F32)).astype(BF16)
    return jnp.dot(hi, tri, preferred_element_type=F32) + jnp.dot(lo, tri, preferred_element_type=F32)


def _sb_fwd(proj, *, H, qcol0, kcol0, vcol0, scale, name, ride=None):
    T = proj.shape[0]
    tq = tk = ATT_BLK
    nq = T // tq
    qb, kb, vb = qcol0 // 128, kcol0 // 128, vcol0 // 128

    assert T // tk <= V7X_LANES

    def body(*refs):
        (q_ref, k_ref, v_ref, y_ref, rems_ref), cps = _ride_unpack(ride, refs, 3, 2)
        i = pl.program_id(1)
        _ride_start(cps, (pl.program_id(0) == 0) & (i == 0))
        q = q_ref[...].astype(BF16)
        row = i * tq + lax.broadcasted_iota(jnp.int32, (tq, tk), 0)
        col = lax.broadcasted_iota(jnp.int32, (tq, tk), 1)
        r_i = lax.broadcasted_iota(jnp.int32, (tk, tk), 0)
        c_i = lax.broadcasted_iota(jnp.int32, (tk, tk), 1)
        tri_after = (r_i > c_i).astype(BF16)
        lane = lax.broadcasted_iota(jnp.int32, (tq, V7X_LANES), 1)

        def step(j, carry, masked):
            rem, acc, tab = carry
            off = pl.multiple_of(j * tk, tk)
            ks = k_ref[pl.ds(off, tk), :].astype(BF16)
            vs = v_ref[pl.ds(off, tk), :].astype(BF16)
            z = lax.dot_general(q, ks, NT_DIMS, preferred_element_type=F32) * scale
            lb, lom, _ = _log_sigmoid_pair(z)
            if masked:
                valid = col + j * tk < row
                lom = jnp.where(valid, lom, 0.0)
            a = jnp.exp(lb + _tri_dot(lom, tri_after) + rem)
            if masked:
                a = jnp.where(valid, a, 0.0)
            acc = acc + jnp.dot(a.astype(BF16), vs, preferred_element_type=F32)
            rem = rem + jnp.sum(lom, axis=1, keepdims=True)
            return rem, acc, jnp.where(lane == j, rem, tab)

        n_full = (i * tq) // tk
        n_tot = ((i + 1) * tq + tk - 1) // tk
        carry = (jnp.zeros((tq, 1), F32), jnp.zeros((tq, 128), F32), jnp.full((tq, V7X_LANES), NEG, F32))
        carry = lax.fori_loop(0, n_tot - n_full, lambda idx, c: step(n_tot - 1 - idx, c, True), carry)

        def alive(rem):
            return (jnp.max(rem) >= SB_DEAD).astype(jnp.int32)

        def more(state):
            idx, live, _ = state
            return (idx < n_full) & (live > 0)

        def back(state):
            idx, _, c = state
            c = step(n_full - 1 - idx, c, False)
            return idx + 1, alive(c[0]), c

        _, _, (rem, acc, tab) = lax.while_loop(more, back, (jnp.int32(0), alive(carry[0]), carry))
        y_ref[...] = acc
        rems_ref[0] = tab
        _ride_wait(cps, (pl.program_id(0) == H - 1) & (i == nq - 1))

    x_in, x_out, x_shapes, x_scratch, x_args = _ride_parts(ride)
    res = pl.pallas_call(
        body, name=name,
        out_shape=(jax.ShapeDtypeStruct((T, H * 128), F32), jax.ShapeDtypeStruct((H, T, V7X_LANES), F32), *x_shapes),
        grid=(H, nq),
        in_specs=[pl.BlockSpec((tq, 128), lambda h, i: (i, qb + h)), pl.BlockSpec((T, 128), lambda h, i: (0, kb + h)),
                  pl.BlockSpec((T, 128), lambda h, i: (0, vb + h)), *x_in],
        out_specs=(pl.BlockSpec((tq, 128), lambda h, i: (i, h)),
                   pl.BlockSpec((1, tq, V7X_LANES), lambda h, i: (h, i, 0)), *x_out),
        scratch_shapes=x_scratch,
        compiler_params=_cparams("arbitrary", "arbitrary"),
    )(proj, proj, proj, *x_args)
    return res[0], res[1], res[2:]


def _sb_bwd(proj, dy, rems, *, H, qcol0, kcol0, vcol0, scale, name, ride=None):
    T = proj.shape[0]
    tq = tk = ATT_BLK
    nq = T // tq
    qb, kb, vb = qcol0 // 128, kcol0 // 128, vcol0 // 128

    def body(*refs):
        (q_ref, k_ref, v_ref, dy_ref, rems_ref, dq_ref, dk_ref, dv_ref, dk_acc, dv_acc), cps = _ride_unpack(
            ride, refs, 5, 3)
        _ride_start(cps, pl.program_id(0) == 0)
        dk_acc[...] = jnp.zeros_like(dk_acc)
        dv_acc[...] = jnp.zeros_like(dv_acc)
        rowi = lax.broadcasted_iota(jnp.int32, (tq, tk), 0)
        coli = lax.broadcasted_iota(jnp.int32, (tq, tk), 1)
        r_i = lax.broadcasted_iota(jnp.int32, (tk, tk), 0)
        c_i = lax.broadcasted_iota(jnp.int32, (tk, tk), 1)
        tri_upto = (r_i <= c_i).astype(BF16)
        tri_before = (r_i < c_i).astype(BF16)
        lane = lax.broadcasted_iota(jnp.int32, (tq, V7X_LANES), 1)
        lane1 = lax.broadcasted_iota(jnp.int32, (1, V7X_LANES), 1)

        def qblock(i, c):
            qoff = pl.multiple_of(i * tq, tq)
            qs = q_ref[pl.ds(qoff, tq), :].astype(BF16)
            dyb = dy_ref[pl.ds(qoff, tq), :].astype(BF16)
            tab = rems_ref[0, pl.ds(qoff, tq), :]

            def step(j, carry, masked):
                pre, dq = carry
                rem = jnp.sum(jnp.where(lane == j, tab, 0.0), axis=1, keepdims=True)
                koff = pl.multiple_of(j * tk, tk)
                ks = k_ref[pl.ds(koff, tk), :].astype(BF16)
                vs = v_ref[pl.ds(koff, tk), :].astype(BF16)
                z = lax.dot_general(qs, ks, NT_DIMS, preferred_element_type=F32) * scale
                lb, lom, sg = _log_sigmoid_pair(z)
                if masked:
                    valid = coli + j * tk < rowi + i * tq
                    lom = jnp.where(valid, lom, 0.0)
                a = jnp.exp(lb + rem - _tri_dot(lom, tri_upto))
                if masked:
                    a = jnp.where(valid, a, 0.0)
                dv_acc[pl.ds(koff, tk), :] += lax.dot_general(a.astype(BF16), dyb, TN_DIMS,
                                                              preferred_element_type=F32)
                de = a * lax.dot_general(dyb, vs, NT_DIMS, preferred_element_type=F32)
                before = pre + _tri_dot(de, tri_before)
                dz = de * (1.0 - sg) - before * sg
                if masked:
                    dz = jnp.where(valid, dz, 0.0)
                dzb = (dz * scale).astype(BF16)
                dk_acc[pl.ds(koff, tk), :] += lax.dot_general(dzb, qs, TN_DIMS, preferred_element_type=F32)
                dq = dq + jnp.dot(dzb, ks, preferred_element_type=F32)
                return pre + jnp.sum(de, axis=1, keepdims=True), dq

            n_full = (i * tq) // tk
            n_tot = ((i + 1) * tq + tk - 1) // tk
            colmax = jnp.max(tab, axis=0, keepdims=True)
            dead = (lane1 >= 1) & (lane1 <= n_full) & (colmax < SB_DEAD)
            j0 = jnp.sum(dead.astype(jnp.int32))
            carry = (jnp.zeros((tq, 1), F32), jnp.zeros((tq, 128), F32))
            carry = lax.fori_loop(j0, n_full, functools.partial(step, masked=False), carry)
            _, dq = lax.fori_loop(n_full, n_tot, functools.partial(step, masked=True), carry)
            dq_ref[pl.ds(qoff, tq), :] = dq.astype(dq_ref.dtype)
            return c

        lax.fori_loop(0, nq, qblock, 0)
        dk_ref[...] = dk_acc[...].astype(dk_ref.dtype)
        dv_ref[...] = dv_acc[...].astype(dv_ref.dtype)
        _ride_wait(cps, pl.program_id(0) == H - 1)

    def seg(b):
        return pl.BlockSpec((T, 128), lambda h: (0, b + h))

    out = pl.BlockSpec((T, 128), lambda h: (0, h))
    x_in, x_out, x_shapes, x_scratch, x_args = _ride_parts(ride)
    res = pl.pallas_call(
        body, name=name,
        out_shape=(*(jax.ShapeDtypeStruct((T, H * 128), BF16) for _ in range(3)), *x_shapes),
        grid=(H,),
        in_specs=[seg(qb), seg(kb), seg(vb), out, pl.BlockSpec((1, T, V7X_LANES), lambda h: (h, 0, 0)), *x_in],
        out_specs=(out, out, out, *x_out),
        scratch_shapes=[pltpu.VMEM((T, 128), F32), pltpu.VMEM((T, 128), F32), *x_scratch],
        compiler_params=_cparams("arbitrary"),
    )(proj, proj, proj, dy, rems, *x_args)
    return res[0], res[1], res[2], res[3:]


def _final_loss(h, g, target, *, row0, n_rows, name):
    T, D = h.shape
    tm = _row_tile(T, D, 6)
    nt = T // tm

    def body(h_ref, g_ref, t_ref, dh_ref, dg_ref, loss_ref, acc_ref, lacc_ref):
        i = pl.program_id(0)
        xv = h_ref[...]
        r = lax.rsqrt(jnp.mean(xv * xv, axis=-1, keepdims=True) + EPS)
        xh = xv * r
        gv = g_ref[...]
        rows = i * tm + lax.broadcasted_iota(jnp.int32, (tm, 1), 0)
        valid = (rows >= row0) & (rows < row0 + n_rows)
        err = jnp.where(valid, xh * gv - t_ref[...], 0.0)
        dout = err * (1.0 / D)
        dxh = dout * gv
        dh_ref[...] = r * (dxh - xh * jnp.mean(dxh * xh, axis=-1, keepdims=True))
        part = jnp.sum((dout * xh).reshape(tm // 8, 8, D), axis=0)
        lpart = jnp.sum((err * err).reshape(tm // 8, 8, D), axis=0)

        @pl.when(i == 0)
        def _():
            acc_ref[...] = part
            lacc_ref[...] = lpart

        @pl.when(i > 0)
        def _():
            acc_ref[...] += part
            lacc_ref[...] += lpart

        @pl.when(i == nt - 1)
        def _():
            dg_ref[...] = jnp.sum(acc_ref[...], axis=0, keepdims=True)
            loss_ref[...] = (0.5 / D) * jnp.sum(jnp.sum(lacc_ref[...], axis=0, keepdims=True), axis=1, keepdims=True)

    row = pl.BlockSpec((tm, D), lambda i: (i, 0))
    vec = pl.BlockSpec((1, D), lambda i: (0, 0))
    return pl.pallas_call(
        body, name=name,
        out_shape=(jax.ShapeDtypeStruct((T, D), F32), jax.ShapeDtypeStruct((1, D), F32),
                   jax.ShapeDtypeStruct((1, 1), F32)),
        grid=(nt,),
        in_specs=[row, vec, row],
        out_specs=(row, vec, pl.BlockSpec((1, 1), lambda i: (0, 0))),
        scratch_shapes=[pltpu.VMEM((8, D), F32), pltpu.VMEM((8, D), F32)],
        compiler_params=_cparams("arbitrary"),
    )(h, g.reshape(1, D), target)


def _elementwise(fn, args, out_dtypes, name, ride=None):
    shape = args[0].shape
    C = shape[-1]
    R = math.prod(shape[:-1])
    n = len(args) + len(out_dtypes)
    cap = max(16, (VMEM_TILE_BUDGET // 2) // (2 * n * C * 4))
    tr = _divisor_tile(R, cap, 16)
    n_in, n_out = len(args), len(out_dtypes)
    steps = R // tr

    def body(*refs):
        own, cps = _ride_unpack(ride, refs, n_in, n_out)
        _ride_start(cps, pl.program_id(0) == 0)
        outs = fn(*[r[...] for r in own[:n_in]])
        for o_ref, val in zip(own[n_in:], outs):
            o_ref[...] = val.astype(o_ref.dtype)
        _ride_wait(cps, pl.program_id(0) == steps - 1)

    spec = pl.BlockSpec((tr, C), lambda i: (i, 0))
    x_in, x_out, x_shapes, x_scratch, x_args = _ride_parts(ride)
    res = pl.pallas_call(
        body, name=name,
        out_shape=(*(jax.ShapeDtypeStruct((R, C), dt) for dt in out_dtypes), *x_shapes),
        grid=(steps,),
        in_specs=[spec] * n_in + x_in, out_specs=(*([spec] * n_out), *x_out),
        scratch_shapes=x_scratch,
        compiler_params=_cparams("parallel" if ride is None else "arbitrary"),
    )(*[a.reshape(R, C) for a in args], *x_args)
    return (*(r.reshape(shape) for r in res[:n_out]), *res[n_out:])


def _adamw_math(w, g, m, v):
    m = ADAM_B1 * m + (1.0 - ADAM_B1) * g
    v = ADAM_B2 * v + (1.0 - ADAM_B2) * (g * g)
    m_hat = m / (1.0 - ADAM_B1 ** ADAM_STEP)
    v_hat = v / (1.0 - ADAM_B2 ** ADAM_STEP)
    delta = -ADAM_LR * (m_hat / (jnp.sqrt(v_hat) + ADAM_EPS) + ADAM_WD * w)
    return delta, m, v


def _adamw(w, g, m, v, name, ride=None):
    return _elementwise(_adamw_math, [w, g, m, v], [F32, F32, F32], name, ride=ride)


ANY = pl.BlockSpec(memory_space=pl.ANY)


def _position():
    return lax.axis_index("x"), lax.axis_index("y"), lax.axis_index("c")


class _Ride:
    def __init__(self, ins, out_shapes, n, make):
        self.ins, self.out_shapes, self.n, self.make = list(ins), list(out_shapes), n, make


def _ride_parts(ride):
    if ride is None:
        return [], [], [], [], []
    sems = [pltpu.SemaphoreType.DMA((ride.n,)), pltpu.SemaphoreType.DMA((ride.n,))]
    return [ANY] * len(ride.ins), [ANY] * len(ride.out_shapes), ride.out_shapes, sems, ride.ins


def _ride_unpack(ride, refs, n_in, n_out):
    if ride is None:
        return refs, []
    a, b = len(ride.ins), len(ride.out_shapes)
    own = refs[:n_in] + refs[n_in + a:n_in + a + n_out] + refs[n_in + a + n_out + b:-2]
    cps = ride.make(refs[n_in:n_in + a], refs[n_in + a + n_out:n_in + a + n_out + b], refs[-2], refs[-1])
    return own, cps


def _ride_start(cps, first):
    if cps:
        @pl.when(first)
        def _():
            for cp in cps:
                cp.start()


def _ride_wait(cps, last):
    if cps:
        @pl.when(last)
        def _():
            for cp in cps:
                cp.wait()


def _gather_ride_over_ici(halves):
    def make(ins, outs, send_sems, recv_sems):
        x, y, c = _position()
        chips = [(1 - x, y), (x, 1 - y), (1 - x, 1 - y)]
        return [pltpu.make_async_remote_copy(src_ref=a.at[c], dst_ref=o.at[2 * x + y],
                                             send_sem=send_sems.at[3 * t + k], recv_sem=recv_sems.at[3 * t + k],
                                             device_id=(qx, qy, c), device_id_type=MESH)
                for t, (a, o) in enumerate(zip(ins, outs)) for k, (qx, qy) in enumerate(chips)]

    shapes = [jax.ShapeDtypeStruct((4,) + a.shape[1:], a.dtype) for a in halves]
    return _Ride(halves, shapes, 3 * len(halves), make)


def _gather_ride_to_sibling(landed):
    def make(ins, outs, send_sems, recv_sems):
        x, y, c = _position()
        chips = [(1 - x, y), (x, 1 - y), (1 - x, 1 - y)]
        return [pltpu.make_async_remote_copy(src_ref=a.at[2 * qx + qy], dst_ref=o.at[2 * qx + qy],
                                             send_sem=send_sems.at[3 * t + k], recv_sem=recv_sems.at[3 * t + k],
                                             device_id=(x, y, 1 - c), device_id_type=MESH)
                for t, (a, o) in enumerate(zip(ins, outs)) for k, (qx, qy) in enumerate(chips)]

    shapes = [jax.ShapeDtypeStruct(a.shape, a.dtype) for a in landed]
    return _Ride(landed, shapes, 3 * len(landed), make)


def _assemble_shards(own, mine, other):
    x, y, c = _position()
    p = 2 * x + y
    out = []
    for q in range(4):
        full = jnp.where(c == 0, jnp.concatenate([mine[q], other[q]], axis=0),
                         jnp.concatenate([other[q], mine[q]], axis=0))
        out.append(jnp.where(p == q, own, full))
    return out


def _gather_shards(a, name):
    def body(a_ref, o_ref, send_sems, recv_sems):
        x, y, c = _position()
        p = 2 * x + y
        chips = [(1 - x, y), (x, 1 - y), (1 - x, 1 - y)]

        def copy(k, src, dst, to):
            return pltpu.make_async_remote_copy(src_ref=src, dst_ref=dst, send_sem=send_sems.at[k],
                                                recv_sem=recv_sems.at[k], device_id=to, device_id_type=MESH)

        first = [copy(k, a_ref.at[c], o_ref.at[p, c], (qx, qy, c)) for k, (qx, qy) in enumerate(chips)]
        for cp in first:
            cp.start()
        passed = []
        for k, (qx, qy) in enumerate(chips):
            land = o_ref.at[2 * qx + qy, c]
            copy(k, land, land, (x, y, c)).wait_recv()
            fwd = copy(3 + k, land, land, (x, y, 1 - c))
            fwd.start()
            passed.append(fwd)
        for k, (qx, qy) in enumerate(chips):
            land = o_ref.at[2 * qx + qy, 1 - c]
            copy(3 + k, land, land, (x, y, c)).wait_recv()
        for cp in first + passed:
            cp.wait_send()

    return pl.pallas_call(
        body, name=name,
        out_shape=jax.ShapeDtypeStruct((4,) + a.shape, a.dtype),
        in_specs=[ANY], out_specs=ANY,
        scratch_shapes=[pltpu.SemaphoreType.DMA((6,)), pltpu.SemaphoreType.DMA((6,))],
    )(a)


def _with_own(gathered, own):
    x, y, _ = _position()
    p = 2 * x + y
    return [jnp.where(p == q, own, gathered[q]) for q in range(4)]


def _swap_halves(g, name):
    def body(g_ref, o_ref, send_sem, recv_sem):
        x, y, c = _position()
        cp = pltpu.make_async_remote_copy(src_ref=g_ref.at[1 - c], dst_ref=o_ref, send_sem=send_sem,
                                          recv_sem=recv_sem, device_id=(x, y, 1 - c), device_id_type=MESH)
        cp.start()
        cp.wait()

    return pl.pallas_call(
        body, name=name,
        out_shape=jax.ShapeDtypeStruct(g.shape[1:], g.dtype),
        in_specs=[ANY], out_specs=ANY,
        scratch_shapes=[pltpu.SemaphoreType.DMA, pltpu.SemaphoreType.DMA],
    )(g)


def _scatter_copies(p_ref, o_ref, send_sems, recv_sems):
    x, y, c = _position()
    chips = [(1 - x, y), (x, 1 - y), (1 - x, 1 - y)]
    return [pltpu.make_async_remote_copy(src_ref=p_ref.at[2 * qx + qy], dst_ref=o_ref.at[k],
                                         send_sem=send_sems.at[k], recv_sem=recv_sems.at[k],
                                         device_id=(qx, qy, c), device_id_type=MESH)
            for k, (qx, qy) in enumerate(chips)]


def _scatter_to_chips(pb, name):
    def body(p_ref, o_ref, send_sems, recv_sems):
        cps = _scatter_copies(p_ref, o_ref, send_sems, recv_sems)
        for cp in cps:
            cp.start()
        for cp in cps:
            cp.wait()

    return pl.pallas_call(
        body, name=name,
        out_shape=jax.ShapeDtypeStruct((3,) + pb.shape[1:], pb.dtype),
        in_specs=[ANY], out_specs=ANY,
        scratch_shapes=[pltpu.SemaphoreType.DMA((3,)), pltpu.SemaphoreType.DMA((3,))],
    )(pb)


def _join_halves(r, name):
    def body(r_ref, o_ref, send_sem, recv_sem):
        x, y, c = _position()
        cp = pltpu.make_async_remote_copy(src_ref=r_ref, dst_ref=o_ref, send_sem=send_sem,
                                          recv_sem=recv_sem, device_id=(x, y, 1 - c), device_id_type=MESH)
        cp.start()
        cp.wait()

    other = pl.pallas_call(
        body, name=name,
        out_shape=jax.ShapeDtypeStruct(r.shape, r.dtype),
        in_specs=[ANY], out_specs=ANY,
        scratch_shapes=[pltpu.SemaphoreType.DMA, pltpu.SemaphoreType.DMA],
    )(r)
    c = lax.axis_index("c")
    return jnp.stack([jnp.where(c == 0, r, other), jnp.where(c == 0, other, r)], axis=0)


def _reduce_to_shard(gh, tag):
    psum, pb = _reduce_to_shard_begin(gh, tag)
    return _reduce_to_shard_end(psum, _scatter_to_chips(pb, f"scatter_{tag}"), tag)


def _reduce_to_shard_begin(gh, tag):
    c = lax.axis_index("c")
    sib = _swap_halves(gh, f"swap_{tag}")
    mine = lax.dynamic_index_in_dim(gh, c, 0, keepdims=False)
    return _elementwise(lambda a, b: (a + b, a + b), [mine, sib], [F32, BF16], f"pairsum_{tag}")


def _reduce_to_shard_end(psum, got, tag):
    x, y, _ = _position()
    own = lax.dynamic_index_in_dim(psum, 2 * x + y, 0, keepdims=False)
    (red,) = _elementwise(lambda o, a, b, d: (((o + a.astype(F32)) + b.astype(F32)) + d.astype(F32),),
                          [own, got[0], got[1], got[2]], [F32], f"chipsum_{tag}")
    return _join_halves(red, f"join_{tag}")


def _scatter_ride(pieces):
    def make(ins, outs, send_sems, recv_sems):
        return _scatter_copies(ins[0], outs[0], send_sems, recv_sems)

    return _Ride([pieces], [jax.ShapeDtypeStruct((3,) + pieces.shape[1:], pieces.dtype)], 3, make)


def _swap_rows(g, name):
    K, N = g.shape
    Kh = K // 2

    def body(g_ref, o_ref, send_sem, recv_sem):
        x, y, c = _position()
        theirs = g_ref.at[pl.ds(pl.multiple_of((1 - c) * Kh, 8), Kh)]
        cp = pltpu.make_async_remote_copy(src_ref=theirs, dst_ref=o_ref, send_sem=send_sem,
                                          recv_sem=recv_sem, device_id=(x, y, 1 - c), device_id_type=MESH)
        cp.start()
        cp.wait()

    return pl.pallas_call(
        body, name=name,
        out_shape=jax.ShapeDtypeStruct((Kh, N), g.dtype),
        in_specs=[ANY], out_specs=ANY,
        scratch_shapes=[pltpu.SemaphoreType.DMA, pltpu.SemaphoreType.DMA],
    )(g)


def _swap_rows_ride(g):
    K, N = g.shape
    Kh = K // 2

    def make(ins, outs, send_sems, recv_sems):
        x, y, c = _position()
        theirs = ins[0].at[pl.ds(pl.multiple_of((1 - c) * Kh, 8), Kh)]
        return [pltpu.make_async_remote_copy(src_ref=theirs, dst_ref=outs[0], send_sem=send_sems.at[0],
                                             recv_sem=recv_sems.at[0], device_id=(x, y, 1 - c),
                                             device_id_type=MESH)]

    return _Ride([g], [jax.ShapeDtypeStruct((Kh, N), g.dtype)], 1, make)


def _pairsum_rows(g, sib, name):
    K, N = g.shape
    Kh = K // 2
    cap = max(16, (VMEM_TILE_BUDGET // 2) // (2 * 4 * N * 4))
    tr = _divisor_tile(Kh, cap, 16)
    nb = Kh // tr

    def body(c_ref, g_ref, s_ref, p_ref, pb_ref):
        s = g_ref[...] + s_ref[...]
        p_ref[...] = s
        pb_ref[...] = s.astype(BF16)

    mine = pl.BlockSpec((tr, N), lambda i, c_ref: (i + c_ref[0] * nb, 0))
    row = pl.BlockSpec((tr, N), lambda i, c_ref: (i, 0))
    return pl.pallas_call(
        body, name=name,
        out_shape=(jax.ShapeDtypeStruct((Kh, N), F32), jax.ShapeDtypeStruct((Kh, N), BF16)),
        grid_spec=pltpu.PrefetchScalarGridSpec(num_scalar_prefetch=1, grid=(nb,), in_specs=[mine, row],
                                               out_specs=(row, row)),
        compiler_params=_cparams("parallel"),
    )(lax.axis_index("c").astype(jnp.int32).reshape(1), g, sib)


def _reduce_rows_begin(g, sib, tag):
    K, N = g.shape
    n = N // 4
    x, y, _ = _position()
    psum, pb = _pairsum_rows(g, sib, f"pairsum_{tag}")
    pieces = pb.reshape(K // 2, 4, n).transpose(1, 0, 2)
    own = lax.dynamic_slice_in_dim(psum, (2 * x + y) * n, n, axis=1)
    return pieces, own


def _reduce_rows_end(own, got, tag):
    (red,) = _elementwise(lambda o, a, b, d: (((o + a.astype(F32)) + b.astype(F32)) + d.astype(F32),),
                          [own, got[0], got[1], got[2]], [F32], f"chipsum_{tag}")
    return _join_halves(red, f"join_{tag}").reshape(2 * own.shape[0], own.shape[1])


def _all_reduce_small(vec, name):
    R = vec.shape[0]

    def body(v_ref, o_ref, land_ref, send_sems, recv_sems):
        x, y, c = _position()
        me = 4 * x + 2 * y + c
        land_ref[me] = v_ref[...]
        cps = []
        for r in range(1, 8):
            rx, ry, rc = (r >> 2) & 1, (r >> 1) & 1, r & 1
            to = (x ^ rx, y ^ ry, c ^ rc)
            cps.append(pltpu.make_async_remote_copy(src_ref=v_ref, dst_ref=land_ref.at[me],
                                                    send_sem=send_sems.at[r - 1], recv_sem=recv_sems.at[r - 1],
                                                    device_id=to, device_id_type=MESH))
        for cp in cps:
            cp.start()
        for cp in cps:
            cp.wait()
        total = land_ref[0]
        for d in range(1, 8):
            total = total + land_ref[d]
        o_ref[...] = total

    return pl.pallas_call(
        body, name=name,
        out_shape=jax.ShapeDtypeStruct((R, 128), F32),
        in_specs=[pl.BlockSpec(memory_space=pltpu.VMEM)], out_specs=pl.BlockSpec(memory_space=pltpu.VMEM),
        scratch_shapes=[pltpu.VMEM((8, R, 128), F32), pltpu.SemaphoreType.DMA((7,)), pltpu.SemaphoreType.DMA((7,))],
    )(vec)


def _rot(w):
    half = ROPE_DIM // 2
    return jnp.concatenate([-w[..., half:], w[..., :half]], axis=-1)


def _unrot(g):
    half = ROPE_DIM // 2
    return jnp.concatenate([g[..., half:], -g[..., :half]], axis=-1)


class _Layout:
    def __init__(self, D, QL, KVL):
        self.D, self.QL, self.KVL = D, QL, KVL
        self.H = D // 256
        self.WG = self.H * 128
        WG = self.WG
        self.z_mla, self.q_sb, self.k_sb, self.v_sb, self.z_sb = 0, WG, 2 * WG, 3 * WG, 4 * WG
        self.c_q = 5 * WG
        self.c_kv = self.c_q + QL
        self.k_r = self.c_kv + KVL
        self.width = -(-(self.k_r + 256) // 512) * 512
        self.orig = (QL, KVL, ROPE_DIM, WG, WG, WG, WG, WG)
        self.din = sum(self.orig)

    def pack_w_in(self, w):
        cuts = []
        o = 0
        for s in self.orig:
            cuts.append(w[:, o:o + s])
            o += s
        c_q, c_kv, k_r, z_mla, q_sb, k_sb, v_sb, z_sb = cuts
        z64 = jnp.zeros((w.shape[0], 128 - ROPE_DIM), w.dtype)
        pad = jnp.zeros((w.shape[0], self.width - self.k_r - 256), w.dtype)
        return jnp.concatenate([z_mla, q_sb, k_sb, v_sb, z_sb, c_q, c_kv, k_r, z64, _rot(k_r), z64, pad], axis=1)

    def pack_w_uq(self, w):
        H = self.H
        w3 = w.reshape(w.shape[0], H, 128 + ROPE_DIM)
        nope = w3[:, :, :128]
        r = w3[:, :, 128:]
        z = jnp.zeros(r.shape, w.dtype)
        a = jnp.concatenate([r, z], axis=-1)
        b = jnp.concatenate([_rot(r), z], axis=-1)
        return jnp.concatenate([nope.reshape(-1, H * 128), a.reshape(-1, H * 128), b.reshape(-1, H * 128)], axis=1)

    def unpack_dw_uq(self, g):
        H = self.H
        HB = H * 128
        nope = g[:, :HB].reshape(-1, H, 128)
        a = g[:, HB:2 * HB].reshape(-1, H, 128)[:, :, :ROPE_DIM]
        b = g[:, 2 * HB:].reshape(-1, H, 128)[:, :, :ROPE_DIM]
        return jnp.concatenate([nope, a + _unrot(b)], axis=-1).reshape(-1, H * (128 + ROPE_DIM))


def _rope_tables(T):
    inv_freq = ROPE_THETA ** (-jnp.arange(0, ROPE_DIM, 2, dtype=F32) / ROPE_DIM)
    ang = jnp.arange(T, dtype=jnp.int32).astype(F32)[:, None] * inv_freq[None, :]
    z = jnp.zeros((T, 128 - ROPE_DIM), F32)
    cos, sin = jnp.cos(ang), jnp.sin(ang)
    return jnp.concatenate([cos, cos, z], axis=1), jnp.concatenate([sin, sin, z], axis=1)


def _layer_fwd(h, wl, lay, tabs, nxt):
    g_norm, w_in, g_q, g_kv, w_uq, w_ukv, g_mla, g_sb, w_o = wl
    cosp, sinp = tabs
    H = lay.H
    u = _rms_fwd(h, g_norm, col0=0, out_dtype=BF16, name="rms_h")
    proj = _matmul(u, w_in, mode="nn", out_dtype=F32, name="mm_in")
    cqn = _rms_fwd(proj, g_q, col0=lay.c_q, out_dtype=BF16, name="rms_cq")
    ckvn = _rms_fwd(proj, g_kv, col0=lay.c_kv, out_dtype=BF16, name="rms_ckv")
    q = _matmul(cqn, w_uq, mode="nn", out_dtype=F32, name="mm_uq")
    kv = _matmul(ckvn, w_ukv, mode="nn", out_dtype=F32, name="mm_ukv")
    qc, kc, v = _rope_fwd(q, kv, proj, cosp, sinp, H=H, kr_col0=lay.k_r, name="rope_fwd")
    if nxt is None:
        o_mla, lse, _ = _mla_fwd(qc, kc, v, H=H, scale=1.0 / math.sqrt(128 + ROPE_DIM), name="mla_fwd")
        o_sb, rems, _ = _sb_fwd(proj, H=H, qcol0=lay.q_sb, kcol0=lay.k_sb, vcol0=lay.v_sb,
                                scale=1.0 / math.sqrt(128), name="sb_fwd")
        gathered = None
    else:
        o_mla, lse, mine = _mla_fwd(qc, kc, v, H=H, scale=1.0 / math.sqrt(128 + ROPE_DIM), name="mla_fwd_gather",
                                    ride=_gather_ride_over_ici(nxt))
        o_sb, rems, other = _sb_fwd(proj, H=H, qcol0=lay.q_sb, kcol0=lay.k_sb, vcol0=lay.v_sb,
                                    scale=1.0 / math.sqrt(128), name="sb_fwd_gather",
                                    ride=_gather_ride_to_sibling(mine))
        gathered = (mine, other)
    y_mla = _gate_fwd(o_mla, proj, g_mla, zcol0=lay.z_mla, name="gate_fwd_mla")
    y_sb = _gate_fwd(o_sb, proj, g_sb, zcol0=lay.z_sb, name="gate_fwd_sb")
    y = jnp.concatenate([y_mla, y_sb], axis=1)
    h_out = _matmul(y, w_o, mode="nn", out_dtype=F32, name="mm_o", residual=h)
    saved = (h, u, proj, cqn, ckvn, qc, kc, v, o_mla, lse, o_sb, rems, y)
    return h_out, saved, gathered


def _layer_bwd(dh, saved, wl, lay, tabs, carry):
    g_norm, w_in, g_q, g_kv, w_uq, w_ukv, g_mla, g_sb, w_o = wl
    h, u, proj, cqn, ckvn, qc, kc, v, o_mla, lse, o_sb, rems, y = saved
    cosp, sinp = tabs
    H = lay.H
    dy = _matmul(dh, w_o, mode="nt", out_dtype=F32, name="mm_o_dx")
    d_w_o = _matmul(y, dh, mode="tn", out_dtype=F32, name="mm_o_dw")
    do_mla, dz_mla, dg_mla = _gate_bwd(dy, o_mla, proj, g_mla, grp=0, zcol0=lay.z_mla, name="gate_bwd_mla")
    do_sb, dz_sb, dg_sb = _gate_bwd(dy, o_sb, proj, g_sb, grp=1, zcol0=lay.z_sb, name="gate_bwd_sb")
    if carry is None:
        dq_sb, dk_sb, dv_sb, _ = _sb_bwd(proj, do_sb, rems, H=H, qcol0=lay.q_sb, kcol0=lay.k_sb, vcol0=lay.v_sb,
                                         scale=1.0 / math.sqrt(128), name="sb_bwd")
        dqc, dkc, dv = _mla_bwd(qc, kc, v, o_mla, do_mla, lse, H=H, scale=1.0 / math.sqrt(128 + ROPE_DIM),
                                name="mla_bwd")
        g_w_in_above = None
    else:
        dq_sb, dk_sb, dv_sb, (sib,) = _sb_bwd(proj, do_sb, rems, H=H, qcol0=lay.q_sb, kcol0=lay.k_sb,
                                              vcol0=lay.v_sb, scale=1.0 / math.sqrt(128), name="sb_bwd_swap",
                                              ride=_swap_rows_ride(carry))
        pieces, own = _reduce_rows_begin(carry, sib, "w_in")
        dqc, dkc, dv, got = _mla_bwd(qc, kc, v, o_mla, do_mla, lse, H=H, scale=1.0 / math.sqrt(128 + ROPE_DIM),
                                     name="mla_bwd_scatter", pieces=pieces)
        g_w_in_above = _reduce_rows_end(own, got, "w_in")
    dq, dkv, dkr = _rope_bwd(dqc, dkc, dv, cosp, sinp, H=H, name="rope_bwd")
    d_w_uq = _matmul(cqn, dq, mode="tn", out_dtype=F32, name="mm_uq_dw")
    dcqn = _matmul(dq, w_uq, mode="nt", out_dtype=F32, name="mm_uq_dx")
    d_w_ukv = _matmul(ckvn, dkv, mode="tn", out_dtype=F32, name="mm_ukv_dw")
    dckvn = _matmul(dkv, w_ukv, mode="nt", out_dtype=F32, name="mm_ukv_dx")
    dcq, dg_q = _rms_bwd(proj, g_q, dcqn, col0=lay.c_q, out_dtype=BF16, name="rms_cq_bwd")
    dckv, dg_kv = _rms_bwd(proj, g_kv, dckvn, col0=lay.c_kv, out_dtype=BF16, name="rms_ckv_bwd")
    pad = jnp.zeros((dh.shape[0], lay.width - lay.k_r - 256), BF16)
    dproj = jnp.concatenate([dz_mla, dq_sb, dk_sb, dv_sb, dz_sb, dcq, dckv, dkr, pad], axis=1)
    du = _matmul(dproj, w_in, mode="nt", out_dtype=F32, name="mm_in_dx")
    dkr_o = (dkr[:, :ROPE_DIM].astype(F32) + _unrot(dkr[:, 128:128 + ROPE_DIM].astype(F32))).astype(BF16)
    pad_o = jnp.zeros((dh.shape[0], lay.width - lay.din), BF16)
    dproj_o = jnp.concatenate([dcq, dckv, dkr_o, dz_mla, dq_sb, dk_sb, dv_sb, dz_sb, pad_o], axis=1)
    d_w_in = _matmul(u, dproj_o, mode="tn", out_dtype=F32, name="mm_in_dw", out_cols=lay.din)
    dh_prev, dg_norm = _rms_bwd(h, g_norm, du, col0=0, out_dtype=F32, name="rms_h_bwd", residual=dh)
    grads = (dg_norm[0], None, dg_q[0], dg_kv[0], lay.unpack_dw_uq(d_w_uq), d_w_ukv, dg_mla[0], dg_sb[0], d_w_o)
    return dh_prev, grads, d_w_in, g_w_in_above


def _halves(a):
    return a.reshape((2, a.shape[0] // 2) + a.shape[1:])


def _gather_cols(w, name):
    L, K, n = w.shape
    own = w.astype(BF16)
    g = _with_own(_gather_shards(_halves(own), name).reshape(4, L, K, n), own)
    return [jnp.concatenate([g[q][l] for q in range(4)], axis=1) for l in range(L)]


def _cut_cols(g):
    L, K, N = g.shape
    return g.reshape(2, L // 2, K, 4, N // 4).transpose(0, 3, 1, 2, 4)


def kernel(x, meta_tokens, g_norm, w_in, g_q, g_kv, w_uq, w_ukv, g_out_mla, g_out_sb, w_o, g_final, loss_target, m_meta_tokens, m_g_norm, m_w_in, m_g_q, m_g_kv, m_w_uq, m_w_ukv, m_g_out_mla, m_g_out_sb, m_w_o, m_g_final, v_meta_tokens, v_g_norm, v_w_in, v_g_q, v_g_kv, v_w_uq, v_w_ukv, v_g_out_mla, v_g_out_sb, v_w_o, v_g_final):
    _, S, D = x.shape
    NM = meta_tokens.shape[0]
    L = g_norm.shape[0]
    lay = _Layout(D, g_q.shape[1], g_kv.shape[1])
    TP = -(-(NM + S) // ROW_ALIGN) * ROW_ALIGN
    tabs = _rope_tables(TP)

    w_in_b, w_o_b = w_in.astype(BF16), w_o.astype(BF16)

    def row_halves(a):
        return a.reshape((2, a.shape[0] // 2) + a.shape[1:])

    def first_layer(w, name):
        g = _gather_shards(row_halves(w), name)
        return _with_own(g.reshape((4,) + w.shape), w)

    w_in_0 = lay.pack_w_in(jnp.concatenate(first_layer(w_in_b[0], "gather_w_in"), axis=1))
    w_o_0 = jnp.concatenate(first_layer(w_o_b[0], "gather_w_o"), axis=0)
    w_uq_full = [lay.pack_w_uq(w) for w in _gather_cols(w_uq, "gather_w_uq")]
    w_ukv_full = _gather_cols(w_ukv, "gather_w_ukv")
    meta_g = _with_own(_gather_shards(meta_tokens.reshape(2, NM // 2, -1), "gather_meta").reshape(4, NM, -1),
                       meta_tokens)
    meta_full = jnp.concatenate(meta_g, axis=1)

    h = jnp.concatenate([meta_full, x[0], jnp.zeros((TP - NM - S, D), F32)], axis=0)
    target = jnp.pad(loss_target[0], ((NM, TP - NM - S), (0, 0)))
    weights, saved = [], []
    w_in_l, w_o_l = w_in_0, w_o_0
    for l in range(L):
        weights.append((g_norm[l], w_in_l, g_q[l], g_kv[l], w_uq_full[l], w_ukv_full[l], g_out_mla[l],
                        g_out_sb[l], w_o_l))
        nxt = [row_halves(w_in_b[l + 1]), row_halves(w_o_b[l + 1])] if l + 1 < L else None
        h, s, gathered = _layer_fwd(h, weights[l], lay, tabs, nxt)
        saved.append(s)
        if gathered is not None:
            (in_mine, o_mine), (in_other, o_other) = gathered
            w_in_l = lay.pack_w_in(jnp.concatenate(_assemble_shards(w_in_b[l + 1], in_mine, in_other), axis=1))
            w_o_l = jnp.concatenate(_assemble_shards(w_o_b[l + 1], o_mine, o_other), axis=0)
    dh, dg_final, loss_part = _final_loss(h, g_final, target, row0=NM, n_rows=S, name="final_loss")

    layer_grads = [None] * L
    g_w_in_layers = [None] * L
    carry = None
    for l in reversed(range(L)):
        dh, layer_grads[l], carry, g_above = _layer_bwd(dh, saved[l], weights[l], lay, tabs, carry)
        if g_above is not None:
            g_w_in_layers[l + 1] = g_above
    pieces, own = _reduce_rows_begin(carry, _swap_rows(carry, "swap_w_in"), "w_in")
    g_w_in_layers[0] = _reduce_rows_end(own, _scatter_to_chips(pieces, "scatter_w_in"), "w_in")
    grad_x = dh[NM:NM + S][None]
    d_meta = dh[:NM]

    def stack(i):
        return jnp.stack([layer_grads[l][i] for l in range(L)], axis=0)

    small = [stack(0), stack(2), stack(3), stack(6), stack(7), dg_final[0]]
    flat = jnp.concatenate([s.reshape(-1) for s in small])
    n_flat = flat.shape[0]
    rows = -(-n_flat // (8 * 128)) * 8
    packed = jnp.pad(flat, (0, rows * 128 - n_flat)).reshape(rows, 128)
    summed = _all_reduce_small(packed, "allreduce_gains").reshape(-1)
    small_red = []
    o = 0
    for s in small:
        small_red.append(summed[o:o + s.size].reshape(s.shape))
        o += s.size
    g_g_norm, g_g_q, g_g_kv, g_g_mla, g_g_sb, g_g_final = small_red

    g_w_in = jnp.stack(g_w_in_layers, axis=0)
    g_w_uq = _reduce_to_shard(_cut_cols(stack(4)), "w_uq").reshape(w_uq.shape)
    g_w_ukv = _reduce_to_shard(_cut_cols(stack(5)), "w_ukv").reshape(w_ukv.shape)
    d_w_o = stack(8).reshape(2, L // 2, 4, w_o.shape[1], D).transpose(0, 2, 1, 3, 4)
    psum_w_o, pieces_w_o = _reduce_to_shard_begin(d_w_o, "w_o")
    d_meta = d_meta.reshape(2, NM // 2, 4, D // 4).transpose(0, 2, 1, 3)
    g_meta = _reduce_to_shard(d_meta, "meta").reshape(meta_tokens.shape)

    loss = lax.psum(loss_part[0, 0], ("x", "y", "c"))

    names = ["meta", "g_norm", "w_in", "g_q", "g_kv", "w_uq", "w_ukv", "g_out_mla", "g_out_sb", "w_o", "g_final"]
    ws = [meta_tokens, g_norm, w_in, g_q, g_kv, w_uq, w_ukv, g_out_mla, g_out_sb, w_o, g_final]
    gs = [g_meta, g_g_norm, g_w_in, g_g_q, g_g_kv, g_w_uq, g_w_ukv, g_g_mla, g_g_sb, None, g_g_final]
    ms = [m_meta_tokens, m_g_norm, m_w_in, m_g_q, m_g_kv, m_w_uq, m_w_ukv, m_g_out_mla, m_g_out_sb, m_w_o, m_g_final]
    vs = [v_meta_tokens, v_g_norm, v_w_in, v_g_q, v_g_kv, v_w_uq, v_w_ukv, v_g_out_mla, v_g_out_sb, v_w_o, v_g_final]
    deltas, new_m, new_v = [], [], []
    for i, (n, w, m, v) in enumerate(zip(names, ws, ms, vs)):
        g = gs[i]
        shape = w.shape
        if w.ndim == 1:
            w, g, m, v = (a.reshape(1, -1) for a in (w, g, m, v))
        flip = w.ndim == 3 and w.shape[2] % V7X_LANES != 0
        if flip:
            w, g, m, v = (jnp.swapaxes(a, 1, 2) for a in (w, g, m, v))
        if n == "w_in":
            d, nm, nv, got_w_o = _adamw(w, g, m, v, f"adamw_{n}", ride=_scatter_ride(pieces_w_o))
            gs[names.index("w_o")] = _reduce_to_shard_end(psum_w_o, got_w_o, "w_o").reshape(w_o.shape)
        else:
            d, nm, nv = _adamw(w, g, m, v, f"adamw_{n}")
        if flip:
            d, nm, nv = (jnp.swapaxes(a, 1, 2) for a in (d, nm, nv))
        deltas.append(d.reshape(shape))
        new_m.append(nm.reshape(shape))
        new_v.append(nv.reshape(shape))
    return (loss, grad_x, *gs, *deltas, *new_m, *new_v)
```

```python
import functools
import math

import jax
import jax.numpy as jnp
from jax import lax
from jax.experimental import pallas as pl
from jax.experimental.pallas import tpu as pltpu

F32 = jnp.float32
BF16 = jnp.bfloat16
MESH = pl.DeviceIdType.MESH

V7X_LANES = 128
VMEM_LIMIT = 56 * 1024 * 1024
VMEM_TILE_BUDGET = 40 * 1024 * 1024

ROPE_DIM = 64
ROPE_THETA = 10000.0
EPS = 1e-6
ROW_ALIGN = 384
ATT_BLK = 384
ROPE_ROWS = 128

ADAM_LR = 0.001
ADAM_B1 = 0.9
ADAM_B2 = 0.999
ADAM_EPS = 1e-08
ADAM_WD = 0.01
ADAM_STEP = 10

NEG = -1e30
SB_DEAD = -104.0
NT_DIMS = (((1,), (1,)), ((), ()))
TN_DIMS = (((0,), (0,)), ((), ()))
NN_DIMS = (((1,), (0,)), ((), ()))


def _cparams(*sem):
    return pltpu.CompilerParams(dimension_semantics=sem, vmem_limit_bytes=VMEM_LIMIT)


def _divisor_tile(n, cap, align):
    best = None
    t = align
    while t <= min(n, cap):
        if n % t == 0:
            best = t
        t += align
    return best if best is not None else n


def _mm_tiles(M, N, K, a_bytes, b_bytes, o_bytes, has_res):
    best = None
    for tm in sorted({_divisor_tile(M, c, 128) for c in (1408, 1024, 704, 512, 384, 256, 128)}, reverse=True):
        for tn in sorted({_divisor_tile(N, c, 128) for c in (1024, 512, 256, 128)}, reverse=True):
            for tk in sorted({_divisor_tile(K, c, 128) for c in (4096, 2048, 1408, 1024, 704, 512, 384, 256, 128)},
                             reverse=True):
                need = 2 * (tm * tk * a_bytes + tk * tn * b_bytes + tm * tn * o_bytes)
                need += 2 * tm * tn * 4 if has_res else 0
                need += tm * tn * 4 if tk != K else 0
                need += tm * tk * 2 if a_bytes != 2 else 0
                need += tk * tn * 2 if b_bytes != 2 else 0
                need += tm * tn * 4
                if need > VMEM_TILE_BUDGET:
                    continue
                score = (tm * tn / (tm + tn), tk)
                if best is None or score > best[0]:
                    best = (score, (tm, tn, tk))
    assert best is not None, (M, N, K)
    return best[1]


def _matmul(a, b, *, mode, out_dtype, name, residual=None, out_cols=None):
    if mode == "nn":
        (M, K), N = a.shape, b.shape[1]
    elif mode == "nt":
        (M, K), N = a.shape, b.shape[0]
    else:
        (K, M), N = a.shape, b.shape[1]
    tm, tn, tk = _mm_tiles(M, N, K, a.dtype.itemsize, b.dtype.itemsize, jnp.dtype(out_dtype).itemsize,
                           residual is not None)
    nk = K // tk
    assert out_cols is None or 0 <= N - out_cols < tn
    dims = {"nn": NN_DIMS, "nt": NT_DIMS, "tn": TN_DIMS}[mode]

    def body(*refs):
        if residual is not None:
            a_ref, b_ref, r_ref, o_ref = refs[:4]
        else:
            a_ref, b_ref, o_ref = refs[:3]
            r_ref = None
        part = lax.dot_general(a_ref[...].astype(BF16), b_ref[...].astype(BF16), dims,
                               preferred_element_type=F32)
        if nk == 1:
            if r_ref is not None:
                part = part + r_ref[...]
            o_ref[...] = part.astype(o_ref.dtype)
            return
        acc_ref = refs[-1]
        k = pl.program_id(2)

        @pl.when(k == 0)
        def _():
            acc_ref[...] = part

        @pl.when(k > 0)
        def _():
            acc_ref[...] += part

        @pl.when(k == nk - 1)
        def _():
            r = acc_ref[...]
            if r_ref is not None:
                r = r + r_ref[...]
            o_ref[...] = r.astype(o_ref.dtype)

    if mode == "tn":
        a_spec = pl.BlockSpec((tk, tm), lambda i, j, k: (k, i))
    else:
        a_spec = pl.BlockSpec((tm, tk), lambda i, j, k: (i, k))
    if mode == "nt":
        b_spec = pl.BlockSpec((tn, tk), lambda i, j, k: (j, k))
    else:
        b_spec = pl.BlockSpec((tk, tn), lambda i, j, k: (k, j))
    o_spec = pl.BlockSpec((tm, tn), lambda i, j, k: (i, j))
    in_specs = [a_spec, b_spec]
    args = [a, b]
    if residual is not None:
        in_specs.append(o_spec)
        args.append(residual)
    return pl.pallas_call(
        body, name=name,
        out_shape=jax.ShapeDtypeStruct((M, N if out_cols is None else out_cols), out_dtype),
        grid=(M // tm, N // tn, nk),
        in_specs=in_specs, out_specs=o_spec,
        scratch_shapes=[pltpu.VMEM((tm, tn), F32)] if nk > 1 else [],
        compiler_params=_cparams("parallel", "parallel", "arbitrary"),
    )(*args)


def _row_tile(rows, width, n_arrays):
    cap = max(16, VMEM_TILE_BUDGET // (2 * n_arrays * width * 4))
    return _divisor_tile(rows, min(cap, 384), 16)


def _rms_fwd(x, g, *, col0, out_dtype, name):
    T = x.shape[0]
    W = g.shape[-1]
    assert col0 % W == 0
    cb = col0 // W
    tm = _row_tile(T, W, 3)

    def body(x_ref, g_ref, o_ref):
        xv = x_ref[...].astype(F32)
        r = lax.rsqrt(jnp.mean(xv * xv, axis=-1, keepdims=True) + EPS)
        o_ref[...] = ((xv * r) * g_ref[...]).astype(o_ref.dtype)

    return pl.pallas_call(
        body, name=name,
        out_shape=jax.ShapeDtypeStruct((T, W), out_dtype),
        grid=(T // tm,),
        in_specs=[pl.BlockSpec((tm, W), lambda i: (i, cb)), pl.BlockSpec((1, W), lambda i: (0, 0))],
        out_specs=pl.BlockSpec((tm, W), lambda i: (i, 0)),
        compiler_params=_cparams("parallel"),
    )(x, g.reshape(1, W))


def _rms_bwd(x, g, dy, *, col0, out_dtype, name, residual=None):
    T = x.shape[0]
    W = g.shape[-1]
    assert col0 % W == 0
    cb = col0 // W
    tm = _row_tile(T, W, 6)
    nt = T // tm

    def body(*refs):
        if residual is not None:
            x_ref, g_ref, dy_ref, r_ref, dx_ref, dg_ref, acc_ref = refs
        else:
            x_ref, g_ref, dy_ref, dx_ref, dg_ref, acc_ref = refs
            r_ref = None
        i = pl.program_id(0)
        xv = x_ref[...].astype(F32)
        r = lax.rsqrt(jnp.mean(xv * xv, axis=-1, keepdims=True) + EPS)
        xh = xv * r
        dyv = dy_ref[...].astype(F32)
        dxh = dyv * g_ref[...]
        dx = r * (dxh - xh * jnp.mean(dxh * xh, axis=-1, keepdims=True))
        if r_ref is not None:
            dx = dx + r_ref[...]
        dx_ref[...] = dx.astype(dx_ref.dtype)
        part = jnp.sum((dyv * xh).reshape(tm // 8, 8, W), axis=0)

        @pl.when(i == 0)
        def _():
            acc_ref[...] = part

        @pl.when(i > 0)
        def _():
            acc_ref[...] += part

        @pl.when(i == nt - 1)
        def _():
            dg_ref[...] = jnp.sum(acc_ref[...], axis=0, keepdims=True)

    row = pl.BlockSpec((tm, W), lambda i: (i, 0))
    in_specs = [pl.BlockSpec((tm, W), lambda i: (i, cb)), pl.BlockSpec((1, W), lambda i: (0, 0)), row]
    args = [x, g.reshape(1, W), dy]
    if residual is not None:
        in_specs.append(row)
        args.append(residual)
    return pl.pallas_call(
        body, name=name,
        out_shape=(jax.ShapeDtypeStruct((T, W), out_dtype), jax.ShapeDtypeStruct((1, W), F32)),
        grid=(nt,),
        in_specs=in_specs,
        out_specs=(row, pl.BlockSpec((1, W), lambda i: (0, 0))),
        scratch_shapes=[pltpu.VMEM((8, W), F32)],
        compiler_params=_cparams("arbitrary"),
    )(*args)


def _gate_fwd(o, proj, g, *, zcol0, name):
    T, W = o.shape
    assert zcol0 % W == 0
    zb = zcol0 // W
    tm = _row_tile(T, W, 4)

    def body(o_ref, z_ref, g_ref, y_ref):
        ov = o_ref[...]
        r = lax.rsqrt(jnp.mean(ov * ov, axis=-1, keepdims=True) + EPS)
        z = z_ref[...]
        sg = 1.0 / (1.0 + jnp.exp(-z))
        y_ref[...] = (((ov * r) * g_ref[...]) * (z * sg)).astype(y_ref.dtype)

    return pl.pallas_call(
        body, name=name,
        out_shape=jax.ShapeDtypeStruct((T, W), BF16),
        grid=(T // tm,),
        in_specs=[pl.BlockSpec((tm, W), lambda i: (i, 0)), pl.BlockSpec((tm, W), lambda i: (i, zb)),
                  pl.BlockSpec((1, W), lambda i: (0, 0))],
        out_specs=pl.BlockSpec((tm, W), lambda i: (i, 0)),
        compiler_params=_cparams("parallel"),
    )(o, proj, g.reshape(1, W))


def _gate_bwd(dy, o, proj, g, *, grp, zcol0, name):
    T, W = o.shape
    assert zcol0 % W == 0
    zb = zcol0 // W
    tm = _row_tile(T, W, 8)
    nt = T // tm

    def body(dy_ref, o_ref, z_ref, g_ref, do_ref, dz_ref, dg_ref, acc_ref):
        i = pl.program_id(0)
        ov = o_ref[...]
        r = lax.rsqrt(jnp.mean(ov * ov, axis=-1, keepdims=True) + EPS)
        xh = ov * r
        gv = g_ref[...]
        z = z_ref[...]
        sg = 1.0 / (1.0 + jnp.exp(-z))
        dyv = dy_ref[...]
        dn = dyv * (z * sg)
        dz_ref[...] = (dyv * (xh * gv) * (sg * (1.0 + z * (1.0 - sg)))).astype(dz_ref.dtype)
        dxh = dn * gv
        do_ref[...] = r * (dxh - xh * jnp.mean(dxh * xh, axis=-1, keepdims=True))
        part = jnp.sum((dn * xh).reshape(tm // 8, 8, W), axis=0)

        @pl.when(i == 0)
        def _():
            acc_ref[...] = part

        @pl.when(i > 0)
        def _():
            acc_ref[...] += part

        @pl.when(i == nt - 1)
        def _():
            dg_ref[...] = jnp.sum(acc_ref[...], axis=0, keepdims=True)

    row = pl.BlockSpec((tm, W), lambda i: (i, 0))
    return pl.pallas_call(
        body, name=name,
        out_shape=(jax.ShapeDtypeStruct((T, W), F32), jax.ShapeDtypeStruct((T, W), BF16),
                   jax.ShapeDtypeStruct((1, W), F32)),
        grid=(nt,),
        in_specs=[pl.BlockSpec((tm, W), lambda i: (i, grp)), row, pl.BlockSpec((tm, W), lambda i: (i, zb)),
                  pl.BlockSpec((1, W), lambda i: (0, 0))],
        out_specs=(row, row, pl.BlockSpec((1, W), lambda i: (0, 0))),
        scratch_shapes=[pltpu.VMEM((8, W), F32)],
        compiler_params=_cparams("arbitrary"),
    )(dy, o, proj, g.reshape(1, W))


def _rope_fwd(q, kv, proj, cosp, sinp, *, H, kr_col0, name):
    T = q.shape[0]
    HB = H * V7X_LANES
    tm = ROPE_ROWS
    krb = kr_col0 // V7X_LANES

    def body(q_ref, kv_ref, kra_ref, krb_ref, cos_ref, sin_ref, qc_ref, kc_ref, v_ref):
        cos = cos_ref[...]
        sin = sin_ref[...]
        kr = (kra_ref[...] * cos + krb_ref[...] * sin).astype(BF16)
        for h in range(H):
            lo, hi = h * 128, (h + 1) * 128
            qc_ref[:, 2 * lo:2 * lo + 128] = q_ref[:, lo:hi].astype(BF16)
            qc_ref[:, 2 * lo + 128:2 * hi] = (q_ref[:, HB + lo:HB + hi] * cos
                                              + q_ref[:, 2 * HB + lo:2 * HB + hi] * sin).astype(BF16)
            kc_ref[:, 2 * lo:2 * lo + 128] = kv_ref[:, 2 * lo:2 * lo + 128].astype(BF16)
            kc_ref[:, 2 * lo + 128:2 * hi] = kr
            v_ref[:, lo:hi] = kv_ref[:, 2 * lo + 128:2 * hi].astype(BF16)

    tab = pl.BlockSpec((tm, 128), lambda i: (i, 0))
    return pl.pallas_call(
        body, name=name,
        out_shape=(jax.ShapeDtypeStruct((T, 2 * HB), BF16), jax.ShapeDtypeStruct((T, 2 * HB), BF16),
                   jax.ShapeDtypeStruct((T, HB), BF16)),
        grid=(T // tm,),
        in_specs=[pl.BlockSpec((tm, 3 * HB), lambda i: (i, 0)), pl.BlockSpec((tm, 2 * HB), lambda i: (i, 0)),
                  pl.BlockSpec((tm, 128), lambda i: (i, krb)), pl.BlockSpec((tm, 128), lambda i: (i, krb + 1)),
                  tab, tab],
        out_specs=(pl.BlockSpec((tm, 2 * HB), lambda i: (i, 0)), pl.BlockSpec((tm, 2 * HB), lambda i: (i, 0)),
                   pl.BlockSpec((tm, HB), lambda i: (i, 0))),
        compiler_params=_cparams("parallel"),
    )(q, kv, proj, proj, cosp, sinp)


def _rope_bwd(dqc, dkc, dv, cosp, sinp, *, H, name):
    T = dqc.shape[0]
    HB = H * V7X_LANES
    tm = ROPE_ROWS

    def body(dqc_ref, dkc_ref, dv_ref, cos_ref, sin_ref, dq_ref, dkv_ref, dkr_ref):
        cos = cos_ref[...]
        sin = sin_ref[...]
        gk = jnp.zeros((tm, 128), F32)
        for h in range(H):
            lo, hi = h * 128, (h + 1) * 128
            dq_ref[:, lo:hi] = dqc_ref[:, 2 * lo:2 * lo + 128].astype(BF16)
            gq = dqc_ref[:, 2 * lo + 128:2 * hi]
            dq_ref[:, HB + lo:HB + hi] = (gq * cos).astype(BF16)
            dq_ref[:, 2 * HB + lo:2 * HB + hi] = (gq * sin).astype(BF16)
            dkv_ref[:, 2 * lo:2 * lo + 128] = dkc_ref[:, 2 * lo:2 * lo + 128].astype(BF16)
            dkv_ref[:, 2 * lo + 128:2 * hi] = dv_ref[:, lo:hi].astype(BF16)
            gk = gk + dkc_ref[:, 2 * lo + 128:2 * hi]
        dkr_ref[:, 0:128] = (gk * cos).astype(BF16)
        dkr_ref[:, 128:256] = (gk * sin).astype(BF16)

    tab = pl.BlockSpec((tm, 128), lambda i: (i, 0))
    return pl.pallas_call(
        body, name=name,
        out_shape=(jax.ShapeDtypeStruct((T, 3 * HB), BF16), jax.ShapeDtypeStruct((T, 2 * HB), BF16),
                   jax.ShapeDtypeStruct((T, 256), BF16)),
        grid=(T // tm,),
        in_specs=[pl.BlockSpec((tm, 2 * HB), lambda i: (i, 0)), pl.BlockSpec((tm, 2 * HB), lambda i: (i, 0)),
                  pl.BlockSpec((tm, HB), lambda i: (i, 0)), tab, tab],
        out_specs=(pl.BlockSpec((tm, 3 * HB), lambda i: (i, 0)), pl.BlockSpec((tm, 2 * HB), lambda i: (i, 0)),
                   pl.BlockSpec((tm, 256), lambda i: (i, 0))),
        compiler_params=_cparams("parallel"),
    )(dqc, dkc, dv, cosp, sinp)


def _mla_fwd(qc, kc, v, *, H, scale, name, ride=None):
    T = qc.shape[0]
    tq = tk = ATT_BLK
    nq = T // tq

    def body(*refs):
        (q_ref, k_ref, v_ref, o_ref, lse_ref), cps = _ride_unpack(ride, refs, 3, 2)
        i = pl.program_id(1)
        _ride_start(cps, (pl.program_id(0) == 0) & (i == 0))
        q = q_ref[...]
        row = i * tq + lax.broadcasted_iota(jnp.int32, (tq, tk), 0)
        col = lax.broadcasted_iota(jnp.int32, (tq, tk), 1)

        def step(j, carry, masked):
            m, l, acc = carry
            off = pl.multiple_of(j * tk, tk)
            ks = k_ref[pl.ds(off, tk), :]
            vs = v_ref[pl.ds(off, tk), :]
            s = lax.dot_general(q, ks, NT_DIMS, preferred_element_type=F32) * scale
            if masked:
                s = jnp.where(col + j * tk <= row, s, NEG)
            m_new = jnp.maximum(m, jnp.max(s, axis=1, keepdims=True))
            alpha = jnp.exp(m - m_new)
            p = jnp.exp(s - m_new)
            l = alpha * l + jnp.sum(p, axis=1, keepdims=True)
            acc = alpha * acc + jnp.dot(p.astype(BF16), vs, preferred_element_type=F32)
            return m_new, l, acc

        def two_steps(t, carry):
            m, l, acc = carry
            off_a = pl.multiple_of(2 * t * tk, tk)
            off_b = pl.multiple_of((2 * t + 1) * tk, tk)
            s_a = lax.dot_general(q, k_ref[pl.ds(off_a, tk), :], NT_DIMS, preferred_element_type=F32) * scale
            s_b = lax.dot_general(q, k_ref[pl.ds(off_b, tk), :], NT_DIMS, preferred_element_type=F32) * scale
            m_new = jnp.maximum(m, jnp.maximum(jnp.max(s_a, axis=1, keepdims=True),
                                               jnp.max(s_b, axis=1, keepdims=True)))
            alpha = jnp.exp(m - m_new)
            p_a = jnp.exp(s_a - m_new)
            p_b = jnp.exp(s_b - m_new)
            l = alpha * l + (jnp.sum(p_a, axis=1, keepdims=True) + jnp.sum(p_b, axis=1, keepdims=True))
            acc = alpha * acc + (jnp.dot(p_a.astype(BF16), v_ref[pl.ds(off_a, tk), :], preferred_element_type=F32)
                                 + jnp.dot(p_b.astype(BF16), v_ref[pl.ds(off_b, tk), :], preferred_element_type=F32))
            return m_new, l, acc

        n_full = (i * tq) // tk
        n_tot = ((i + 1) * tq + tk - 1) // tk
        carry = (jnp.full((tq, 1), NEG, F32), jnp.zeros((tq, 1), F32), jnp.zeros((tq, 128), F32))
        carry = lax.fori_loop(0, n_full // 2, two_steps, carry)
        carry = lax.fori_loop(2 * (n_full // 2), n_full, functools.partial(step, masked=False), carry)
        m, l, acc = lax.fori_loop(n_full, n_tot, functools.partial(step, masked=True), carry)
        o_ref[...] = acc / l
        lse_ref[0] = m + jnp.log(l)
        _ride_wait(cps, (pl.program_id(0) == H - 1) & (i == nq - 1))

    x_in, x_out, x_shapes, x_scratch, x_args = _ride_parts(ride)
    res = pl.pallas_call(
        body, name=name,
        out_shape=(jax.ShapeDtypeStruct((T, H * 128), F32), jax.ShapeDtypeStruct((H, T, 1), F32), *x_shapes),
        grid=(H, nq),
        in_specs=[pl.BlockSpec((tq, 256), lambda h, i: (i, h)), pl.BlockSpec((T, 256), lambda h, i: (0, h)),
                  pl.BlockSpec((T, 128), lambda h, i: (0, h)), *x_in],
        out_specs=(pl.BlockSpec((tq, 128), lambda h, i: (i, h)), pl.BlockSpec((1, tq, 1), lambda h, i: (h, i, 0)),
                   *x_out),
        scratch_shapes=x_scratch,
        compiler_params=_cparams("arbitrary", "arbitrary"),
    )(qc, kc, v, *x_args)
    return res[0], res[1], res[2:]


def _mla_bwd(qc, kc, v, o, do, lse, *, H, scale, name, pieces=None):
    T = qc.shape[0]
    tq = tk = ATT_BLK
    nq, nk = T // tq, T // tk

    def body(*refs):
        if pieces is not None:
            (q_ref, k_ref, v_ref, o_ref, do_ref, lse_ref, p_ref, dq_ref, dk_ref, dv_ref, got_ref, delta_ref,
             send_sems, recv_sems) = refs
            cps = _scatter_copies(p_ref, got_ref, send_sems, recv_sems)

            @pl.when(pl.program_id(0) == 0)
            def _():
                for cp in cps:
                    cp.start()
        else:
            q_ref, k_ref, v_ref, o_ref, do_ref, lse_ref, dq_ref, dk_ref, dv_ref, delta_ref = refs
        dq_ref[...] = jnp.zeros_like(dq_ref)

        def fill_delta(i, c):
            off = pl.multiple_of(i * tq, tq)
            delta_ref[pl.ds(off, tq), :] = jnp.sum(do_ref[pl.ds(off, tq), :] * o_ref[pl.ds(off, tq), :],
                                                   axis=1, keepdims=True)
            return c

        lax.fori_loop(0, nq, fill_delta, 0)
        rowi = lax.broadcasted_iota(jnp.int32, (tq, tk), 0)
        coli = lax.broadcasted_iota(jnp.int32, (tq, tk), 1)

        def kblock(j, c):
            koff = pl.multiple_of(j * tk, tk)
            ks = k_ref[pl.ds(koff, tk), :]
            vs = v_ref[pl.ds(koff, tk), :]

            def qstep(i, carry, masked):
                dk, dv = carry
                qoff = pl.multiple_of(i * tq, tq)
                qs = q_ref[pl.ds(qoff, tq), :]
                dob = do_ref[pl.ds(qoff, tq), :].astype(BF16)
                s = lax.dot_general(qs, ks, NT_DIMS, preferred_element_type=F32) * scale
                if masked:
                    s = jnp.where(coli + j * tk <= rowi + i * tq, s, NEG)
                p = jnp.exp(s - lse_ref[0, pl.ds(qoff, tq), :])
                dv = dv + lax.dot_general(p.astype(BF16), dob, TN_DIMS, preferred_element_type=F32)
                dp = lax.dot_general(dob, vs, NT_DIMS, preferred_element_type=F32)
                ds = (p * (dp - delta_ref[pl.ds(qoff, tq), :]) * scale).astype(BF16)
                dk = dk + lax.dot_general(ds, qs, TN_DIMS, preferred_element_type=F32)
                dq_ref[pl.ds(qoff, tq), :] += jnp.dot(ds, ks, preferred_element_type=F32)
                return dk, dv

            i0 = (j * tk) // tq
            i1 = jnp.minimum(((j + 1) * tk + tq - 1) // tq, nq)
            carry = (jnp.zeros((tk, 256), F32), jnp.zeros((tk, 128), F32))
            carry = lax.fori_loop(i0, i1, functools.partial(qstep, masked=True), carry)
            pairs = (nq - i1) // 2
            carry = lax.fori_loop(0, pairs, lambda t, c: qstep(i1 + 2 * t + 1, qstep(i1 + 2 * t, c, False), False),
                                  carry)
            dk, dv = lax.fori_loop(i1 + 2 * pairs, nq, functools.partial(qstep, masked=False), carry)
            dk_ref[pl.ds(koff, tk), :] = dk
            dv_ref[pl.ds(koff, tk), :] = dv
            return c

        lax.fori_loop(0, nk, kblock, 0)
        if pieces is not None:
            @pl.when(pl.program_id(0) == H - 1)
            def _():
                for cp in cps:
                    cp.wait()

    wide = pl.BlockSpec((T, 256), lambda h: (0, h))
    narrow = pl.BlockSpec((T, 128), lambda h: (0, h))
    out_shape = [jax.ShapeDtypeStruct((T, H * 256), F32), jax.ShapeDtypeStruct((T, H * 256), F32),
                 jax.ShapeDtypeStruct((T, H * 128), F32)]
    in_specs = [wide, wide, narrow, narrow, narrow, pl.BlockSpec((1, T, 1), lambda h: (h, 0, 0))]
    out_specs = [wide, wide, narrow]
    scratch = [pltpu.VMEM((T, 1), F32)]
    args = [qc, kc, v, o, do, lse]
    if pieces is not None:
        out_shape.append(jax.ShapeDtypeStruct((3,) + pieces.shape[1:], pieces.dtype))
        in_specs.append(ANY)
        out_specs.append(ANY)
        scratch += [pltpu.SemaphoreType.DMA((3,)), pltpu.SemaphoreType.DMA((3,))]
        args.append(pieces)
    return pl.pallas_call(
        body, name=name,
        out_shape=tuple(out_shape),
        grid=(H,),
        in_specs=in_specs, out_specs=tuple(out_specs),
        scratch_shapes=scratch,
        compiler_params=_cparams("arbitrary"),
    )(*args)


def _log_sigmoid_pair(z):
    e = jnp.exp(-jnp.abs(z))
    lb = jnp.minimum(z, 0.0) - jnp.log(1.0 + e)
    inv = 1.0 / (1.0 + e)
    sg = jnp.where(z >= 0.0, inv, e * inv)
    return lb, lb - z, sg


def _tri_dot(x, tri):
    hi = x.astype(BF16)
    lo = (x - hi.astype(F32)).astype(BF16)
    return jnp.dot(hi, tri, preferred_element_type=F32) + jnp.dot(lo, tri, preferred_element_type=F32)


def _sb_fwd(proj, *, H, qcol0, kcol0, vcol0, scale, name, ride=None):
    T = proj.shape[0]
    tq = tk = ATT_BLK
    nq = T // tq
    qb, kb, vb = qcol0 // 128, kcol0 // 128, vcol0 // 128

    assert T // tk <= V7X_LANES

    def body(*refs):
        (q_ref, k_ref, v_ref, y_ref, rems_ref), cps = _ride_unpack(ride, refs, 3, 2)
        i = pl.program_id(1)
        _ride_start(cps, (pl.program_id(0) == 0) & (i == 0))
        q = q_ref[...].astype(BF16)
        row = i * tq + lax.broadcasted_iota(jnp.int32, (tq, tk), 0)
        col = lax.broadcasted_iota(jnp.int32, (tq, tk), 1)
        r_i = lax.broadcasted_iota(jnp.int32, (tk, tk), 0)
        c_i = lax.broadcasted_iota(jnp.int32, (tk, tk), 1)
        tri_after = (r_i > c_i).astype(BF16)
        lane = lax.broadcasted_iota(jnp.int32, (tq, V7X_LANES), 1)

        def step(j, carry, masked):
            rem, acc, tab = carry
            off = pl.multiple_of(j * tk, tk)
            ks = k_ref[pl.ds(off, tk), :].astype(BF16)
            vs = v_ref[pl.ds(off, tk), :].astype(BF16)
            z = lax.dot_general(q, ks, NT_DIMS, preferred_element_type=F32) * scale
            lb, lom, _ = _log_sigmoid_pair(z)
            if masked:
                valid = col + j * tk < row
                lom = jnp.where(valid, lom, 0.0)
            a = jnp.exp(lb + _tri_dot(lom, tri_after) + rem)
            if masked:
                a = jnp.where(valid, a, 0.0)
            acc = acc + jnp.dot(a.astype(BF16), vs, preferred_element_type=F32)
            rem = rem + jnp.sum(lom, axis=1, keepdims=True)
            return rem, acc, jnp.where(lane == j, rem, tab)

        n_full = (i * tq) // tk
        n_tot = ((i + 1) * tq + tk - 1) // tk
        carry = (jnp.zeros((tq, 1), F32), jnp.zeros((tq, 128), F32), jnp.full((tq, V7X_LANES), NEG, F32))
        carry = lax.fori_loop(0, n_tot - n_full, lambda idx, c: step(n_tot - 1 - idx, c, True), carry)

        def alive(rem):
            return (jnp.max(rem) >= SB_DEAD).astype(jnp.int32)

        def more(state):
            idx, live, _ = state
            return (idx < n_full) & (live > 0)

        def back(state):
            idx, _, c = state
            c = step(n_full - 1 - idx, c, False)
            return idx + 1, alive(c[0]), c

        _, _, (rem, acc, tab) = lax.while_loop(more, back, (jnp.int32(0), alive(carry[0]), carry))
        y_ref[...] = acc
        rems_ref[0] = tab
        _ride_wait(cps, (pl.program_id(0) == H - 1) & (i == nq - 1))

    x_in, x_out, x_shapes, x_scratch, x_args = _ride_parts(ride)
    res = pl.pallas_call(
        body, name=name,
        out_shape=(jax.ShapeDtypeStruct((T, H * 128), F32), jax.ShapeDtypeStruct((H, T, V7X_LANES), F32), *x_shapes),
        grid=(H, nq),
        in_specs=[pl.BlockSpec((tq, 128), lambda h, i: (i, qb + h)), pl.BlockSpec((T, 128), lambda h, i: (0, kb + h)),
                  pl.BlockSpec((T, 128), lambda h, i: (0, vb + h)), *x_in],
        out_specs=(pl.BlockSpec((tq, 128), lambda h, i: (i, h)),
                   pl.BlockSpec((1, tq, V7X_LANES), lambda h, i: (h, i, 0)), *x_out),
        scratch_shapes=x_scratch,
        compiler_params=_cparams("arbitrary", "arbitrary"),
    )(proj, proj, proj, *x_args)
    return res[0], res[1], res[2:]


def _sb_bwd(proj, dy, rems, *, H, qcol0, kcol0, vcol0, scale, name, ride=None):
    T = proj.shape[0]
    tq = tk = ATT_BLK
    nq = T // tq
    qb, kb, vb = qcol0 // 128, kcol0 // 128, vcol0 // 128

    def body(*refs):
        (q_ref, k_ref, v_ref, dy_ref, rems_ref, dq_ref, dk_ref, dv_ref, dk_acc, dv_acc), cps = _ride_unpack(
            ride, refs, 5, 3)
        _ride_start(cps, pl.program_id(0) == 0)
        dk_acc[...] = jnp.zeros_like(dk_acc)
        dv_acc[...] = jnp.zeros_like(dv_acc)
        rowi = lax.broadcasted_iota(jnp.int32, (tq, tk), 0)
        coli = lax.broadcasted_iota(jnp.int32, (tq, tk), 1)
        r_i = lax.broadcasted_iota(jnp.int32, (tk, tk), 0)
        c_i = lax.broadcasted_iota(jnp.int32, (tk, tk), 1)
        tri_upto = (r_i <= c_i).astype(BF16)
        tri_before = (r_i < c_i).astype(BF16)
        lane = lax.broadcasted_iota(jnp.int32, (tq, V7X_LANES), 1)
        lane1 = lax.broadcasted_iota(jnp.int32, (1, V7X_LANES), 1)

        def qblock(i, c):
            qoff = pl.multiple_of(i * tq, tq)
            qs = q_ref[pl.ds(qoff, tq), :].astype(BF16)
            dyb = dy_ref[pl.ds(qoff, tq), :].astype(BF16)
            tab = rems_ref[0, pl.ds(qoff, tq), :]

            def step(j, carry, masked):
                pre, dq = carry
                rem = jnp.sum(jnp.where(lane == j, tab, 0.0), axis=1, keepdims=True)
                koff = pl.multiple_of(j * tk, tk)
                ks = k_ref[pl.ds(koff, tk), :].astype(BF16)
                vs = v_ref[pl.ds(koff, tk), :].astype(BF16)
                z = lax.dot_general(qs, ks, NT_DIMS, preferred_element_type=F32) * scale
                lb, lom, sg = _log_sigmoid_pair(z)
                if masked:
                    valid = coli + j * tk < rowi + i * tq
                    lom = jnp.where(valid, lom, 0.0)
                a = jnp.exp(lb + rem - _tri_dot(lom, tri_upto))
                if masked:
                    a = jnp.where(valid, a, 0.0)
                dv_acc[pl.ds(koff, tk), :] += lax.dot_general(a.astype(BF16), dyb, TN_DIMS,
                                                              preferred_element_type=F32)
                de = a * lax.dot_general(dyb, vs, NT_DIMS, preferred_element_type=F32)
                before = pre + _tri_dot(de, tri_before)
                dz = de * (1.0 - sg) - before * sg
                if masked:
                    dz = jnp.where(valid, dz, 0.0)
                dzb = (dz * scale).astype(BF16)
                dk_acc[pl.ds(koff, tk), :] += lax.dot_general(dzb, qs, TN_DIMS, preferred_element_type=F32)
                dq = dq + jnp.dot(dzb, ks, preferred_element_type=F32)
                return pre + jnp.sum(de, axis=1, keepdims=True), dq

            n_full = (i * tq) // tk
            n_tot = ((i + 1) * tq + tk - 1) // tk
            colmax = jnp.max(tab, axis=0, keepdims=True)
            dead = (lane1 >= 1) & (lane1 <= n_full) & (colmax < SB_DEAD)
            j0 = jnp.sum(dead.astype(jnp.int32))
            carry = (jnp.zeros((tq, 1), F32), jnp.zeros((tq, 128), F32))
            carry = lax.fori_loop(j0, n_full, functools.partial(step, masked=False), carry)
            _, dq = lax.fori_loop(n_full, n_tot, functools.partial(step, masked=True), carry)
            dq_ref[pl.ds(qoff, tq), :] = dq.astype(dq_ref.dtype)
            return c

        lax.fori_loop(0, nq, qblock, 0)
        dk_ref[...] = dk_acc[...].astype(dk_ref.dtype)
        dv_ref[...] = dv_acc[...].astype(dv_ref.dtype)
        _ride_wait(cps, pl.program_id(0) == H - 1)

    def seg(b):
        return pl.BlockSpec((T, 128), lambda h: (0, b + h))

    out = pl.BlockSpec((T, 128), lambda h: (0, h))
    x_in, x_out, x_shapes, x_scratch, x_args = _ride_parts(ride)
    res = pl.pallas_call(
        body, name=name,
        out_shape=(*(jax.ShapeDtypeStruct((T, H * 128), BF16) for _ in range(3)), *x_shapes),
        grid=(H,),
        in_specs=[seg(qb), seg(kb), seg(vb), out, pl.BlockSpec((1, T, V7X_LANES), lambda h: (h, 0, 0)), *x_in],
        out_specs=(out, out, out, *x_out),
        scratch_shapes=[pltpu.VMEM((T, 128), F32), pltpu.VMEM((T, 128), F32), *x_scratch],
        compiler_params=_cparams("arbitrary"),
    )(proj, proj, proj, dy, rems, *x_args)
    return res[0], res[1], res[2], res[3:]


def _final_loss(h, g, target, *, row0, n_rows, name):
    T, D = h.shape
    tm = _row_tile(T, D, 6)
    nt = T // tm

    def body(h_ref, g_ref, t_ref, dh_ref, dg_ref, loss_ref, acc_ref, lacc_ref):
        i = pl.program_id(0)
        xv = h_ref[...]
        r = lax.rsqrt(jnp.mean(xv * xv, axis=-1, keepdims=True) + EPS)
        xh = xv * r
        gv = g_ref[...]
        rows = i * tm + lax.broadcasted_iota(jnp.int32, (tm, 1), 0)
        valid = (rows >= row0) & (rows < row0 + n_rows)
        err = jnp.where(valid, xh * gv - t_ref[...], 0.0)
        dout = err * (1.0 / D)
        dxh = dout * gv
        dh_ref[...] = r * (dxh - xh * jnp.mean(dxh * xh, axis=-1, keepdims=True))
        part = jnp.sum((dout * xh).reshape(tm // 8, 8, D), axis=0)
        lpart = jnp.sum((err * err).reshape(tm // 8, 8, D), axis=0)

        @pl.when(i == 0)
        def _():
            acc_ref[...] = part
            lacc_ref[...] = lpart

        @pl.when(i > 0)
        def _():
            acc_ref[...] += part
            lacc_ref[...] += lpart

        @pl.when(i == nt - 1)
        def _():
            dg_ref[...] = jnp.sum(acc_ref[...], axis=0, keepdims=True)
            loss_ref[...] = (0.5 / D) * jnp.sum(jnp.sum(lacc_ref[...], axis=0, keepdims=True), axis=1, keepdims=True)

    row = pl.BlockSpec((tm, D), lambda i: (i, 0))
    vec = pl.BlockSpec((1, D), lambda i: (0, 0))
    return pl.pallas_call(
        body, name=name,
        out_shape=(jax.ShapeDtypeStruct((T, D), F32), jax.ShapeDtypeStruct((1, D), F32),
                   jax.ShapeDtypeStruct((1, 1), F32)),
        grid=(nt,),
        in_specs=[row, vec, row],
        out_specs=(row, vec, pl.BlockSpec((1, 1), lambda i: (0, 0))),
        scratch_shapes=[pltpu.VMEM((8, D), F32), pltpu.VMEM((8, D), F32)],
        compiler_params=_cparams("arbitrary"),
    )(h, g.reshape(1, D), target)


def _elementwise(fn, args, out_dtypes, name, ride=None):
    shape = args[0].shape
    C = shape[-1]
    R = math.prod(shape[:-1])
    n = len(args) + len(out_dtypes)
    cap = max(16, (VMEM_TILE_BUDGET // 2) // (2 * n * C * 4))
    tr = _divisor_tile(R, cap, 16)
    n_in, n_out = len(args), len(out_dtypes)
    steps = R // tr

    def body(*refs):
        own, cps = _ride_unpack(ride, refs, n_in, n_out)
        _ride_start(cps, pl.program_id(0) == 0)
        outs = fn(*[r[...] for r in own[:n_in]])
        for o_ref, val in zip(own[n_in:], outs):
            o_ref[...] = val.astype(o_ref.dtype)
        _ride_wait(cps, pl.program_id(0) == steps - 1)

    spec = pl.BlockSpec((tr, C), lambda i: (i, 0))
    x_in, x_out, x_shapes, x_scratch, x_args = _ride_parts(ride)
    res = pl.pallas_call(
        body, name=name,
        out_shape=(*(jax.ShapeDtypeStruct((R, C), dt) for dt in out_dtypes), *x_shapes),
        grid=(steps,),
        in_specs=[spec] * n_in + x_in, out_specs=(*([spec] * n_out), *x_out),
        scratch_shapes=x_scratch,
        compiler_params=_cparams("parallel" if ride is None else "arbitrary"),
    )(*[a.reshape(R, C) for a in args], *x_args)
    return (*(r.reshape(shape) for r in res[:n_out]), *res[n_out:])


def _adamw_math(w, g, m, v):
    m = ADAM_B1 * m + (1.0 - ADAM_B1) * g
    v = ADAM_B2 * v + (1.0 - ADAM_B2) * (g * g)
    m_hat = m / (1.0 - ADAM_B1 ** ADAM_STEP)
    v_hat = v / (1.0 - ADAM_B2 ** ADAM_STEP)
    delta = -ADAM_LR * (m_hat / (jnp.sqrt(v_hat) + ADAM_EPS) + ADAM_WD * w)
    return delta, m, v


def _adamw(w, g, m, v, name, ride=None):
    return _elementwise(_adamw_math, [w, g, m, v], [F32, F32, F32], name, ride=ride)


ANY = pl.BlockSpec(memory_space=pl.ANY)


def _position():
    return lax.axis_index("x"), lax.axis_index("y"), lax.axis_index("c")


class _Ride:
    def __init__(self, ins, out_shapes, n, make):
        self.ins, self.out_shapes, self.n, self.make = list(ins), list(out_shapes), n, make


def _ride_parts(ride):
    if ride is None:
        return [], [], [], [], []
    sems = [pltpu.SemaphoreType.DMA((ride.n,)), pltpu.SemaphoreType.DMA((ride.n,))]
    return [ANY] * len(ride.ins), [ANY] * len(ride.out_shapes), ride.out_shapes, sems, ride.ins


def _ride_unpack(ride, refs, n_in, n_out):
    if ride is None:
        return refs, []
    a, b = len(ride.ins), len(ride.out_shapes)
    own = refs[:n_in] + refs[n_in + a:n_in + a + n_out] + refs[n_in + a + n_out + b:-2]
    cps = ride.make(refs[n_in:n_in + a], refs[n_in + a + n_out:n_in + a + n_out + b], refs[-2], refs[-1])
    return own, cps


def _ride_start(cps, first):
    if cps:
        @pl.when(first)
        def _():
            for cp in cps:
                cp.start()


def _ride_wait(cps, last):
    if cps:
        @pl.when(last)
        def _():
            for cp in cps:
                cp.wait()


def _gather_ride_over_ici(halves):
    def make(ins, outs, send_sems, recv_sems):
        x, y, c = _position()
        chips = [(1 - x, y), (x, 1 - y), (1 - x, 1 - y)]
        return [pltpu.make_async_remote_copy(src_ref=a.at[c], dst_ref=o.at[2 * x + y],
                                             send_sem=send_sems.at[3 * t + k], recv_sem=recv_sems.at[3 * t + k],
                                             device_id=(qx, qy, c), device_id_type=MESH)
                for t, (a, o) in enumerate(zip(ins, outs)) for k, (qx, qy) in enumerate(chips)]

    shapes = [jax.ShapeDtypeStruct((4,) + a.shape[1:], a.dtype) for a in halves]
    return _Ride(halves, shapes, 3 * len(halves), make)


def _gather_ride_to_sibling(landed):
    def make(ins, outs, send_sems, recv_sems):
        x, y, c = _position()
        chips = [(1 - x, y), (x, 1 - y), (1 - x, 1 - y)]
        return [pltpu.make_async_remote_copy(src_ref=a.at[2 * qx + qy], dst_ref=o.at[2 * qx + qy],
                                             send_sem=send_sems.at[3 * t + k], recv_sem=recv_sems.at[3 * t + k],
                                             device_id=(x, y, 1 - c), device_id_type=MESH)
                for t, (a, o) in enumerate(zip(ins, outs)) for k, (qx, qy) in enumerate(chips)]

    shapes = [jax.ShapeDtypeStruct(a.shape, a.dtype) for a in landed]
    return _Ride(landed, shapes, 3 * len(landed), make)


def _assemble_shards(own, mine, other):
    x, y, c = _position()
    p = 2 * x + y
    out = []
    for q in range(4):
        full = jnp.where(c == 0, jnp.concatenate([mine[q], other[q]], axis=0),
                         jnp.concatenate([other[q], mine[q]], axis=0))
        out.append(jnp.where(p == q, own, full))
    return out


def _gather_shards(a, name):
    def body(a_ref, o_ref, send_sems, recv_sems):
        x, y, c = _position()
        p = 2 * x + y
        chips = [(1 - x, y), (x, 1 - y), (1 - x, 1 - y)]

        def copy(k, src, dst, to):
            return pltpu.make_async_remote_copy(src_ref=src, dst_ref=dst, send_sem=send_sems.at[k],
                                                recv_sem=recv_sems.at[k], device_id=to, device_id_type=MESH)

        first = [copy(k, a_ref.at[c], o_ref.at[p, c], (qx, qy, c)) for k, (qx, qy) in enumerate(chips)]
        for cp in first:
            cp.start()
        passed = []
        for k, (qx, qy) in enumerate(chips):
            land = o_ref.at[2 * qx + qy, c]
            copy(k, land, land, (x, y, c)).wait_recv()
            fwd = copy(3 + k, land, land, (x, y, 1 - c))
            fwd.start()
            passed.append(fwd)
        for k, (qx, qy) in enumerate(chips):
            land = o_ref.at[2 * qx + qy, 1 - c]
            copy(3 + k, land, land, (x, y, c)).wait_recv()
        for cp in first + passed:
            cp.wait_send()

    return pl.pallas_call(
        body, name=name,
        out_shape=jax.ShapeDtypeStruct((4,) + a.shape, a.dtype),
        in_specs=[ANY], out_specs=ANY,
        scratch_shapes=[pltpu.SemaphoreType.DMA((6,)), pltpu.SemaphoreType.DMA((6,))],
    )(a)


def _with_own(gathered, own):
    x, y, _ = _position()
    p = 2 * x + y
    return [jnp.where(p == q, own, gathered[q]) for q in range(4)]


def _swap_halves(g, name):
    def body(g_ref, o_ref, send_sem, recv_sem):
        x, y, c = _position()
        cp = pltpu.make_async_remote_copy(src_ref=g_ref.at[1 - c], dst_ref=o_ref, send_sem=send_sem,
                                          recv_sem=recv_sem, device_id=(x, y, 1 - c), device_id_type=MESH)
        cp.start()
        cp.wait()

    return pl.pallas_call(
        body, name=name,
        out_shape=jax.ShapeDtypeStruct(g.shape[1:], g.dtype),
        in_specs=[ANY], out_specs=ANY,
        scratch_shapes=[pltpu.SemaphoreType.DMA, pltpu.SemaphoreType.DMA],
    )(g)


def _scatter_copies(p_ref, o_ref, send_sems, recv_sems):
    x, y, c = _position()
    chips = [(1 - x, y), (x, 1 - y), (1 - x, 1 - y)]
    return [pltpu.make_async_remote_copy(src_ref=p_ref.at[2 * qx + qy], dst_ref=o_ref.at[k],
                                         send_sem=send_sems.at[k], recv_sem=recv_sems.at[k],
                                         device_id=(qx, qy, c), device_id_type=MESH)
            for k, (qx, qy) in enumerate(chips)]


def _scatter_to_chips(pb, name):
    def body(p_ref, o_ref, send_sems, recv_sems):
        cps = _scatter_copies(p_ref, o_ref, send_sems, recv_sems)
        for cp in cps:
            cp.start()
        for cp in cps:
            cp.wait()

    return pl.pallas_call(
        body, name=name,
        out_shape=jax.ShapeDtypeStruct((3,) + pb.shape[1:], pb.dtype),
        in_specs=[ANY], out_specs=ANY,
        scratch_shapes=[pltpu.SemaphoreType.DMA((3,)), pltpu.SemaphoreType.DMA((3,))],
    )(pb)


def _join_halves(r, name):
    def body(r_ref, o_ref, send_sem, recv_sem):
        x, y, c = _position()
        cp = pltpu.make_async_remote_copy(src_ref=r_ref, dst_ref=o_ref, send_sem=send_sem,
                                          recv_sem=recv_sem, device_id=(x, y, 1 - c), device_id_type=MESH)
        cp.start()
        cp.wait()

    other = pl.pallas_call(
        body, name=name,
        out_shape=jax.ShapeDtypeStruct(r.shape, r.dtype),
        in_specs=[ANY], out_specs=ANY,
        scratch_shapes=[pltpu.SemaphoreType.DMA, pltpu.SemaphoreType.DMA],
    )(r)
    c = lax.axis_index("c")
    return jnp.stack([jnp.where(c == 0, r, other), jnp.where(c == 0, other, r)], axis=0)


def _reduce_to_shard(gh, tag):
    psum, pb = _reduce_to_shard_begin(gh, tag)
    return _reduce_to_shard_end(psum, _scatter_to_chips(pb, f"scatter_{tag}"), tag)


def _reduce_to_shard_begin(gh, tag, ride=None):
    c = lax.axis_index("c")
    sib = _swap_halves(gh, f"swap_{tag}")
    mine = lax.dynamic_index_in_dim(gh, c, 0, keepdims=False)
    return _elementwise(lambda a, b: (a + b, a + b), [mine, sib], [F32, BF16], f"pairsum_{tag}", ride=ride)


def _reduce_to_shard_end(psum, got, tag):
    x, y, _ = _position()
    own = lax.dynamic_index_in_dim(psum, 2 * x + y, 0, keepdims=False)
    (red,) = _elementwise(lambda o, a, b, d: (((o + a.astype(F32)) + b.astype(F32)) + d.astype(F32),),
                          [own, got[0], got[1], got[2]], [F32], f"chipsum_{tag}")
    return _join_halves(red, f"join_{tag}")


def _scatter_ride(pieces):
    def make(ins, outs, send_sems, recv_sems):
        return _scatter_copies(ins[0], outs[0], send_sems, recv_sems)

    return _Ride([pieces], [jax.ShapeDtypeStruct((3,) + pieces.shape[1:], pieces.dtype)], 3, make)


def _swap_rows(g, name):
    K, N = g.shape
    Kh = K // 2

    def body(g_ref, o_ref, send_sem, recv_sem):
        x, y, c = _position()
        theirs = g_ref.at[pl.ds(pl.multiple_of((1 - c) * Kh, 8), Kh)]
        cp = pltpu.make_async_remote_copy(src_ref=theirs, dst_ref=o_ref, send_sem=send_sem,
                                          recv_sem=recv_sem, device_id=(x, y, 1 - c), device_id_type=MESH)
        cp.start()
        cp.wait()

    return pl.pallas_call(
        body, name=name,
        out_shape=jax.ShapeDtypeStruct((Kh, N), g.dtype),
        in_specs=[ANY], out_specs=ANY,
        scratch_shapes=[pltpu.SemaphoreType.DMA, pltpu.SemaphoreType.DMA],
    )(g)


def _swap_rows_ride(g):
    K, N = g.shape
    Kh = K // 2

    def make(ins, outs, send_sems, recv_sems):
        x, y, c = _position()
        theirs = ins[0].at[pl.ds(pl.multiple_of((1 - c) * Kh, 8), Kh)]
        return [pltpu.make_async_remote_copy(src_ref=theirs, dst_ref=outs[0], send_sem=send_sems.at[0],
                                             recv_sem=recv_sems.at[0], device_id=(x, y, 1 - c),
                                             device_id_type=MESH)]

    return _Ride([g], [jax.ShapeDtypeStruct((Kh, N), g.dtype)], 1, make)


def _pairsum_rows(g, sib, name):
    K, N = g.shape
    Kh = K // 2
    cap = max(16, (VMEM_TILE_BUDGET // 2) // (2 * 4 * N * 4))
    tr = _divisor_tile(Kh, cap, 16)
    nb = Kh // tr

    def body(c_ref, g_ref, s_ref, p_ref, pb_ref):
        s = g_ref[...] + s_ref[...]
        p_ref[...] = s
        pb_ref[...] = s.astype(BF16)

    mine = pl.BlockSpec((tr, N), lambda i, c_ref: (i + c_ref[0] * nb, 0))
    row = pl.BlockSpec((tr, N), lambda i, c_ref: (i, 0))
    return pl.pallas_call(
        body, name=name,
        out_shape=(jax.ShapeDtypeStruct((Kh, N), F32), jax.ShapeDtypeStruct((Kh, N), BF16)),
        grid_spec=pltpu.PrefetchScalarGridSpec(num_scalar_prefetch=1, grid=(nb,), in_specs=[mine, row],
                                               out_specs=(row, row)),
        compiler_params=_cparams("parallel"),
    )(lax.axis_index("c").astype(jnp.int32).reshape(1), g, sib)


def _reduce_rows_begin(g, sib, tag):
    K, N = g.shape
    n = N // 4
    x, y, _ = _position()
    psum, pb = _pairsum_rows(g, sib, f"pairsum_{tag}")
    pieces = pb.reshape(K // 2, 4, n).transpose(1, 0, 2)
    own = lax.dynamic_slice_in_dim(psum, (2 * x + y) * n, n, axis=1)
    return pieces, own


def _reduce_rows_end(own, got, tag):
    (red,) = _elementwise(lambda o, a, b, d: (((o + a.astype(F32)) + b.astype(F32)) + d.astype(F32),),
                          [own, got[0], got[1], got[2]], [F32], f"chipsum_{tag}")
    return _join_halves(red, f"join_{tag}").reshape(2 * own.shape[0], own.shape[1])


def _all_reduce_small(vec, name):
    R = vec.shape[0]

    def body(v_ref, o_ref, land_ref, send_sems, recv_sems):
        x, y, c = _position()
        me = 4 * x + 2 * y + c
        land_ref[me] = v_ref[...]
        cps = []
        for r in range(1, 8):
            rx, ry, rc = (r >> 2) & 1, (r >> 1) & 1, r & 1
            to = (x ^ rx, y ^ ry, c ^ rc)
            cps.append(pltpu.make_async_remote_copy(src_ref=v_ref, dst_ref=land_ref.at[me],
                                                    send_sem=send_sems.at[r - 1], recv_sem=recv_sems.at[r - 1],
                                                    device_id=to, device_id_type=MESH))
        for cp in cps:
            cp.start()
        for cp in cps:
            cp.wait()
        total = land_ref[0]
        for d in range(1, 8):
            total = total + land_ref[d]
        o_ref[...] = total

    return pl.pallas_call(
        body, name=name,
        out_shape=jax.ShapeDtypeStruct((R, 128), F32),
        in_specs=[pl.BlockSpec(memory_space=pltpu.VMEM)], out_specs=pl.BlockSpec(memory_space=pltpu.VMEM),
        scratch_shapes=[pltpu.VMEM((8, R, 128), F32), pltpu.SemaphoreType.DMA((7,)), pltpu.SemaphoreType.DMA((7,))],
    )(vec)


def _rot(w):
    half = ROPE_DIM // 2
    return jnp.concatenate([-w[..., half:], w[..., :half]], axis=-1)


def _unrot(g):
    half = ROPE_DIM // 2
    return jnp.concatenate([g[..., half:], -g[..., :half]], axis=-1)


class _Layout:
    def __init__(self, D, QL, KVL):
        self.D, self.QL, self.KVL = D, QL, KVL
        self.H = D // 256
        self.WG = self.H * 128
        WG = self.WG
        self.z_mla, self.q_sb, self.k_sb, self.v_sb, self.z_sb = 0, WG, 2 * WG, 3 * WG, 4 * WG
        self.c_q = 5 * WG
        self.c_kv = self.c_q + QL
        self.k_r = self.c_kv + KVL
        self.width = -(-(self.k_r + 256) // 512) * 512
        self.orig = (QL, KVL, ROPE_DIM, WG, WG, WG, WG, WG)
        self.din = sum(self.orig)

    def pack_w_in(self, w):
        cuts = []
        o = 0
        for s in self.orig:
            cuts.append(w[:, o:o + s])
            o += s
        c_q, c_kv, k_r, z_mla, q_sb, k_sb, v_sb, z_sb = cuts
        z64 = jnp.zeros((w.shape[0], 128 - ROPE_DIM), w.dtype)
        pad = jnp.zeros((w.shape[0], self.width - self.k_r - 256), w.dtype)
        return jnp.concatenate([z_mla, q_sb, k_sb, v_sb, z_sb, c_q, c_kv, k_r, z64, _rot(k_r), z64, pad], axis=1)

    def pack_w_uq(self, w):
        H = self.H
        w3 = w.reshape(w.shape[0], H, 128 + ROPE_DIM)
        nope = w3[:, :, :128]
        r = w3[:, :, 128:]
        z = jnp.zeros(r.shape, w.dtype)
        a = jnp.concatenate([r, z], axis=-1)
        b = jnp.concatenate([_rot(r), z], axis=-1)
        return jnp.concatenate([nope.reshape(-1, H * 128), a.reshape(-1, H * 128), b.reshape(-1, H * 128)], axis=1)

    def unpack_dw_uq(self, g):
        H = self.H
        HB = H * 128
        nope = g[:, :HB].reshape(-1, H, 128)
        a = g[:, HB:2 * HB].reshape(-1, H, 128)[:, :, :ROPE_DIM]
        b = g[:, 2 * HB:].reshape(-1, H, 128)[:, :, :ROPE_DIM]
        return jnp.concatenate([nope, a + _unrot(b)], axis=-1).reshape(-1, H * (128 + ROPE_DIM))


def _rope_tables(T):
    inv_freq = ROPE_THETA ** (-jnp.arange(0, ROPE_DIM, 2, dtype=F32) / ROPE_DIM)
    ang = jnp.arange(T, dtype=jnp.int32).astype(F32)[:, None] * inv_freq[None, :]
    z = jnp.zeros((T, 128 - ROPE_DIM), F32)
    cos, sin = jnp.cos(ang), jnp.sin(ang)
    return jnp.concatenate([cos, cos, z], axis=1), jnp.concatenate([sin, sin, z], axis=1)


def _layer_fwd(h, wl, lay, tabs, nxt):
    g_norm, w_in, g_q, g_kv, w_uq, w_ukv, g_mla, g_sb, w_o = wl
    cosp, sinp = tabs
    H = lay.H
    u = _rms_fwd(h, g_norm, col0=0, out_dtype=BF16, name="rms_h")
    proj = _matmul(u, w_in, mode="nn", out_dtype=F32, name="mm_in")
    cqn = _rms_fwd(proj, g_q, col0=lay.c_q, out_dtype=BF16, name="rms_cq")
    ckvn = _rms_fwd(proj, g_kv, col0=lay.c_kv, out_dtype=BF16, name="rms_ckv")
    q = _matmul(cqn, w_uq, mode="nn", out_dtype=F32, name="mm_uq")
    kv = _matmul(ckvn, w_ukv, mode="nn", out_dtype=F32, name="mm_ukv")
    qc, kc, v = _rope_fwd(q, kv, proj, cosp, sinp, H=H, kr_col0=lay.k_r, name="rope_fwd")
    if nxt is None:
        o_mla, lse, _ = _mla_fwd(qc, kc, v, H=H, scale=1.0 / math.sqrt(128 + ROPE_DIM), name="mla_fwd")
        o_sb, rems, _ = _sb_fwd(proj, H=H, qcol0=lay.q_sb, kcol0=lay.k_sb, vcol0=lay.v_sb,
                                scale=1.0 / math.sqrt(128), name="sb_fwd")
        gathered = None
    else:
        o_mla, lse, mine = _mla_fwd(qc, kc, v, H=H, scale=1.0 / math.sqrt(128 + ROPE_DIM), name="mla_fwd_gather",
                                    ride=_gather_ride_over_ici(nxt))
        o_sb, rems, other = _sb_fwd(proj, H=H, qcol0=lay.q_sb, kcol0=lay.k_sb, vcol0=lay.v_sb,
                                    scale=1.0 / math.sqrt(128), name="sb_fwd_gather",
                                    ride=_gather_ride_to_sibling(mine))
        gathered = (mine, other)
    y_mla = _gate_fwd(o_mla, proj, g_mla, zcol0=lay.z_mla, name="gate_fwd_mla")
    y_sb = _gate_fwd(o_sb, proj, g_sb, zcol0=lay.z_sb, name="gate_fwd_sb")
    y = jnp.concatenate([y_mla, y_sb], axis=1)
    h_out = _matmul(y, w_o, mode="nn", out_dtype=F32, name="mm_o", residual=h)
    saved = (h, u, proj, cqn, ckvn, qc, kc, v, o_mla, lse, o_sb, rems, y)
    return h_out, saved, gathered


def _layer_bwd(dh, saved, wl, lay, tabs, carry):
    g_norm, w_in, g_q, g_kv, w_uq, w_ukv, g_mla, g_sb, w_o = wl
    h, u, proj, cqn, ckvn, qc, kc, v, o_mla, lse, o_sb, rems, y = saved
    cosp, sinp = tabs
    H = lay.H
    dy = _matmul(dh, w_o, mode="nt", out_dtype=F32, name="mm_o_dx")
    d_w_o = _matmul(y, dh, mode="tn", out_dtype=F32, name="mm_o_dw")
    do_mla, dz_mla, dg_mla = _gate_bwd(dy, o_mla, proj, g_mla, grp=0, zcol0=lay.z_mla, name="gate_bwd_mla")
    do_sb, dz_sb, dg_sb = _gate_bwd(dy, o_sb, proj, g_sb, grp=1, zcol0=lay.z_sb, name="gate_bwd_sb")
    if carry is None:
        dq_sb, dk_sb, dv_sb, _ = _sb_bwd(proj, do_sb, rems, H=H, qcol0=lay.q_sb, kcol0=lay.k_sb, vcol0=lay.v_sb,
                                         scale=1.0 / math.sqrt(128), name="sb_bwd")
        dqc, dkc, dv = _mla_bwd(qc, kc, v, o_mla, do_mla, lse, H=H, scale=1.0 / math.sqrt(128 + ROPE_DIM),
                                name="mla_bwd")
        g_w_in_above = None
    else:
        dq_sb, dk_sb, dv_sb, (sib,) = _sb_bwd(proj, do_sb, rems, H=H, qcol0=lay.q_sb, kcol0=lay.k_sb,
                                              vcol0=lay.v_sb, scale=1.0 / math.sqrt(128), name="sb_bwd_swap",
                                              ride=_swap_rows_ride(carry))
        pieces, own = _reduce_rows_begin(carry, sib, "w_in")
        dqc, dkc, dv, got = _mla_bwd(qc, kc, v, o_mla, do_mla, lse, H=H, scale=1.0 / math.sqrt(128 + ROPE_DIM),
                                     name="mla_bwd_scatter", pieces=pieces)
        g_w_in_above = _reduce_rows_end(own, got, "w_in")
    dq, dkv, dkr = _rope_bwd(dqc, dkc, dv, cosp, sinp, H=H, name="rope_bwd")
    d_w_uq = _matmul(cqn, dq, mode="tn", out_dtype=F32, name="mm_uq_dw")
    dcqn = _matmul(dq, w_uq, mode="nt", out_dtype=F32, name="mm_uq_dx")
    d_w_ukv = _matmul(ckvn, dkv, mode="tn", out_dtype=F32, name="mm_ukv_dw")
    dckvn = _matmul(dkv, w_ukv, mode="nt", out_dtype=F32, name="mm_ukv_dx")
    dcq, dg_q = _rms_bwd(proj, g_q, dcqn, col0=lay.c_q, out_dtype=BF16, name="rms_cq_bwd")
    dckv, dg_kv = _rms_bwd(proj, g_kv, dckvn, col0=lay.c_kv, out_dtype=BF16, name="rms_ckv_bwd")
    pad = jnp.zeros((dh.shape[0], lay.width - lay.k_r - 256), BF16)
    dproj = jnp.concatenate([dz_mla, dq_sb, dk_sb, dv_sb, dz_sb, dcq, dckv, dkr, pad], axis=1)
    du = _matmul(dproj, w_in, mode="nt", out_dtype=F32, name="mm_in_dx")
    dkr_o = (dkr[:, :ROPE_DIM].astype(F32) + _unrot(dkr[:, 128:128 + ROPE_DIM].astype(F32))).astype(BF16)
    pad_o = jnp.zeros((dh.shape[0], lay.width - lay.din), BF16)
    dproj_o = jnp.concatenate([dcq, dckv, dkr_o, dz_mla, dq_sb, dk_sb, dv_sb, dz_sb, pad_o], axis=1)
    d_w_in = _matmul(u, dproj_o, mode="tn", out_dtype=F32, name="mm_in_dw", out_cols=lay.din)
    dh_prev, dg_norm = _rms_bwd(h, g_norm, du, col0=0, out_dtype=F32, name="rms_h_bwd", residual=dh)
    grads = (dg_norm[0], None, dg_q[0], dg_kv[0], lay.unpack_dw_uq(d_w_uq), d_w_ukv, dg_mla[0], dg_sb[0], d_w_o)
    return dh_prev, grads, d_w_in, g_w_in_above


def _halves(a):
    return a.reshape((2, a.shape[0] // 2) + a.shape[1:])


def _gather_cols(w, name):
    L, K, n = w.shape
    own = w.astype(BF16)
    g = _with_own(_gather_shards(_halves(own), name).reshape(4, L, K, n), own)
    return [jnp.concatenate([g[q][l] for q in range(4)], axis=1) for l in range(L)]


def _cut_cols(g):
    L, K, N = g.shape
    return g.reshape(2, L // 2, K, 4, N // 4).transpose(0, 3, 1, 2, 4)


def kernel(x, meta_tokens, g_norm, w_in, g_q, g_kv, w_uq, w_ukv, g_out_mla, g_out_sb, w_o, g_final, loss_target, m_meta_tokens, m_g_norm, m_w_in, m_g_q, m_g_kv, m_w_uq, m_w_ukv, m_g_out_mla, m_g_out_sb, m_w_o, m_g_final, v_meta_tokens, v_g_norm, v_w_in, v_g_q, v_g_kv, v_w_uq, v_w_ukv, v_g_out_mla, v_g_out_sb, v_w_o, v_g_final):
    _, S, D = x.shape
    NM = meta_tokens.shape[0]
    L = g_norm.shape[0]
    lay = _Layout(D, g_q.shape[1], g_kv.shape[1])
    TP = -(-(NM + S) // ROW_ALIGN) * ROW_ALIGN
    tabs = _rope_tables(TP)

    w_in_b, w_o_b = w_in.astype(BF16), w_o.astype(BF16)

    def row_halves(a):
        return a.reshape((2, a.shape[0] // 2) + a.shape[1:])

    def first_layer(w, name):
        g = _gather_shards(row_halves(w), name)
        return _with_own(g.reshape((4,) + w.shape), w)

    w_in_0 = lay.pack_w_in(jnp.concatenate(first_layer(w_in_b[0], "gather_w_in"), axis=1))
    w_o_0 = jnp.concatenate(first_layer(w_o_b[0], "gather_w_o"), axis=0)
    w_uq_full = [lay.pack_w_uq(w) for w in _gather_cols(w_uq, "gather_w_uq")]
    w_ukv_full = _gather_cols(w_ukv, "gather_w_ukv")
    meta_g = _with_own(_gather_shards(meta_tokens.reshape(2, NM // 2, -1), "gather_meta").reshape(4, NM, -1),
                       meta_tokens)
    meta_full = jnp.concatenate(meta_g, axis=1)

    h = jnp.concatenate([meta_full, x[0], jnp.zeros((TP - NM - S, D), F32)], axis=0)
    target = jnp.pad(loss_target[0], ((NM, TP - NM - S), (0, 0)))
    weights, saved = [], []
    w_in_l, w_o_l = w_in_0, w_o_0
    for l in range(L):
        weights.append((g_norm[l], w_in_l, g_q[l], g_kv[l], w_uq_full[l], w_ukv_full[l], g_out_mla[l],
                        g_out_sb[l], w_o_l))
        nxt = [row_halves(w_in_b[l + 1]), row_halves(w_o_b[l + 1])] if l + 1 < L else None
        h, s, gathered = _layer_fwd(h, weights[l], lay, tabs, nxt)
        saved.append(s)
        if gathered is not None:
            (in_mine, o_mine), (in_other, o_other) = gathered
            w_in_l = lay.pack_w_in(jnp.concatenate(_assemble_shards(w_in_b[l + 1], in_mine, in_other), axis=1))
            w_o_l = jnp.concatenate(_assemble_shards(w_o_b[l + 1], o_mine, o_other), axis=0)
    dh, dg_final, loss_part = _final_loss(h, g_final, target, row0=NM, n_rows=S, name="final_loss")

    layer_grads = [None] * L
    g_w_in_layers = [None] * L
    carry = None
    for l in reversed(range(L)):
        dh, layer_grads[l], carry, g_above = _layer_bwd(dh, saved[l], weights[l], lay, tabs, carry)
        if g_above is not None:
            g_w_in_layers[l + 1] = g_above
    pieces_in0, own_in0 = _reduce_rows_begin(carry, _swap_rows(carry, "swap_w_in"), "w_in")
    grad_x = dh[NM:NM + S][None]
    d_meta = dh[:NM]

    def stack(i):
        return jnp.stack([layer_grads[l][i] for l in range(L)], axis=0)

    small = [stack(0), stack(2), stack(3), stack(6), stack(7), dg_final[0]]
    flat = jnp.concatenate([s.reshape(-1) for s in small])
    n_flat = flat.shape[0]
    rows = -(-n_flat // (8 * 128)) * 8
    packed = jnp.pad(flat, (0, rows * 128 - n_flat)).reshape(rows, 128)
    summed = _all_reduce_small(packed, "allreduce_gains").reshape(-1)
    small_red = []
    o = 0
    for s in small:
        small_red.append(summed[o:o + s.size].reshape(s.shape))
        o += s.size
    g_g_norm, g_g_q, g_g_kv, g_g_mla, g_g_sb, g_g_final = small_red

    g_w_uq =_reduce_to_shard(_cut_cols(stack(4)), "w_uq").reshape(w_uq.shape)
    g_w_ukv = _reduce_to_shard(_cut_cols(stack(5)), "w_ukv").reshape(w_ukv.shape)
    d_w_o = stack(8).reshape(2, L // 2, 4, w_o.shape[1], D).transpose(0, 2, 1, 3, 4)
    psum_w_o, pieces_w_o, got_in0 = _reduce_to_shard_begin(d_w_o, "w_o", ride=_scatter_ride(pieces_in0))
    g_w_in_layers[0] = _reduce_rows_end(own_in0, got_in0, "w_in")
    g_w_in = jnp.stack(g_w_in_layers, axis=0)
    d_meta = d_meta.reshape(2, NM // 2, 4, D // 4).transpose(0, 2, 1, 3)
    g_meta = _reduce_to_shard(d_meta, "meta").reshape(meta_tokens.shape)

    loss = lax.psum(loss_part[0, 0], ("x", "y", "c"))

    names = ["meta", "g_norm", "w_in", "g_q", "g_kv", "w_uq", "w_ukv", "g_out_mla", "g_out_sb", "w_o", "g_final"]
    ws = [meta_tokens, g_norm, w_in, g_q, g_kv, w_uq, w_ukv, g_out_mla, g_out_sb, w_o, g_final]
    gs = [g_meta, g_g_norm, g_w_in, g_g_q, g_g_kv, g_w_uq, g_w_ukv, g_g_mla, g_g_sb, None, g_g_final]
    ms = [m_meta_tokens, m_g_norm, m_w_in, m_g_q, m_g_kv, m_w_uq, m_w_ukv, m_g_out_mla, m_g_out_sb, m_w_o, m_g_final]
    vs = [v_meta_tokens, v_g_norm, v_w_in, v_g_q, v_g_kv, v_w_uq, v_w_ukv, v_g_out_mla, v_g_out_sb, v_w_o, v_g_final]
    deltas, new_m, new_v = [], [], []
    for i, (n, w, m, v) in enumerate(zip(names, ws, ms, vs)):
        g = gs[i]
        shape = w.shape
        if w.ndim == 1:
            w, g, m, v = (a.reshape(1, -1) for a in (w, g, m, v))
        flip = w.ndim == 3 and w.shape[2] % V7X_LANES != 0
        if flip:
            w, g, m, v = (jnp.swapaxes(a, 1, 2) for a in (w, g, m, v))
        if n == "w_in":
            d, nm, nv, got_w_o = _adamw(w, g, m, v, f"adamw_{n}", ride=_scatter_ride(pieces_w_o))
            gs[names.index("w_o")] = _reduce_to_shard_end(psum_w_o, got_w_o, "w_o").reshape(w_o.shape)
        else:
            d, nm, nv = _adamw(w, g, m, v, f"adamw_{n}")
        if flip:
            d, nm, nv = (jnp.swapaxes(a, 1, 2) for a in (d, nm, nv))
        deltas.append(d.reshape(shape))
        new_m.append(nm.reshape(shape))
        new_v.append(nv.reshape(shape))
    return (loss, grad_x, *gs, *deltas, *new_m, *new_v)
```
